```python
import math, functools
import jax, jax.numpy as jnp
from jax import lax
import numpy as np

D_MODEL = 4096
BATCH = 8
SEQ = 2048
DEPTH = 2
DEC_BATCH = 8
DEC_SEQ = 16
PAST_LEN = 4096

CHUNK = 64
Q_BLOCK = 128
ROPE_THETA = 10000.0
EPS = 1e-6
MIX_WIDTH = D_MODEL
POOL_WINDOWS = (2, 4, 8, 16)
POOL_WIDTH = MIX_WIDTH // 4
POOL_GROUP = POOL_WIDTH // len(POOL_WINDOWS)
POOL_STATE = max(POOL_WINDOWS) - 1
DIFF_D = 64
DIFF_HD = 2 * DIFF_D
DIFF_HEADS = MIX_WIDTH // 4 // DIFF_HD
DIFF_WIDTH = DIFF_HEADS * DIFF_HD
MLA_NOPE = 128
MLA_ROPE = 64
MLA_V = 128
MLA_HEADS = MIX_WIDTH // 2 // MLA_V
MLA_Q_RANK = D_MODEL // 4
MLA_KV_RANK = 512
MLA_WIDTH = MLA_HEADS * MLA_V
MEM_LEN = 256
MEM_HEADS = 4
MEM_HD = 128
MEM_WIDTH = MEM_HEADS * MEM_HD
D_FF = ((-(-8 * D_MODEL // 3)) + 255) // 256 * 256
IN_SPLITS = (POOL_WIDTH, DIFF_WIDTH, DIFF_WIDTH, DIFF_WIDTH, MLA_Q_RANK, MLA_KV_RANK, MLA_ROPE)
IN_WIDTH = sum(IN_SPLITS)

kernel_name = 'hybrid_pool_diff_mla_streaming_step'


def rms_norm(x, g):
    xf = x.astype(jnp.float32)
    y = xf * lax.rsqrt(jnp.mean(xf * xf, axis=-1, keepdims=True) + EPS)
    return (y * g.astype(jnp.float32)).astype(x.dtype)


def rope(x, pos):
    half = x.shape[-1] // 2
    inv = ROPE_THETA ** (-jnp.arange(half, dtype=jnp.float32) / half)
    ang = pos.astype(jnp.float32)[:, None] * inv[None, :]
    bshape = (pos.shape[0],) + (1,) * (x.ndim - 3) + (half,)
    cos = jnp.cos(ang).reshape(bshape)
    sin = jnp.sin(ang).reshape(bshape)
    xf = x.astype(jnp.float32)
    x1, x2 = xf[..., :half], xf[..., half:]
    return jnp.concatenate([x1 * cos - x2 * sin, x2 * cos + x1 * sin], axis=-1).astype(x.dtype)


def chunk_mask(qpos, kpos):
    return (kpos[None, :] // CHUNK) <= (qpos[:, None] // CHUNK)


def diff_attend(q, k, v, qpos, kpos, lam):
    s = jnp.einsum('bqhcd,bkhcd->bhcqk', q, k).astype(jnp.float32) * (DIFF_D ** -0.5)
    s = jnp.where(chunk_mask(qpos, kpos), s, -jnp.inf)
    pr = jax.nn.softmax(s, axis=-1)
    w = pr[:, :, 0] - lam * pr[:, :, 1]
    return jnp.einsum('bhqk,bkhe->bqhe', w.astype(v.dtype), v)


def mla_attend(q_lat, q_pe, ckv, kpe, qpos, kpos):
    s = (jnp.einsum('bqhr,bkr->bhqk', q_lat, ckv) + jnp.einsum('bqhp,bkp->bhqk', q_pe, kpe)).astype(jnp.float32)
    s = jnp.where(chunk_mask(qpos, kpos), s * ((MLA_NOPE + MLA_ROPE) ** -0.5), -jnp.inf)
    pr = jax.nn.softmax(s, axis=-1)
    return jnp.einsum('bhqk,bkr->bqhr', pr.astype(ckv.dtype), ckv)


def prompt_blocks(fn, qs, ks, pos):
    T = pos.shape[0]
    outs = []
    for lo in range(0, T, Q_BLOCK):
        hi = min(lo + Q_BLOCK, T)
        outs.append(fn(*[q[:, lo:hi] for q in qs], *[k[:, :hi] for k in ks], pos[lo:hi], pos[:hi]))
    return jnp.concatenate(outs, axis=1)


def pool_mix(u_ext, pos, w_pool, scale):
    B, L, C = u_ext.shape
    T = L - POOL_STATE
    cs = jnp.cumsum(u_ext.astype(jnp.float32), axis=1)
    cs = jnp.concatenate([jnp.zeros((B, 1, C), jnp.float32), cs], axis=1)
    end = cs[:, POOL_STATE + 1:]
    u = u_ext[:, POOL_STATE:].astype(jnp.float32)
    outs = []
    for g, w in enumerate(POOL_WINDOWS):
        sl = slice(g * POOL_GROUP, (g + 1) * POOL_GROUP)
        start = cs[:, POOL_STATE + 1 - w:POOL_STATE + 1 - w + T, sl]
        cnt = jnp.minimum(w, pos + 1).astype(jnp.float32)[None, :, None]
        outs.append((end[..., sl] - start) / cnt - u[..., sl])
    d = jnp.stack(outs, axis=2).astype(u_ext.dtype)
    y = jnp.einsum('btgc,gce->btge', d, w_pool).reshape(B, T, C)
    return y * scale


def memory_kv(mem, g, wk, wv):
    B, M, _ = mem.shape
    m = rms_norm(mem, g)
    return (m @ wk).reshape(B, M, MEM_HEADS, MEM_HD), (m @ wv).reshape(B, M, MEM_HEADS, MEM_HD)


def cross_attend(h, mk, mv, wq, wo):
    B, T, _ = h.shape
    q = (h @ wq).reshape(B, T, MEM_HEADS, MEM_HD)
    s = jnp.einsum('bqhd,bkhd->bhqk', q, mk).astype(jnp.float32) * (MEM_HD ** -0.5)
    pr = jax.nn.softmax(s, axis=-1)
    o = jnp.einsum('bhqk,bkhd->bqhd', pr.astype(mv.dtype), mv).reshape(B, T, MEM_WIDTH)
    return o @ wo


def hybrid_layer(p, l, x, pos, mem_k, mem_v, past):
    B, T, _ = x.shape
    lam_init = 0.8 - 0.6 * math.exp(-0.3 * l)
    h = rms_norm(x, p['g_mix_pre'][l])
    z = h @ p['w_in'][l]
    offs = np.cumsum(IN_SPLITS)[:-1].tolist()
    u, qd, kd, vd, cq, ckv, kpe = jnp.split(z, offs, axis=-1)
    u_prev = jnp.zeros((B, POOL_STATE, POOL_WIDTH), u.dtype) if past is None else past['pool']
    u_ext = jnp.concatenate([u_prev, u], axis=1)
    y_pool = pool_mix(u_ext, pos, p['w_pool'][l], p['pool_scale'][l])
    new_pool = u_ext[:, -POOL_STATE:]
    qd = rope(qd.reshape(B, T, DIFF_HEADS, 2, DIFF_D), pos)
    kd = rope(kd.reshape(B, T, DIFF_HEADS, 2, DIFF_D), pos)
    vd = vd.reshape(B, T, DIFF_HEADS, DIFF_HD)
    lq = p['diff_lambda'][l].astype(jnp.float32)
    lam = jnp.exp(jnp.sum(lq[0] * lq[1])) - jnp.exp(jnp.sum(lq[2] * lq[3])) + lam_init
    diff_fn = functools.partial(diff_attend, lam=lam)
    q = (rms_norm(cq, p['g_mla_q'][l]) @ p['w_mla_uq'][l]).reshape(B, T, MLA_HEADS, MLA_NOPE + MLA_ROPE)
    q_lat = jnp.einsum('bthn,rhn->bthr', q[..., :MLA_NOPE], p['w_mla_uk'][l])
    q_pe = rope(q[..., MLA_NOPE:], pos)
    ckv = rms_norm(ckv, p['g_mla_kv'][l])
    kpe = rope(kpe, pos)
    if past is None:
        o_diff = prompt_blocks(diff_fn, (qd,), (kd, vd), pos)
        o_lat = prompt_blocks(mla_attend, (q_lat, q_pe), (ckv, kpe), pos)
    else:
        P = past['ckv'].shape[1]
        kpos = jnp.arange(P + T)
        kd_all = jnp.concatenate([past['diff_k'].reshape(B, P, DIFF_HEADS, 2, DIFF_D), kd], axis=1)
        vd_all = jnp.concatenate([past['diff_v'], vd], axis=1)
        o_diff = diff_fn(qd, kd_all, vd_all, pos, kpos)
        o_lat = mla_attend(q_lat, q_pe, jnp.concatenate([past['ckv'], ckv], axis=1),
                           jnp.concatenate([past['kpe'], kpe], axis=1), pos, kpos)
    o_diff = rms_norm(o_diff, p['g_diff_sub'][l]) * (1.0 - lam_init)
    o_mla = jnp.einsum('bthr,rhv->bthv', o_lat, p['w_mla_uv'][l])
    mix = jnp.concatenate([y_pool, o_diff.reshape(B, T, DIFF_WIDTH), o_mla.reshape(B, T, MLA_WIDTH)], axis=-1)
    x = x + rms_norm(mix @ p['w_out'][l], p['g_mix_post'][l])
    h = rms_norm(x, p['g_x_pre'][l])
    x = x + rms_norm(cross_attend(h, mem_k, mem_v, p['w_mem_q'][l], p['w_mem_o'][l]), p['g_x_post'][l])
    h = rms_norm(x, p['g_ff_pre'][l])
    f = (jax.nn.silu(h @ p['w_gate'][l]) * (h @ p['w_up'][l])) @ p['w_down'][l]
    x = x + rms_norm(f, p['g_ff_post'][l])
    new = (kd.reshape(B, T, DIFF_HEADS, DIFF_HD), vd, ckv, kpe, new_pool)
    return x, new


def setup_inputs(seed: int = 0) -> dict:
    key = jax.random.key(seed)
    ks = iter(jax.random.split(key, 48))

    def nrm(shape, scale=1.0):
        return jax.random.normal(next(ks), shape, jnp.float32) * scale

    def gain(shape):
        return 1.0 + 0.05 * nrm(shape)

    L, D = DEPTH, D_MODEL
    return {
        'x_prompt': nrm((BATCH, SEQ, D)),
        'x_sample': nrm((DEC_BATCH, DEC_SEQ, D)),
        'cache_diff_k': nrm((L, DEC_BATCH, PAST_LEN, DIFF_HEADS, DIFF_HD)),
        'cache_diff_v': nrm((L, DEC_BATCH, PAST_LEN, DIFF_HEADS, DIFF_HD)),
        'cache_mla_ckv': nrm((L, DEC_BATCH, PAST_LEN, MLA_KV_RANK)),
        'cache_mla_kpe': nrm((L, DEC_BATCH, PAST_LEN, MLA_ROPE)),
        'cache_pool': nrm((L, DEC_BATCH, POOL_STATE, POOL_WIDTH)),
        'cache_mem_k': nrm((L, DEC_BATCH, MEM_LEN, MEM_HEADS, MEM_HD)),
        'cache_mem_v': nrm((L, DEC_BATCH, MEM_LEN, MEM_HEADS, MEM_HD)),
        'mem_prompt': nrm((BATCH, MEM_LEN, D)),
        'g_mix_pre': gain((L, D)),
        'w_in': nrm((L, D, IN_WIDTH), D ** -0.5),
        'w_pool': nrm((L, len(POOL_WINDOWS), POOL_GROUP, POOL_GROUP), POOL_GROUP ** -0.5),
        'pool_scale': 1.0 + 0.1 * nrm((L, POOL_WIDTH)),
        'diff_lambda': nrm((L, 4, DIFF_D), 0.1),
        'g_diff_sub': gain((L, DIFF_HD)),
        'g_mla_q': gain((L, MLA_Q_RANK)),
        'w_mla_uq': nrm((L, MLA_Q_RANK, MLA_HEADS * (MLA_NOPE + MLA_ROPE)), MLA_Q_RANK ** -0.5),
        'w_mla_uk': nrm((L, MLA_KV_RANK, MLA_HEADS, MLA_NOPE), MLA_KV_RANK ** -0.5),
        'w_mla_uv': nrm((L, MLA_KV_RANK, MLA_HEADS, MLA_V), MLA_KV_RANK ** -0.5),
        'g_mla_kv': gain((L, MLA_KV_RANK)),
        'w_out': nrm((L, MIX_WIDTH, D), MIX_WIDTH ** -0.5),
        'g_mix_post': gain((L, D)),
        'g_mem': gain((L, D)),
        'w_mem_k': nrm((L, D, MEM_WIDTH), D ** -0.5),
        'w_mem_v': nrm((L, D, MEM_WIDTH), D ** -0.5),
        'w_mem_q': nrm((L, D, MEM_WIDTH), D ** -0.5),
        'w_mem_o': nrm((L, MEM_WIDTH, D), MEM_WIDTH ** -0.5),
        'g_x_pre': gain((L, D)),
        'g_x_post': gain((L, D)),
        'g_ff_pre': gain((L, D)),
        'w_gate': nrm((L, D, D_FF), D ** -0.5),
        'w_up': nrm((L, D, D_FF), D ** -0.5),
        'w_down': nrm((L, D_FF, D), D_FF ** -0.5),
        'g_ff_post': gain((L, D)),
    }


def reference(x_prompt, x_sample, cache_diff_k, cache_diff_v, cache_mla_ckv, cache_mla_kpe, cache_pool,
              cache_mem_k, cache_mem_v, mem_prompt, g_mix_pre, w_in, w_pool, pool_scale, diff_lambda,
              g_diff_sub, g_mla_q, w_mla_uq, w_mla_uk, w_mla_uv, g_mla_kv, w_out, g_mix_post, g_mem,
              w_mem_k, w_mem_v, w_mem_q, w_mem_o, g_x_pre, g_x_post, g_ff_pre, w_gate, w_up, w_down,
              g_ff_post):
    p = {'g_mix_pre': g_mix_pre, 'w_in': w_in, 'w_pool': w_pool, 'pool_scale': pool_scale,
         'diff_lambda': diff_lambda, 'g_diff_sub': g_diff_sub, 'g_mla_q': g_mla_q, 'w_mla_uq': w_mla_uq,
         'w_mla_uk': w_mla_uk, 'w_mla_uv': w_mla_uv, 'g_mla_kv': g_mla_kv, 'w_out': w_out,
         'g_mix_post': g_mix_post, 'w_mem_q': w_mem_q, 'w_mem_o': w_mem_o, 'g_x_pre': g_x_pre,
         'g_x_post': g_x_post, 'g_ff_pre': g_ff_pre, 'w_gate': w_gate, 'w_up': w_up, 'w_down': w_down,
         'g_ff_post': g_ff_post}
    pos_p = jnp.arange(x_prompt.shape[1])
    pos_s = cache_mla_ckv.shape[2] + jnp.arange(x_sample.shape[1])
    xp, xs = x_prompt, x_sample
    pk, pv, pc, pe, pp, pmk, pmv = [], [], [], [], [], [], []
    sk, sv, sc, se, sp = [], [], [], [], []
    for l in range(DEPTH):
        mk, mv = memory_kv(mem_prompt, g_mem[l], w_mem_k[l], w_mem_v[l])
        xp, (a, b, c, d, e) = hybrid_layer(p, l, xp, pos_p, mk, mv, None)
        pk.append(a); pv.append(b); pc.append(c); pe.append(d); pp.append(e); pmk.append(mk); pmv.append(mv)
        past = {'diff_k': cache_diff_k[l], 'diff_v': cache_diff_v[l], 'ckv': cache_mla_ckv[l],
                'kpe': cache_mla_kpe[l], 'pool': cache_pool[l]}
        xs, (a, b, c, d, e) = hybrid_layer(p, l, xs, pos_s, cache_mem_k[l], cache_mem_v[l], past)
        sk.append(a); sv.append(b); sc.append(c); se.append(d); sp.append(e)
    return (xp, xs,
            jnp.stack(pk), jnp.stack(pv), jnp.stack(pc), jnp.stack(pe), jnp.stack(pp),
            jnp.stack(pmk), jnp.stack(pmv),
            jnp.stack(sk), jnp.stack(sv), jnp.stack(sc), jnp.stack(se), jnp.stack(sp))
```

```python
import functools
import math

import jax
import jax.numpy as jnp
from jax import lax
from jax.experimental import pallas as pl
from jax.experimental.pallas import tpu as pltpu

F32 = jnp.float32
BF16 = jnp.bfloat16

EPS = 1e-6
CHUNK = 64
ROPE_THETA = 10000.0
POOL_WINDOWS = (2, 4, 8, 16)
POOL_HALO = 16
DIFF_D = 64
MLA_NOPE = 128
MLA_ROPE = 64
LANES = 128
MIB = 1024 * 1024
VMEM_LIMIT = 56 * MIB

_NT = (((1,), (1,)), ((), ()))


def _cparams(*sem):
    return pltpu.CompilerParams(dimension_semantics=sem, vmem_limit_bytes=VMEM_LIMIT)


def _rms(x, g):
    return x * lax.rsqrt(jnp.mean(x * x, axis=-1, keepdims=True) + EPS) * g


def _rmsnorm_kernel(x_ref, g_ref, o_ref):
    o_ref[...] = _rms(x_ref[...], g_ref[...]).astype(o_ref.dtype)


def rmsnorm_bf16(x, g, tm):
    m, d = x.shape
    return pl.pallas_call(
        _rmsnorm_kernel,
        grid=(m // tm,),
        in_specs=[pl.BlockSpec((tm, d), lambda i: (i, 0)), pl.BlockSpec((1, d), lambda i: (0, 0))],
        out_specs=pl.BlockSpec((tm, d), lambda i: (i, 0)),
        out_shape=jax.ShapeDtypeStruct((m, d), BF16),
        compiler_params=_cparams("parallel"),
        name="rmsnorm",
    )(x, g.reshape(1, d))


def _post_res_kernel(y_ref, x_ref, gp_ref, gn_ref, xo_ref, ho_ref):
    x = x_ref[...] + _rms(y_ref[...], gp_ref[...])
    xo_ref[...] = x
    ho_ref[...] = _rms(x, gn_ref[...]).astype(ho_ref.dtype)


def _post_res_last_kernel(y_ref, x_ref, gp_ref, xo_ref):
    xo_ref[...] = x_ref[...] + _rms(y_ref[...], gp_ref[...])


def post_res(y, x, g_post, g_next, tm):
    m, d = x.shape
    row = pl.BlockSpec((tm, d), lambda i: (i, 0))
    vec = pl.BlockSpec((1, d), lambda i: (0, 0))
    if g_next is None:
        return pl.pallas_call(
            _post_res_last_kernel, grid=(m // tm,), in_specs=[row, row, vec], out_specs=row,
            out_shape=jax.ShapeDtypeStruct((m, d), F32), compiler_params=_cparams("parallel"),
            name="post_res_last",
        )(y, x, g_post.reshape(1, d)), None
    return pl.pallas_call(
        _post_res_kernel, grid=(m // tm,), in_specs=[row, row, vec, vec], out_specs=[row, row],
        out_shape=[jax.ShapeDtypeStruct((m, d), F32), jax.ShapeDtypeStruct((m, d), BF16)],
        compiler_params=_cparams("parallel"), name="post_res",
    )(y, x, g_post.reshape(1, d), g_next.reshape(1, d))


def _mm_kernel(x_ref, w_ref, o_ref):
    o_ref[...] = jnp.dot(x_ref[...], w_ref[...], preferred_element_type=F32).astype(o_ref.dtype)


def matmul(x, w, out_dtype, tm, tn, weights_outer=True):
    m, k = x.shape
    n = w.shape[1]
    tm, tn = min(tm, m), min(tn, n)
    if weights_outer:
        grid = (n // tn, m // tm)
        xmap, wmap, omap = (lambda j, i: (i, 0)), (lambda j, i: (0, j)), (lambda j, i: (i, j))
    else:
        grid = (m // tm, n // tn)
        xmap, wmap, omap = (lambda i, j: (i, 0)), (lambda i, j: (0, j)), (lambda i, j: (i, j))
    return pl.pallas_call(
        _mm_kernel, grid=grid,
        in_specs=[pl.BlockSpec((tm, k), xmap), pl.BlockSpec((k, tn), wmap)],
        out_specs=pl.BlockSpec((tm, tn), omap),
        out_shape=jax.ShapeDtypeStruct((m, n), out_dtype),
        compiler_params=_cparams("parallel", "parallel"), name="matmul",
    )(x, w)


def _swiglu_kernel(x_ref, wg_ref, wu_ref, o_ref):
    x = x_ref[...]
    g = jnp.dot(x, wg_ref[...], preferred_element_type=F32)
    u = jnp.dot(x, wu_ref[...], preferred_element_type=F32)
    o_ref[...] = (g * jax.nn.sigmoid(g) * u).astype(o_ref.dtype)


def matmul_swiglu(x, wg, wu, tm, tn):
    m, k = x.shape
    n = wg.shape[1]
    tm = min(tm, m)
    wspec = pl.BlockSpec((k, tn), lambda i, j: (0, j))
    return pl.pallas_call(
        _swiglu_kernel, grid=(m // tm, n // tn),
        in_specs=[pl.BlockSpec((tm, k), lambda i, j: (i, 0)), wspec, wspec],
        out_specs=pl.BlockSpec((tm, tn), lambda i, j: (i, j)),
        out_shape=jax.ShapeDtypeStruct((m, n), BF16),
        compiler_params=_cparams("parallel", "parallel"), name="matmul_swiglu",
    )(x, wg, wu)


def _rope_chunk(x, cos, sa, sb):
    return x * cos + pltpu.roll(x, LANES - 32, 1) * sa + pltpu.roll(x, 32, 1) * sb


def _split_kernel(z_ref, cos_ref, sa_ref, sb_ref, gq_ref, gkv_ref,
                  qd_ref, kd32_ref, kd16_ref, vd32_ref, vd16_ref, cq_ref, ckv32_ref, ckv16_ref,
                  kpe32_ref, kpe16_ref, *, offs, q_scale):
    o_qd, o_kd, o_vd, o_cq, o_ckv, o_kpe = offs
    cos, sa, sb = cos_ref[...], sa_ref[...], sb_ref[...]
    width = vd32_ref.shape[1]
    for c in range(width // LANES):
        sl = slice(c * LANES, (c + 1) * LANES)
        q = _rope_chunk(z_ref[:, o_qd + c * LANES:o_qd + (c + 1) * LANES], cos, sa, sb)
        qd_ref[:, sl] = (q * q_scale).astype(BF16)
        k = _rope_chunk(z_ref[:, o_kd + c * LANES:o_kd + (c + 1) * LANES], cos, sa, sb)
        kd32_ref[:, sl] = k
        kd16_ref[:, sl] = k.astype(BF16)
    v = z_ref[:, o_vd:o_vd + width]
    vd32_ref[...] = v
    vd16_ref[...] = v.astype(BF16)
    cq_ref[...] = _rms(z_ref[:, o_cq:o_cq + cq_ref.shape[1]], gq_ref[...]).astype(BF16)
    ckv = _rms(z_ref[:, o_ckv:o_ckv + ckv32_ref.shape[1]], gkv_ref[...])
    ckv32_ref[...] = ckv
    ckv16_ref[...] = ckv.astype(BF16)
    kpe = _rope_chunk(z_ref[:, o_kpe:o_kpe + LANES], cos, sa, sb)[:, :MLA_ROPE]
    kpe32_ref[...] = kpe
    kpe16_ref[...] = kpe.astype(BF16)


def split_projection(z, tabs, g_q, g_kv, dims, t_len, tt):
    m, nz = z.shape
    pool_w, diff_w, q_rank, kv_rank = dims
    offs = (pool_w, pool_w + diff_w, pool_w + 2 * diff_w, pool_w + 3 * diff_w,
            pool_w + 3 * diff_w + q_rank, pool_w + 3 * diff_w + q_rank + kv_rank)
    nt = t_len // tt
    row = lambda w: pl.BlockSpec((tt, w), lambda i: (i, 0))
    tab = pl.BlockSpec((tt, LANES), lambda i: (i % nt, 0))
    vec = lambda w: pl.BlockSpec((1, w), lambda i: (0, 0))
    sds = lambda w, dt: jax.ShapeDtypeStruct((m, w), dt)
    return pl.pallas_call(
        functools.partial(_split_kernel, offs=offs, q_scale=DIFF_D ** -0.5),
        grid=(m // tt,),
        in_specs=[row(nz), tab, tab, tab, vec(q_rank), vec(kv_rank)],
        out_specs=[row(diff_w), row(diff_w), row(diff_w), row(diff_w), row(diff_w), row(q_rank),
                   row(kv_rank), row(kv_rank), row(MLA_ROPE), row(MLA_ROPE)],
        out_shape=[sds(diff_w, BF16), sds(diff_w, F32), sds(diff_w, BF16), sds(diff_w, F32), sds(diff_w, BF16),
                   sds(q_rank, BF16), sds(kv_rank, F32), sds(kv_rank, BF16), sds(MLA_ROPE, F32),
                   sds(MLA_ROPE, BF16)],
        compiler_params=_cparams("parallel"), name="split_projection",
    )(z, *tabs, g_q.reshape(1, q_rank), g_kv.reshape(1, kv_rank))


def _pool_kernel(z_ref, halo_ref, prev_ref, wp_ref, sc_ref, o_ref, ext_ref, *, tt, pos0):
    i = pl.program_id(1)
    ext_ref[0:POOL_HALO, :] = jnp.where(i == 0, prev_ref[0], halo_ref[...])
    ext_ref[POOL_HALO:POOL_HALO + tt, :] = z_ref[...]
    pos = pos0 + i * tt + lax.broadcasted_iota(jnp.int32, (tt, 1), 0)
    group = wp_ref.shape[1]
    for g, w in enumerate(POOL_WINDOWS):
        cs = slice(g * group, (g + 1) * group)
        s = ext_ref[POOL_HALO:POOL_HALO + tt, cs]
        u = s
        for j in range(1, w):
            s = s + ext_ref[POOL_HALO - j:POOL_HALO - j + tt, cs]
        cnt = jnp.minimum(w, pos + 1).astype(F32)
        d = s / cnt - u
        y = jnp.dot(d.astype(BF16), wp_ref[g], preferred_element_type=F32) * sc_ref[:, cs]
        o_ref[:, cs] = y.astype(o_ref.dtype)


def pool_mix(z, prev, w_pool, scale, batch, t_len, pos0, tt):
    m = z.shape[0]
    c = prev.shape[2]
    nt = t_len // tt
    hb = tt // POOL_HALO
    return pl.pallas_call(
        functools.partial(_pool_kernel, tt=tt, pos0=pos0),
        grid=(batch, nt),
        in_specs=[pl.BlockSpec((tt, c), lambda b, i: (b * nt + i, 0)),
                  pl.BlockSpec((POOL_HALO, c), lambda b, i: (jnp.maximum((b * nt + i) * hb - 1, 0), 0)),
                  pl.BlockSpec((1, POOL_HALO, c), lambda b, i: (b, 0, 0)),
                  pl.BlockSpec(w_pool.shape, lambda b, i: (0, 0, 0)),
                  pl.BlockSpec((1, c), lambda b, i: (0, 0))],
        out_specs=pl.BlockSpec((tt, c), lambda b, i: (b * nt + i, 0)),
        out_shape=jax.ShapeDtypeStruct((m, c), BF16),
        scratch_shapes=[pltpu.VMEM((POOL_HALO + tt, c), F32)],
        compiler_params=_cparams("parallel", "parallel"), name="pool_mix",
    )(z, z, prev, w_pool, scale.reshape(1, c))


def _last_kv_block(i, tq, tkv, q0, s_len):
    last_q = q0 + (i + 1) * tq - 1
    visible = jnp.minimum((last_q // CHUNK + 1) * CHUNK, s_len)
    return (visible - 1) // tkv


def _mask(i, j, tq, tkv, q0, s_len):
    qpos = q0 + i * tq + lax.broadcasted_iota(jnp.int32, (tq, 1), 0)
    kpos = j * tkv + lax.broadcasted_iota(jnp.int32, (1, tkv), 1)
    shift = CHUNK.bit_length() - 1
    return (jnp.right_shift(kpos, shift) <= jnp.right_shift(qpos, shift)) & (kpos < s_len)


def _online_softmax_step(s, m_prev, l_prev):
    m_new = jnp.maximum(m_prev, jnp.max(s, axis=-1, keepdims=True))
    p = jnp.exp(s - m_new)
    alpha = jnp.exp(m_prev - m_new)
    return p, alpha, m_new, alpha * l_prev + jnp.sum(p, axis=-1, keepdims=True)


def _diff_kernel(lam_ref, q_ref, k_ref, v_ref, g_ref, o_ref, m_ref, l_ref, acc_ref, *,
                 tq, tkv, q0, s_len, out_scale):
    i, j = pl.program_id(2), pl.program_id(3)

    @pl.when(j == 0)
    def _():
        m_ref[...] = jnp.full(m_ref.shape, -jnp.inf, F32)
        l_ref[...] = jnp.zeros(l_ref.shape, F32)
        acc_ref[...] = jnp.zeros(acc_ref.shape, F32)

    @pl.when(j <= _last_kv_block(i, tq, tkv, q0, s_len))
    def _():
        q, k, v = q_ref[...], k_ref[...], v_ref[...]
        mask = _mask(i, j, tq, tkv, q0, s_len)
        lane = lax.broadcasted_iota(jnp.int32, q.shape, 1)
        for c in range(2):
            qc = jnp.where((lane >= c * DIFF_D) & (lane < (c + 1) * DIFF_D), q, jnp.zeros_like(q))
            s = lax.dot_general(qc, k, _NT, preferred_element_type=F32)
            s = jnp.where(mask, s, -jnp.inf)
            p, alpha, m_new, l_new = _online_softmax_step(s, m_ref[c], l_ref[c])
            acc_ref[c] = alpha * acc_ref[c] + jnp.dot(p.astype(BF16), v, preferred_element_type=F32)
            m_ref[c] = m_new
            l_ref[c] = l_new

    @pl.when(j == pl.num_programs(3) - 1)
    def _():
        o = acc_ref[0] / l_ref[0] - lam_ref[0, 0] * (acc_ref[1] / l_ref[1])
        o_ref[...] = (_rms(o, g_ref[...]) * out_scale).astype(o_ref.dtype)


def diff_attention(q, k, v, lam, g_sub, batch, heads, t_len, s_pad, s_len, q0, out_scale, tq, tkv):
    hd = 2 * DIFF_D
    nq, nkv = t_len // tq, s_pad // tkv
    last = functools.partial(_last_kv_block, tq=tq, tkv=tkv, q0=q0, s_len=s_len)
    kvmap = lambda b, h, i, j: (b * nkv + jnp.minimum(j, last(i)), h)
    return pl.pallas_call(
        functools.partial(_diff_kernel, tq=tq, tkv=tkv, q0=q0, s_len=s_len, out_scale=out_scale),
        grid=(batch, heads, nq, nkv),
        in_specs=[pl.BlockSpec(memory_space=pltpu.SMEM),
                  pl.BlockSpec((tq, hd), lambda b, h, i, j: (b * nq + i, h)),
                  pl.BlockSpec((tkv, hd), kvmap),
                  pl.BlockSpec((tkv, hd), kvmap),
                  pl.BlockSpec((1, hd), lambda b, h, i, j: (0, 0))],
        out_specs=pl.BlockSpec((tq, hd), lambda b, h, i, j: (b * nq + i, h)),
        out_shape=jax.ShapeDtypeStruct((batch * t_len, heads * hd), BF16),
        scratch_shapes=[pltpu.VMEM((2, tq, 1), F32), pltpu.VMEM((2, tq, 1), F32), pltpu.VMEM((2, tq, hd), F32)],
        compiler_params=_cparams("parallel", "parallel", "parallel", "arbitrary"), name="diff_attention",
    )(lam.reshape(1, 1), q, k, v, g_sub.reshape(1, hd))


def _mla_kernel(q_ref, cos_ref, sa_ref, sb_ref, wuk_ref, ckv_ref, kpe_ref, wuv_ref, o_ref,
                qlat_ref, qpe_ref, m_ref, l_ref, acc_ref, *, heads, tq, tkv, q0, s_len, scale):
    i, j = pl.program_id(1), pl.program_id(2)
    nope_w = heads * MLA_NOPE

    @pl.when(j == 0)
    def _():
        m_ref[...] = jnp.full(m_ref.shape, -jnp.inf, F32)
        l_ref[...] = jnp.zeros(l_ref.shape, F32)
        acc_ref[...] = jnp.zeros(acc_ref.shape, F32)
        cos, sa, sb = cos_ref[...], sa_ref[...], sb_ref[...]
        for h in range(heads):
            qn = q_ref[:, h * MLA_NOPE:(h + 1) * MLA_NOPE].astype(BF16)
            qlat_ref[h] = (jnp.dot(qn, wuk_ref[h], preferred_element_type=F32) * scale).astype(BF16)
        for c in range(heads * MLA_ROPE // LANES):
            pe = _rope_chunk(q_ref[:, nope_w + c * LANES:nope_w + (c + 1) * LANES], cos, sa, sb) * scale
            qpe_ref[2 * c] = pe[:, :MLA_ROPE].astype(BF16)
            qpe_ref[2 * c + 1] = pe[:, MLA_ROPE:].astype(BF16)

    @pl.when(j <= _last_kv_block(i, tq, tkv, q0, s_len))
    def _():
        ckv, kpe = ckv_ref[...], kpe_ref[...]
        mask = _mask(i, j, tq, tkv, q0, s_len)
        for h in range(heads):
            s = (lax.dot_general(qlat_ref[h], ckv, _NT, preferred_element_type=F32)
                 + lax.dot_general(qpe_ref[h], kpe, _NT, preferred_element_type=F32))
            s = jnp.where(mask, s, -jnp.inf)
            p, alpha, m_new, l_new = _online_softmax_step(s, m_ref[h], l_ref[h])
            acc_ref[h] = alpha * acc_ref[h] + jnp.dot(p.astype(BF16), ckv, preferred_element_type=F32)
            m_ref[h] = m_new
            l_ref[h] = l_new

    @pl.when(j == pl.num_programs(2) - 1)
    def _():
        v_w = wuv_ref.shape[2]
        for h in range(heads):
            o = (acc_ref[h] / l_ref[h]).astype(BF16)
            o_ref[:, h * v_w:(h + 1) * v_w] = jnp.dot(o, wuv_ref[h], preferred_element_type=F32).astype(o_ref.dtype)


def mla_attention(q, tabs, wuk_t, ckv, kpe, wuv, batch, t_len, s_pad, s_len, q0, tq, tkv):
    heads, _, rank = wuk_t.shape
    v_w = wuv.shape[2]
    nq, nkv = t_len // tq, s_pad // tkv
    last = functools.partial(_last_kv_block, tq=tq, tkv=tkv, q0=q0, s_len=s_len)
    kvmap = lambda b, i, j: (b * nkv + jnp.minimum(j, last(i)), 0)
    tab = pl.BlockSpec((tq, LANES), lambda b, i, j: (i, 0))
    full3 = lambda a: pl.BlockSpec(a.shape, lambda b, i, j: (0, 0, 0))
    return pl.pallas_call(
        functools.partial(_mla_kernel, heads=heads, tq=tq, tkv=tkv, q0=q0, s_len=s_len,
                          scale=(MLA_NOPE + MLA_ROPE) ** -0.5),
        grid=(batch, nq, nkv),
        in_specs=[pl.BlockSpec((tq, q.shape[1]), lambda b, i, j: (b * nq + i, 0)),
                  tab, tab, tab, full3(wuk_t),
                  pl.BlockSpec((tkv, rank), kvmap),
                  pl.BlockSpec((tkv, MLA_ROPE), kvmap),
                  full3(wuv)],
        out_specs=pl.BlockSpec((tq, heads * v_w), lambda b, i, j: (b * nq + i, 0)),
        out_shape=jax.ShapeDtypeStruct((batch * t_len, heads * v_w), BF16),
        scratch_shapes=[pltpu.VMEM((heads, tq, rank), BF16), pltpu.VMEM((heads, tq, MLA_ROPE), BF16),
                        pltpu.VMEM((heads, tq, 1), F32), pltpu.VMEM((heads, tq, 1), F32),
                        pltpu.VMEM((heads, tq, rank), F32)],
        compiler_params=_cparams("parallel", "parallel", "arbitrary"), name="mla_attention",
    )(q, *tabs, wuk_t, ckv, kpe, wuv)


def _cross_kernel(q_ref, k_ref, v_ref, o_ref, *, heads, hd):
    for h in range(heads):
        sl = slice(h * hd, (h + 1) * hd)
        s = lax.dot_general(q_ref[:, sl], k_ref[:, sl], _NT, preferred_element_type=F32) * (hd ** -0.5)
        p = jnp.exp(s - jnp.max(s, axis=-1, keepdims=True))
        o = jnp.dot(p.astype(BF16), v_ref[:, sl], preferred_element_type=F32)
        o_ref[:, sl] = (o / jnp.sum(p, axis=-1, keepdims=True)).astype(o_ref.dtype)


def cross_attention(q, mk, mv, batch, t_len, mem_len, heads, tq):
    w = q.shape[1]
    nq = t_len // tq
    kv = pl.BlockSpec((mem_len, w), lambda b, i: (b, 0))
    return pl.pallas_call(
        functools.partial(_cross_kernel, heads=heads, hd=w // heads),
        grid=(batch, nq),
        in_specs=[pl.BlockSpec((tq, w), lambda b, i: (b * nq + i, 0)), kv, kv],
        out_specs=pl.BlockSpec((tq, w), lambda b, i: (b * nq + i, 0)),
        out_shape=jax.ShapeDtypeStruct((batch * t_len, w), BF16),
        compiler_params=_cparams("parallel", "parallel"), name="cross_attention",
    )(q, mk, mv)


def _rope_tables(pos):
    half = DIFF_D // 2
    inv = ROPE_THETA ** (-jnp.arange(half, dtype=F32) / half)
    ang = pos.astype(F32)[:, None] * inv[None, :]
    cos, sin, zero = jnp.cos(ang), jnp.sin(ang), jnp.zeros_like(ang)
    reps = LANES // DIFF_D
    return (jnp.tile(cos, (1, 2 * reps)), jnp.tile(jnp.concatenate([-sin, zero], 1), (1, reps)),
            jnp.tile(jnp.concatenate([zero, sin], 1), (1, reps)))


def _round_up(a, b):
    return -(-a // b) * b


def _with_past(past, new, batch, s_pad):
    new = new.reshape(batch, -1, new.shape[-1])
    pad = s_pad - past.shape[1] - new.shape[1]
    parts = [past.astype(BF16), new] + ([jnp.zeros((batch, pad, new.shape[-1]), BF16)] if pad else [])
    return jnp.concatenate(parts, axis=1).reshape(batch * s_pad, new.shape[-1])


def _layer(wl, l, x, h, batch, t_len, pos0, tabs, mem_k, mem_v, past, g_next, tiles):
    tm, tt, tq_diff, tkv_diff, tq_mla, tkv_mla, tq_x = tiles
    lam_init = 0.8 - 0.6 * math.exp(-0.3 * l)
    pool_w, diff_w, q_rank, kv_rank = wl["dims"]
    heads_d = diff_w // (2 * DIFF_D)

    z = matmul(h, wl["w_in"], F32, tm, wl["tn_in"])
    qd, kd32, kd16, vd32, vd16, cqn, ckv32, ckv16, kpe32, kpe16 = split_projection(
        z, tabs, wl["g_mla_q"], wl["g_mla_kv"], wl["dims"], t_len, tt)

    u = z[:, :pool_w].reshape(batch, t_len, pool_w)
    zero_row = jnp.zeros((batch, 1, pool_w), F32)
    if past is None:
        prev = jnp.zeros((batch, POOL_HALO, pool_w), F32)
        s_len = s_pad = t_len
        kd_all, vd_all, ckv_all, kpe_all = kd16, vd16, ckv16, kpe16
    else:
        prev = jnp.concatenate([zero_row, past["pool"]], axis=1)
        s_len = past["ckv"].shape[1] + t_len
        s_pad = _round_up(s_len, 3 * LANES)
        kd_all = _with_past(past["diff_k"].reshape(batch, -1, diff_w), kd16, batch, s_pad)
        vd_all = _with_past(past["diff_v"].reshape(batch, -1, diff_w), vd16, batch, s_pad)
        ckv_all = _with_past(past["ckv"], ckv16, batch, s_pad)
        kpe_all = _with_past(past["kpe"], kpe16, batch, s_pad)
        tkv_diff = tkv_mla = s_pad // 3
    new_pool = jnp.concatenate([prev[:, 1:], u], axis=1)[:, -(POOL_HALO - 1):]
    y_pool = pool_mix(z, prev, wl["w_pool"], wl["pool_scale"], batch, t_len, pos0, tt)

    lq = wl["diff_lambda"]
    lam = jnp.exp(jnp.sum(lq[0] * lq[1])) - jnp.exp(jnp.sum(lq[2] * lq[3])) + lam_init
    o_diff = diff_attention(qd, kd_all, vd_all, lam, wl["g_diff_sub"], batch, heads_d, t_len, s_pad, s_len,
                            pos0, 1.0 - lam_init, tq_diff, tkv_diff)

    q = matmul(cqn, wl["w_uq"], F32, tm, 1024)
    o_mla = mla_attention(q, tabs, wl["w_uk_t"], ckv_all, kpe_all, wl["w_uv"], batch, t_len, s_pad, s_len,
                          pos0, tq_mla, tkv_mla)

    mix = jnp.concatenate([y_pool, o_diff, o_mla], axis=-1)
    y = matmul(mix, wl["w_out"], F32, tm, 1024)
    x, h = post_res(y, x, wl["g_mix_post"], wl["g_x_pre"], tt)

    qx = matmul(h, wl["w_mem_q"], BF16, tm, 512)
    mem_len, mem_heads = mem_k.shape[0] // batch, wl["mem_heads"]
    oc = cross_attention(qx, mem_k, mem_v, batch, t_len, mem_len, mem_heads, tq_x)
    y = matmul(oc, wl["w_mem_o"], F32, tm, 1024)
    x, h = post_res(y, x, wl["g_x_post"], wl["g_ff_pre"], tt)

    a = matmul_swiglu(h, wl["w_gate"], wl["w_up"], 2 * tm, 256)
    y = matmul(a, wl["w_down"], F32, 512, 512)
    x, h = post_res(y, x, wl["g_ff_post"], g_next, tt)
    return x, h, (kd32, vd32, ckv32, kpe32, new_pool)


def kernel(x_prompt, x_sample, cache_diff_k, cache_diff_v, cache_mla_ckv, cache_mla_kpe, cache_pool, cache_mem_k, cache_mem_v, mem_prompt, g_mix_pre, w_in, w_pool, pool_scale, diff_lambda, g_diff_sub, g_mla_q, w_mla_uq, w_mla_uk, w_mla_uv, g_mla_kv, w_out, g_mix_post, g_mem, w_mem_k, w_mem_v, w_mem_q, w_mem_o, g_x_pre, g_x_post, g_ff_pre, w_gate, w_up, w_down, g_ff_post):
    depth = w_in.shape[0]
    bp, tp, d = x_prompt.shape
    bs, ts, _ = x_sample.shape
    past_len = cache_mla_ckv.shape[2]
    pool_w = cache_pool.shape[3]
    heads_d, diff_w = cache_diff_k.shape[3], cache_diff_k.shape[3] * cache_diff_k.shape[4]
    q_rank, kv_rank = g_mla_q.shape[1], g_mla_kv.shape[1]
    mla_heads = w_mla_uk.shape[2]
    mem_len, mem_heads, mem_hd = cache_mem_k.shape[2:]
    mem_w = mem_heads * mem_hd
    dims = (pool_w, diff_w, q_rank, kv_rank)

    in_w = w_in.shape[2]
    nz = _round_up(in_w, 5 * LANES)

    layers = []
    for l in range(depth):
        uq = w_mla_uq[l].reshape(q_rank, mla_heads, MLA_NOPE + MLA_ROPE)
        layers.append({
            "dims": dims, "tn_in": nz // 5, "mem_heads": mem_heads,
            "w_in": jnp.pad(w_in[l], ((0, 0), (0, nz - in_w))).astype(BF16),
            "w_pool": w_pool[l].astype(BF16), "pool_scale": pool_scale[l], "diff_lambda": diff_lambda[l],
            "g_diff_sub": g_diff_sub[l], "g_mla_q": g_mla_q[l], "g_mla_kv": g_mla_kv[l],
            "w_uq": jnp.concatenate([uq[:, :, :MLA_NOPE].reshape(q_rank, -1),
                                     uq[:, :, MLA_NOPE:].reshape(q_rank, -1)], axis=1).astype(BF16),
            "w_uk_t": jnp.transpose(w_mla_uk[l], (1, 2, 0)).astype(BF16),
            "w_uv": jnp.transpose(w_mla_uv[l], (1, 0, 2)).astype(BF16),
            "w_out": w_out[l].astype(BF16), "g_mix_post": g_mix_post[l],
            "w_mem_q": w_mem_q[l].astype(BF16), "w_mem_o": w_mem_o[l].astype(BF16),
            "g_x_pre": g_x_pre[l], "g_x_post": g_x_post[l], "g_ff_pre": g_ff_pre[l],
            "w_gate": w_gate[l].astype(BF16), "w_up": w_up[l].astype(BF16), "w_down": w_down[l].astype(BF16),
            "g_ff_post": g_ff_post[l],
        })

    tabs_p = _rope_tables(jnp.arange(tp))
    tabs_s = _rope_tables(past_len + jnp.arange(ts))
    tiles_p = (1024, 256, 512, 256, 256, 256, 512)
    tiles_s = (bs * ts, ts, ts, 0, ts, 0, ts)

    xp = x_prompt.reshape(bp * tp, d)
    xs = x_sample.reshape(bs * ts, d)
    hp = rmsnorm_bf16(xp, g_mix_pre[0], tiles_p[1])
    hs = rmsnorm_bf16(xs, g_mix_pre[0], tiles_s[1])
    mem_n = mem_prompt.reshape(bp * mem_len, d)

    outs_p, outs_s, mem_ks, mem_vs = [], [], [], []
    for l in range(depth):
        wl = layers[l]
        g_next = g_mix_pre[l + 1] if l + 1 < depth else None
        m = rmsnorm_bf16(mem_n, g_mem[l], 256)
        w_kv = jnp.concatenate([w_mem_k[l], w_mem_v[l]], axis=1).astype(BF16)
        mkv = matmul(m, w_kv, F32, 1024, 1024)
        mk, mv = mkv[:, :mem_w], mkv[:, mem_w:]
        mem_ks.append(mk.reshape(bp, mem_len, mem_heads, mem_hd))
        mem_vs.append(mv.reshape(bp, mem_len, mem_heads, mem_hd))
        xp, hp, new_p = _layer(wl, l, xp, hp, bp, tp, 0, tabs_p, mk.astype(BF16), mv.astype(BF16), None,
                               g_next, tiles_p)
        outs_p.append(new_p)
        past = {"diff_k": cache_diff_k[l], "diff_v": cache_diff_v[l], "ckv": cache_mla_ckv[l],
                "kpe": cache_mla_kpe[l], "pool": cache_pool[l]}
        xs, hs, new_s = _layer(wl, l, xs, hs, bs, ts, past_len, tabs_s,
                               cache_mem_k[l].reshape(bs * mem_len, mem_w).astype(BF16),
                               cache_mem_v[l].reshape(bs * mem_len, mem_w).astype(BF16), past, g_next, tiles_s)
        outs_s.append(new_s)

    def stack(outs, idx, batch, t_len, tail):
        return jnp.stack([o[idx].reshape((batch, t_len) + tail) for o in outs])

    hd = 2 * DIFF_D
    return (xp.reshape(bp, tp, d), xs.reshape(bs, ts, d),
            stack(outs_p, 0, bp, tp, (heads_d, hd)), stack(outs_p, 1, bp, tp, (heads_d, hd)),
            stack(outs_p, 2, bp, tp, (kv_rank,)), stack(outs_p, 3, bp, tp, (MLA_ROPE,)),
            jnp.stack([o[4] for o in outs_p]),
            jnp.stack(mem_ks), jnp.stack(mem_vs),
            stack(outs_s, 0, bs, ts, (heads_d, hd)), stack(outs_s, 1, bs, ts, (heads_d, hd)),
            stack(outs_s, 2, bs, ts, (kv_rank,)), stack(outs_s, 3, bs, ts, (MLA_ROPE,)),
            jnp.stack([o[4] for o in outs_s]))
```

```python
import functools
import math

import jax
import jax.numpy as jnp
from jax import lax
from jax.experimental import pallas as pl
from jax.experimental.pallas import tpu as pltpu

F32 = jnp.float32
BF16 = jnp.bfloat16

EPS = 1e-6
CHUNK = 64
ROPE_THETA = 10000.0
POOL_WINDOWS = (2, 4, 8, 16)
POOL_HALO = 16
DIFF_D = 64
MLA_NOPE = 128
MLA_ROPE = 64
LANES = 128
MIB = 1024 * 1024
VMEM_LIMIT = 56 * MIB

_NT = (((1,), (1,)), ((), ()))


def _cparams(*sem):
    return pltpu.CompilerParams(dimension_semantics=sem, vmem_limit_bytes=VMEM_LIMIT)


def _rms(x, g):
    return x * lax.rsqrt(jnp.mean(x * x, axis=-1, keepdims=True) + EPS) * g


def _rmsnorm_kernel(x_ref, g_ref, o_ref):
    o_ref[...] = _rms(x_ref[...], g_ref[...]).astype(o_ref.dtype)


def rmsnorm_bf16(x, g, tm):
    m, d = x.shape
    return pl.pallas_call(
        _rmsnorm_kernel,
        grid=(m // tm,),
        in_specs=[pl.BlockSpec((tm, d), lambda i: (i, 0)), pl.BlockSpec((1, d), lambda i: (0, 0))],
        out_specs=pl.BlockSpec((tm, d), lambda i: (i, 0)),
        out_shape=jax.ShapeDtypeStruct((m, d), BF16),
        compiler_params=_cparams("parallel"),
        name="rmsnorm",
    )(x, g.reshape(1, d))


def _post_res_kernel(y_ref, x_ref, gp_ref, gn_ref, xo_ref, ho_ref):
    x = x_ref[...] + _rms(y_ref[...], gp_ref[...])
    xo_ref[...] = x
    ho_ref[...] = _rms(x, gn_ref[...]).astype(ho_ref.dtype)


def _post_res_last_kernel(y_ref, x_ref, gp_ref, xo_ref):
    xo_ref[...] = x_ref[...] + _rms(y_ref[...], gp_ref[...])


def post_res(y, x, g_post, g_next, tm):
    m, d = x.shape
    row = pl.BlockSpec((tm, d), lambda i: (i, 0))
    vec = pl.BlockSpec((1, d), lambda i: (0, 0))
    if g_next is None:
        return pl.pallas_call(
            _post_res_last_kernel, grid=(m // tm,), in_specs=[row, row, vec], out_specs=row,
            out_shape=jax.ShapeDtypeStruct((m, d), F32), compiler_params=_cparams("parallel"),
            name="post_res_last",
        )(y, x, g_post.reshape(1, d)), None
    return pl.pallas_call(
        _post_res_kernel, grid=(m // tm,), in_specs=[row, row, vec, vec], out_specs=[row, row],
        out_shape=[jax.ShapeDtypeStruct((m, d), F32), jax.ShapeDtypeStruct((m, d), BF16)],
        compiler_params=_cparams("parallel"), name="post_res",
    )(y, x, g_post.reshape(1, d), g_next.reshape(1, d))


def _mm_kernel(x_ref, w_ref, o_ref):
    o_ref[...] = jnp.dot(x_ref[...], w_ref[...], preferred_element_type=F32).astype(o_ref.dtype)


def matmul(x, w, l, out_dtype, tm, tn):
    m, k = x.shape
    n = w.shape[2]
    tm, tn = min(tm, m), min(tn, n)
    return pl.pallas_call(
        _mm_kernel, grid=(n // tn, m // tm),
        in_specs=[pl.BlockSpec((tm, k), lambda j, i: (i, 0)), pl.BlockSpec((None, k, tn), lambda j, i: (l, 0, j))],
        out_specs=pl.BlockSpec((tm, tn), lambda j, i: (i, j)),
        out_shape=jax.ShapeDtypeStruct((m, n), out_dtype),
        compiler_params=_cparams("parallel", "parallel"), name="matmul",
    )(x, w)


def _mm_cat_kernel(*refs):
    n_in = (len(refs) - 1) // 2
    acc = jnp.dot(refs[0][...], refs[n_in][...], preferred_element_type=F32)
    for a in range(1, n_in):
        acc += jnp.dot(refs[a][...], refs[n_in + a][...], preferred_element_type=F32)
    refs[-1][...] = acc.astype(refs[-1].dtype)


def matmul_cat(xs, w, l, out_dtype, tm, tn):
    m = xs[0].shape[0]
    n = w.shape[2]
    tm, tn = min(tm, m), min(tn, n)
    x_specs, w_specs, off = [], [], 0
    for x in xs:
        k = x.shape[1]
        assert off % k == 0, "each input's row band of w must start on a multiple of its own width"
        x_specs.append(pl.BlockSpec((tm, k), lambda j, i: (i, 0)))
        w_specs.append(pl.BlockSpec((None, k, tn), functools.partial(lambda j, i, r: (l, r, j), r=off // k)))
        off += k
    return pl.pallas_call(
        _mm_cat_kernel, grid=(n // tn, m // tm),
        in_specs=x_specs + w_specs,
        out_specs=pl.BlockSpec((tm, tn), lambda j, i: (i, j)),
        out_shape=jax.ShapeDtypeStruct((m, n), out_dtype),
        compiler_params=_cparams("parallel", "parallel"), name="matmul_cat",
    )(*xs, *([w] * len(xs)))


def _swiglu_kernel(x_ref, wg_ref, wu_ref, o_ref):
    x = x_ref[...]
    g = jnp.dot(x, wg_ref[...], preferred_element_type=F32)
    u = jnp.dot(x, wu_ref[...], preferred_element_type=F32)
    o_ref[...] = (g * jax.nn.sigmoid(g) * u).astype(o_ref.dtype)


def matmul_swiglu(x, wg, wu, l, tm, tn):
    m, k = x.shape
    n = wg.shape[2]
    tm = min(tm, m)
    wspec = pl.BlockSpec((None, k, tn), lambda i, j: (l, 0, j))
    return pl.pallas_call(
        _swiglu_kernel, grid=(m // tm, n // tn),
        in_specs=[pl.BlockSpec((tm, k), lambda i, j: (i, 0)), wspec, wspec],
        out_specs=pl.BlockSpec((tm, tn), lambda i, j: (i, j)),
        out_shape=jax.ShapeDtypeStruct((m, n), BF16),
        compiler_params=_cparams("parallel", "parallel"), name="matmul_swiglu",
    )(x, wg, wu)


def _rope_chunk(x, cos, sa, sb):
    return x * cos + pltpu.roll(x, LANES - 32, 1) * sa + pltpu.roll(x, 32, 1) * sb


def _split_kernel(z_ref, cos_ref, sa_ref, sb_ref, gq_ref, gkv_ref,
                  qd_ref, kd32_ref, kd16_ref, vd32_ref, vd16_ref, cq_ref, ckv32_ref, ckv16_ref,
                  kpe32_ref, kpe16_ref, *, offs, q_scale):
    o_qd, o_kd, o_vd, o_cq, o_ckv, o_kpe = offs
    cos, sa, sb = cos_ref[...], sa_ref[...], sb_ref[...]
    width = vd32_ref.shape[1]
    for c in range(width // LANES):
        sl = slice(c * LANES, (c + 1) * LANES)
        q = _rope_chunk(z_ref[:, o_qd + c * LANES:o_qd + (c + 1) * LANES], cos, sa, sb)
        qd_ref[:, sl] = (q * q_scale).astype(BF16)
        k = _rope_chunk(z_ref[:, o_kd + c * LANES:o_kd + (c + 1) * LANES], cos, sa, sb)
        kd32_ref[:, sl] = k
        kd16_ref[:, sl] = k.astype(BF16)
    v = z_ref[:, o_vd:o_vd + width]
    vd32_ref[...] = v
    vd16_ref[...] = v.astype(BF16)
    cq_ref[...] = _rms(z_ref[:, o_cq:o_cq + cq_ref.shape[1]], gq_ref[...]).astype(BF16)
    ckv = _rms(z_ref[:, o_ckv:o_ckv + ckv32_ref.shape[1]], gkv_ref[...])
    ckv32_ref[...] = ckv
    ckv16_ref[...] = ckv.astype(BF16)
    kpe = _rope_chunk(z_ref[:, o_kpe:o_kpe + LANES], cos, sa, sb)[:, :MLA_ROPE]
    kpe32_ref[...] = kpe
    kpe16_ref[...] = kpe.astype(BF16)


def split_projection(z, tabs, g_q, g_kv, dims, t_len, tt):
    m, nz = z.shape
    pool_w, diff_w, q_rank, kv_rank = dims
    offs = (pool_w, pool_w + diff_w, pool_w + 2 * diff_w, pool_w + 3 * diff_w,
            pool_w + 3 * diff_w + q_rank, pool_w + 3 * diff_w + q_rank + kv_rank)
    nt = t_len // tt
    row = lambda w: pl.BlockSpec((tt, w), lambda i: (i, 0))
    tab = pl.BlockSpec((tt, LANES), lambda i: (i % nt, 0))
    vec = lambda w: pl.BlockSpec((1, w), lambda i: (0, 0))
    sds = lambda w, dt: jax.ShapeDtypeStruct((m, w), dt)
    return pl.pallas_call(
        functools.partial(_split_kernel, offs=offs, q_scale=DIFF_D ** -0.5),
        grid=(m // tt,),
        in_specs=[row(nz), tab, tab, tab, vec(q_rank), vec(kv_rank)],
        out_specs=[row(diff_w), row(diff_w), row(diff_w), row(diff_w), row(diff_w), row(q_rank),
                   row(kv_rank), row(kv_rank), row(MLA_ROPE), row(MLA_ROPE)],
        out_shape=[sds(diff_w, BF16), sds(diff_w, F32), sds(diff_w, BF16), sds(diff_w, F32), sds(diff_w, BF16),
                   sds(q_rank, BF16), sds(kv_rank, F32), sds(kv_rank, BF16), sds(MLA_ROPE, F32),
                   sds(MLA_ROPE, BF16)],
        compiler_params=_cparams("parallel"), name="split_projection",
    )(z, *tabs, g_q.reshape(1, q_rank), g_kv.reshape(1, kv_rank))


def _pool_kernel(z_ref, halo_ref, prev_ref, wp_ref, sc_ref, o_ref, ext_ref, *, tt, pos0):
    i = pl.program_id(1)
    ext_ref[0:POOL_HALO, :] = jnp.where(i == 0, prev_ref[0], halo_ref[...])
    ext_ref[POOL_HALO:POOL_HALO + tt, :] = z_ref[...]
    pos = pos0 + i * tt + lax.broadcasted_iota(jnp.int32, (tt, 1), 0)
    group = wp_ref.shape[1]
    for g, w in enumerate(POOL_WINDOWS):
        cs = slice(g * group, (g + 1) * group)
        s = ext_ref[POOL_HALO:POOL_HALO + tt, cs]
        u = s
        for j in range(1, w):
            s = s + ext_ref[POOL_HALO - j:POOL_HALO - j + tt, cs]
        cnt = jnp.minimum(w, pos + 1).astype(F32)
        d = s / cnt - u
        y = jnp.dot(d.astype(BF16), wp_ref[g], preferred_element_type=F32) * sc_ref[:, cs]
        o_ref[:, cs] = y.astype(o_ref.dtype)


def pool_mix(z, prev, w_pool, l, scale, batch, t_len, pos0, tt):
    m = z.shape[0]
    c = prev.shape[2]
    nt = t_len // tt
    hb = tt // POOL_HALO
    return pl.pallas_call(
        functools.partial(_pool_kernel, tt=tt, pos0=pos0),
        grid=(batch, nt),
        in_specs=[pl.BlockSpec((tt, c), lambda b, i: (b * nt + i, 0)),
                  pl.BlockSpec((POOL_HALO, c), lambda b, i: (jnp.maximum((b * nt + i) * hb - 1, 0), 0)),
                  pl.BlockSpec((1, POOL_HALO, c), lambda b, i: (b, 0, 0)),
                  pl.BlockSpec((None,) + w_pool.shape[1:], lambda b, i: (l, 0, 0, 0)),
                  pl.BlockSpec((1, c), lambda b, i: (0, 0))],
        out_specs=pl.BlockSpec((tt, c), lambda b, i: (b * nt + i, 0)),
        out_shape=jax.ShapeDtypeStruct((m, c), BF16),
        scratch_shapes=[pltpu.VMEM((POOL_HALO + tt, c), F32)],
        compiler_params=_cparams("parallel", "parallel"), name="pool_mix",
    )(z, z, prev, w_pool, scale.reshape(1, c))


def _last_kv_block(i, tq, tkv, q0, s_len):
    last_q = q0 + (i + 1) * tq - 1
    visible = jnp.minimum((last_q // CHUNK + 1) * CHUNK, s_len)
    return (visible - 1) // tkv


def _mask_bias(i, j, tq, tkv, q0, s_len):
    qpos = q0 + i * tq + lax.broadcasted_iota(jnp.int32, (tq, 1), 0)
    kpos = j * tkv + lax.broadcasted_iota(jnp.int32, (1, tkv), 1)
    shift = CHUNK.bit_length() - 1
    ok = (jnp.right_shift(kpos, shift) <= jnp.right_shift(qpos, shift)) & (kpos < s_len)
    return jnp.where(ok, 0.0, -jnp.inf).astype(F32)


def _add_bias(s, bias, groups):
    rows, tkv = s.shape
    return (s.reshape(groups, rows // groups, tkv) + bias[None]).reshape(rows, tkv)


def _lane_tile(x, width):
    return x if width == LANES else jnp.concatenate([x] * (width // LANES), axis=1)


def _flash_update(s, v, m_ref, l_ref, acc_ref):
    tkv = s.shape[1]
    m_prev = m_ref[...]
    m_new = jnp.maximum(m_prev, jnp.max(s, axis=-1, keepdims=True))
    alpha = jnp.exp(m_prev - m_new)
    p = jnp.exp(s - _lane_tile(m_new, tkv))
    psum = p[:, :LANES]
    for c in range(1, tkv // LANES):
        psum = psum + p[:, c * LANES:(c + 1) * LANES]
    l_ref[...] = alpha * l_ref[...] + psum
    acc_ref[...] = (_lane_tile(alpha, acc_ref.shape[1]) * acc_ref[...]
                    + jnp.dot(p.astype(BF16), v, preferred_element_type=F32))
    m_ref[...] = m_new


def _flash_init(m_ref, l_ref, acc_ref):
    m_ref[...] = jnp.full(m_ref.shape, -jnp.inf, F32)
    l_ref[...] = jnp.zeros(l_ref.shape, F32)
    acc_ref[...] = jnp.zeros(acc_ref.shape, F32)


def _diff_kernel(lam_ref, q_ref, k_ref, v_ref, g_ref, o_ref, qs_ref, m_ref, l_ref, acc_ref, *,
                 heads, tq, tkv, q0, s_len, out_scale):
    i, j = pl.program_id(1), pl.program_id(2)
    hd = 2 * DIFF_D

    @pl.when(j == 0)
    def _():
        _flash_init(m_ref, l_ref, acc_ref)
        lane = lax.broadcasted_iota(jnp.int32, (tq, hd), 1)
        for h in range(heads):
            q = q_ref[:, h * hd:(h + 1) * hd]
            qs_ref[h, 0:tq, :] = jnp.where(lane < DIFF_D, q, jnp.zeros_like(q))
            qs_ref[h, tq:2 * tq, :] = jnp.where(lane >= DIFF_D, q, jnp.zeros_like(q))

    @pl.when(j <= _last_kv_block(i, tq, tkv, q0, s_len))
    def _():
        bias = _mask_bias(i, j, tq, tkv, q0, s_len)
        for h in range(heads):
            sl = slice(h * hd, (h + 1) * hd)
            s = lax.dot_general(qs_ref[h], k_ref[:, sl], _NT, preferred_element_type=F32)
            _flash_update(_add_bias(s, bias, 2), v_ref[:, sl], m_ref.at[h], l_ref.at[h], acc_ref.at[h])

    @pl.when(j == pl.num_programs(2) - 1)
    def _():
        lam = lam_ref[0, 0]
        for h in range(heads):
            a = acc_ref[h] / jnp.sum(l_ref[h], axis=-1, keepdims=True)
            o = a[0:tq] - lam * a[tq:2 * tq]
            o_ref[:, h * hd:(h + 1) * hd] = (_rms(o, g_ref[...]) * out_scale).astype(o_ref.dtype)


def diff_attention(q, k, v, lam, g_sub, batch, heads, t_len, s_pad, s_len, q0, out_scale, tq, tkv):
    hd = 2 * DIFF_D
    w = heads * hd
    nq, nkv = t_len // tq, s_pad // tkv
    last = functools.partial(_last_kv_block, tq=tq, tkv=tkv, q0=q0, s_len=s_len)
    kvmap = lambda b, i, j: (b * nkv + jnp.minimum(j, last(i)), 0)
    qmap = lambda b, i, j: (b * nq + i, 0)
    return pl.pallas_call(
        functools.partial(_diff_kernel, heads=heads, tq=tq, tkv=tkv, q0=q0, s_len=s_len, out_scale=out_scale),
        grid=(batch, nq, nkv),
        in_specs=[pl.BlockSpec(memory_space=pltpu.SMEM),
                  pl.BlockSpec((tq, w), qmap),
                  pl.BlockSpec((tkv, w), kvmap),
                  pl.BlockSpec((tkv, w), kvmap),
                  pl.BlockSpec((1, hd), lambda b, i, j: (0, 0))],
        out_specs=pl.BlockSpec((tq, w), qmap),
        out_shape=jax.ShapeDtypeStruct((batch * t_len, w), BF16),
        scratch_shapes=[pltpu.VMEM((heads, 2 * tq, hd), BF16), pltpu.VMEM((heads, 2 * tq, LANES), F32),
                        pltpu.VMEM((heads, 2 * tq, LANES), F32), pltpu.VMEM((heads, 2 * tq, hd), F32)],
        compiler_params=_cparams("parallel", "parallel", "arbitrary"), name="diff_attention",
    )(lam.reshape(1, 1), q, k, v, g_sub.reshape(1, hd))


def _mla_kernel(q_ref, cos_ref, sa_ref, sb_ref, wuk_ref, ckv_ref, kpe_ref, wuv_ref, o_ref,
                qlat_ref, qpe_ref, m_ref, l_ref, acc_ref, *, heads, tq, tkv, q0, s_len, scale):
    i, j = pl.program_id(1), pl.program_id(2)
    nope_w = heads * MLA_NOPE

    @pl.when(j == 0)
    def _():
        _flash_init(m_ref, l_ref, acc_ref)
        cos, sa, sb = cos_ref[...], sa_ref[...], sb_ref[...]
        for h in range(heads):
            qn = q_ref[:, h * MLA_NOPE:(h + 1) * MLA_NOPE].astype(BF16)
            qlat = jnp.dot(qn, wuk_ref[h], preferred_element_type=F32) * scale
            qlat_ref[h * tq:(h + 1) * tq, :] = qlat.astype(BF16)
        for c in range(heads * MLA_ROPE // LANES):
            pe = _rope_chunk(q_ref[:, nope_w + c * LANES:nope_w + (c + 1) * LANES], cos, sa, sb) * scale
            qpe_ref[2 * c * tq:(2 * c + 1) * tq, :] = pe[:, :MLA_ROPE].astype(BF16)
            qpe_ref[(2 * c + 1) * tq:(2 * c + 2) * tq, :] = pe[:, MLA_ROPE:].astype(BF16)

    @pl.when(j <= _last_kv_block(i, tq, tkv, q0, s_len))
    def _():
        ckv = ckv_ref[...]
        s = (lax.dot_general(qlat_ref[...], ckv, _NT, preferred_element_type=F32)
             + lax.dot_general(qpe_ref[...], kpe_ref[...], _NT, preferred_element_type=F32))
        s = _add_bias(s, _mask_bias(i, j, tq, tkv, q0, s_len), heads)
        _flash_update(s, ckv, m_ref, l_ref, acc_ref)

    @pl.when(j == pl.num_programs(2) - 1)
    def _():
        v_w = wuv_ref.shape[2]
        for h in range(heads):
            rows = slice(h * tq, (h + 1) * tq)
            o = (acc_ref[rows, :] / jnp.sum(l_ref[rows, :], axis=-1, keepdims=True)).astype(BF16)
            o_ref[:, h * v_w:(h + 1) * v_w] = jnp.dot(o, wuv_ref[h], preferred_element_type=F32).astype(o_ref.dtype)


def mla_attention(q, tabs, wuk_t, wuv, l, ckv, kpe, batch, t_len, s_pad, s_len, q0, tq, tkv):
    _, heads, _, rank = wuk_t.shape
    v_w = wuv.shape[3]
    rows = heads * tq
    nq, nkv = t_len // tq, s_pad // tkv
    last = functools.partial(_last_kv_block, tq=tq, tkv=tkv, q0=q0, s_len=s_len)
    kvmap = lambda b, i, j: (b * nkv + jnp.minimum(j, last(i)), 0)
    tab = pl.BlockSpec((tq, LANES), lambda b, i, j: (i, 0))
    layer4 = lambda a: pl.BlockSpec((None,) + a.shape[1:], lambda b, i, j: (l, 0, 0, 0))
    return pl.pallas_call(
        functools.partial(_mla_kernel, heads=heads, tq=tq, tkv=tkv, q0=q0, s_len=s_len,
                          scale=(MLA_NOPE + MLA_ROPE) ** -0.5),
        grid=(batch, nq, nkv),
        in_specs=[pl.BlockSpec((tq, q.shape[1]), lambda b, i, j: (b * nq + i, 0)),
                  tab, tab, tab, layer4(wuk_t),
                  pl.BlockSpec((tkv, rank), kvmap),
                  pl.BlockSpec((tkv, MLA_ROPE), kvmap),
                  layer4(wuv)],
        out_specs=pl.BlockSpec((tq, heads * v_w), lambda b, i, j: (b * nq + i, 0)),
        out_shape=jax.ShapeDtypeStruct((batch * t_len, heads * v_w), BF16),
        scratch_shapes=[pltpu.VMEM((rows, rank), BF16), pltpu.VMEM((rows, MLA_ROPE), BF16),
                        pltpu.VMEM((rows, LANES), F32), pltpu.VMEM((rows, LANES), F32),
                        pltpu.VMEM((rows, rank), F32)],
        compiler_params=_cparams("parallel", "parallel", "arbitrary"), name="mla_attention",
    )(q, *tabs, wuk_t, ckv, kpe, wuv)


def _cross_kernel(q_ref, k_ref, v_ref, o_ref, *, heads, hd):
    for h in range(heads):
        sl = slice(h * hd, (h + 1) * hd)
        s = lax.dot_general(q_ref[:, sl], k_ref[:, sl], _NT, preferred_element_type=F32) * (hd ** -0.5)
        p = jnp.exp(s - jnp.max(s, axis=-1, keepdims=True))
        o = jnp.dot(p.astype(BF16), v_ref[:, sl], preferred_element_type=F32)
        o_ref[:, sl] = (o / jnp.sum(p, axis=-1, keepdims=True)).astype(o_ref.dtype)


def cross_attention(q, mk, mv, batch, t_len, mem_len, heads, tq):
    w = q.shape[1]
    nq = t_len // tq
    kv = pl.BlockSpec((mem_len, w), lambda b, i: (b, 0))
    return pl.pallas_call(
        functools.partial(_cross_kernel, heads=heads, hd=w // heads),
        grid=(batch, nq),
        in_specs=[pl.BlockSpec((tq, w), lambda b, i: (b * nq + i, 0)), kv, kv],
        out_specs=pl.BlockSpec((tq, w), lambda b, i: (b * nq + i, 0)),
        out_shape=jax.ShapeDtypeStruct((batch * t_len, w), BF16),
        compiler_params=_cparams("parallel", "parallel"), name="cross_attention",
    )(q, mk, mv)


def _rope_tables(pos):
    half = DIFF_D // 2
    inv = ROPE_THETA ** (-jnp.arange(half, dtype=F32) / half)
    ang = pos.astype(F32)[:, None] * inv[None, :]
    cos, sin, zero = jnp.cos(ang), jnp.sin(ang), jnp.zeros_like(ang)
    reps = LANES // DIFF_D
    return (jnp.tile(cos, (1, 2 * reps)), jnp.tile(jnp.concatenate([-sin, zero], 1), (1, reps)),
            jnp.tile(jnp.concatenate([zero, sin], 1), (1, reps)))


def _round_up(a, b):
    return -(-a // b) * b


def _with_past(past, new, batch, s_pad):
    new = new.reshape(batch, -1, new.shape[-1])
    pad = s_pad - past.shape[1] - new.shape[1]
    parts = [past.astype(BF16), new] + ([jnp.zeros((batch, pad, new.shape[-1]), BF16)] if pad else [])
    return jnp.concatenate(parts, axis=1).reshape(batch * s_pad, new.shape[-1])


def _layer(wts, l, x, h, batch, t_len, pos0, tabs, mem_k, mem_v, past, g_next, tiles):
    tm, tt, tq_diff, tkv_diff, tq_mla, tkv_mla, tq_x = tiles
    lam_init = 0.8 - 0.6 * math.exp(-0.3 * l)
    pool_w, diff_w, q_rank, kv_rank = wts["dims"]
    heads_d = diff_w // (2 * DIFF_D)

    z = matmul(h, wts["w_in"], l, F32, tm, wts["tn_in"])
    qd, kd32, kd16, vd32, vd16, cqn, ckv32, ckv16, kpe32, kpe16 = split_projection(
        z, tabs, wts["g_mla_q"][l], wts["g_mla_kv"][l], wts["dims"], t_len, tt)

    keep = POOL_HALO - 1
    if past is None:
        prev = jnp.zeros((batch, POOL_HALO, pool_w), F32)
        s_len = s_pad = t_len
        kd_all, vd_all, ckv_all, kpe_all = kd16, vd16, ckv16, kpe16
    else:
        prev = jnp.concatenate([jnp.zeros((batch, 1, pool_w), F32), past["pool"]], axis=1)
        s_len = past["ckv"].shape[1] + t_len
        s_pad = _round_up(s_len, 3 * LANES)
        kd_all = _with_past(past["diff_k"].reshape(batch, -1, diff_w), kd16, batch, s_pad)
        vd_all = _with_past(past["diff_v"].reshape(batch, -1, diff_w), vd16, batch, s_pad)
        ckv_all = _with_past(past["ckv"], ckv16, batch, s_pad)
        kpe_all = _with_past(past["kpe"], kpe16, batch, s_pad)
        tkv_diff = tkv_mla = s_pad // 3
    z3 = z.reshape(batch, t_len, z.shape[1])
    if t_len >= keep:
        new_pool = z3[:, t_len - keep:, :pool_w]
    else:
        new_pool = jnp.concatenate([prev[:, 1 + t_len:], z3[:, :, :pool_w]], axis=1)
    y_pool = pool_mix(z, prev, wts["w_pool"], l, wts["pool_scale"][l], batch, t_len, pos0, tt)

    lq = wts["diff_lambda"][l]
    lam = jnp.exp(jnp.sum(lq[0] * lq[1])) - jnp.exp(jnp.sum(lq[2] * lq[3])) + lam_init
    o_diff = diff_attention(qd, kd_all, vd_all, lam, wts["g_diff_sub"][l], batch, heads_d, t_len, s_pad, s_len,
                            pos0, 1.0 - lam_init, tq_diff, tkv_diff)

    q = matmul(cqn, wts["w_uq"], l, F32, tm, 1024)
    o_mla = mla_attention(q, tabs, wts["w_uk_t"], wts["w_uv"], l, ckv_all, kpe_all, batch, t_len, s_pad, s_len,
                          pos0, tq_mla, tkv_mla)

    y = matmul_cat([y_pool, o_diff, o_mla], wts["w_out"], l, F32, tm, 1024)
    x, h = post_res(y, x, wts["g_mix_post"][l], wts["g_x_pre"][l], tt)

    qx = matmul(h, wts["w_mem_q"], l, BF16, tm, 512)
    mem_len, mem_heads = mem_k.shape[0] // batch, wts["mem_heads"]
    oc = cross_attention(qx, mem_k, mem_v, batch, t_len, mem_len, mem_heads, tq_x)
    y = matmul(oc, wts["w_mem_o"], l, F32, tm, 1024)
    x, h = post_res(y, x, wts["g_x_post"][l], wts["g_ff_pre"][l], tt)

    a = matmul_swiglu(h, wts["w_gate"], wts["w_up"], l, 2 * tm, 256)
    y = matmul(a, wts["w_down"], l, F32, 512, 512)
    x, h = post_res(y, x, wts["g_ff_post"][l], g_next, tt)
    return x, h, (kd32, vd32, ckv32, kpe32, new_pool)


def kernel(x_prompt, x_sample, cache_diff_k, cache_diff_v, cache_mla_ckv, cache_mla_kpe, cache_pool, cache_mem_k, cache_mem_v, mem_prompt, g_mix_pre, w_in, w_pool, pool_scale, diff_lambda, g_diff_sub, g_mla_q, w_mla_uq, w_mla_uk, w_mla_uv, g_mla_kv, w_out, g_mix_post, g_mem, w_mem_k, w_mem_v, w_mem_q, w_mem_o, g_x_pre, g_x_post, g_ff_pre, w_gate, w_up, w_down, g_ff_post):
    depth = w_in.shape[0]
    bp, tp, d = x_prompt.shape
    bs, ts, _ = x_sample.shape
    past_len = cache_mla_ckv.shape[2]
    pool_w = cache_pool.shape[3]
    heads_d, diff_w = cache_diff_k.shape[3], cache_diff_k.shape[3] * cache_diff_k.shape[4]
    q_rank, kv_rank = g_mla_q.shape[1], g_mla_kv.shape[1]
    mla_heads = w_mla_uk.shape[2]
    mem_len, mem_heads, mem_hd = cache_mem_k.shape[2:]
    mem_w = mem_heads * mem_hd

    in_w = w_in.shape[2]
    nz = _round_up(in_w, 5 * LANES)
    uq = w_mla_uq.reshape(depth, q_rank, mla_heads, MLA_NOPE + MLA_ROPE)
    wts = {
        "dims": (pool_w, diff_w, q_rank, kv_rank), "tn_in": nz // 5, "mem_heads": mem_heads,
        "w_in": jnp.pad(w_in, ((0, 0), (0, 0), (0, nz - in_w))).astype(BF16),
        "w_pool": w_pool.astype(BF16), "pool_scale": pool_scale, "diff_lambda": diff_lambda,
        "g_diff_sub": g_diff_sub, "g_mla_q": g_mla_q, "g_mla_kv": g_mla_kv,
        "w_uq": jnp.concatenate([uq[..., :MLA_NOPE].reshape(depth, q_rank, -1),
                                 uq[..., MLA_NOPE:].reshape(depth, q_rank, -1)], axis=2).astype(BF16),
        "w_uk_t": jnp.transpose(w_mla_uk, (0, 2, 3, 1)).astype(BF16),
        "w_uv": jnp.transpose(w_mla_uv, (0, 2, 1, 3)).astype(BF16),
        "w_out": w_out.astype(BF16), "g_mix_post": g_mix_post,
        "w_mem_q": w_mem_q.astype(BF16), "w_mem_o": w_mem_o.astype(BF16),
        "g_x_pre": g_x_pre, "g_x_post": g_x_post, "g_ff_pre": g_ff_pre,
        "w_gate": w_gate.astype(BF16), "w_up": w_up.astype(BF16), "w_down": w_down.astype(BF16),
        "g_ff_post": g_ff_post,
    }
    w_mem_kv = jnp.concatenate([w_mem_k, w_mem_v], axis=2).astype(BF16)

    tabs_p = _rope_tables(jnp.arange(tp))
    tabs_s = _rope_tables(past_len + jnp.arange(ts))
    tiles_p = (1024, 256, 512, 512, 128, 512, 512)
    tiles_s = (bs * ts, ts, ts, 0, ts, 0, ts)

    xp = x_prompt.reshape(bp * tp, d)
    xs = x_sample.reshape(bs * ts, d)
    hp = rmsnorm_bf16(xp, g_mix_pre[0], tiles_p[1])
    hs = rmsnorm_bf16(xs, g_mix_pre[0], tiles_s[1])
    mem_n = mem_prompt.reshape(bp * mem_len, d)
    mem_k_s = cache_mem_k.reshape(depth, bs * mem_len, mem_w).astype(BF16)
    mem_v_s = cache_mem_v.reshape(depth, bs * mem_len, mem_w).astype(BF16)

    outs_p, outs_s, mem_ks, mem_vs = [], [], [], []
    for l in range(depth):
        g_next = g_mix_pre[l + 1] if l + 1 < depth else None
        m = rmsnorm_bf16(mem_n, g_mem[l], 256)
        mkv = matmul(m, w_mem_kv, l, F32, 1024, 1024)
        mk, mv = mkv[:, :mem_w], mkv[:, mem_w:]
        mem_ks.append(mk.reshape(bp, mem_len, mem_heads, mem_hd))
        mem_vs.append(mv.reshape(bp, mem_len, mem_heads, mem_hd))
        xp, hp, new_p = _layer(wts, l, xp, hp, bp, tp, 0, tabs_p, mk.astype(BF16), mv.astype(BF16), None,
                               g_next, tiles_p)
        outs_p.append(new_p)
        past = {"diff_k": cache_diff_k[l], "diff_v": cache_diff_v[l], "ckv": cache_mla_ckv[l],
                "kpe": cache_mla_kpe[l], "pool": cache_pool[l]}
        xs, hs, new_s = _layer(wts, l, xs, hs, bs, ts, past_len, tabs_s, mem_k_s[l], mem_v_s[l], past,
                               g_next, tiles_s)
        outs_s.append(new_s)

    def stack(outs, idx, batch, t_len, tail):
        return jnp.stack([o[idx].reshape((batch, t_len) + tail) for o in outs])

    hd = 2 * DIFF_D
    return (xp.reshape(bp, tp, d), xs.reshape(bs, ts, d),
            stack(outs_p, 0, bp, tp, (heads_d, hd)), stack(outs_p, 1, bp, tp, (heads_d, hd)),
            stack(outs_p, 2, bp, tp, (kv_rank,)), stack(outs_p, 3, bp, tp, (MLA_ROPE,)),
            jnp.stack([o[4] for o in outs_p]),
            jnp.stack(mem_ks), jnp.stack(mem_vs),
            stack(outs_s, 0, bs, ts, (heads_d, hd)), stack(outs_s, 1, bs, ts, (heads_d, hd)),
            stack(outs_s, 2, bs, ts, (kv_rank,)), stack(outs_s, 3, bs, ts, (MLA_ROPE,)),
            jnp.stack([o[4] for o in outs_s]))
```

```python
import functools
import math

import jax
import jax.numpy as jnp
from jax import lax
from jax.experimental import pallas as pl
from jax.experimental.pallas import tpu as pltpu

F32 = jnp.float32
BF16 = jnp.bfloat16

EPS = 1e-6
CHUNK = 64
ROPE_THETA = 10000.0
POOL_WINDOWS = (2, 4, 8, 16)
POOL_HALO = 16
DIFF_D = 64
MLA_NOPE = 128
MLA_ROPE = 64
LANES = 128
MIB = 1024 * 1024
VMEM_LIMIT = 56 * MIB

_NT = (((1,), (1,)), ((), ()))


def _cparams(*sem):
    return pltpu.CompilerParams(dimension_semantics=sem, vmem_limit_bytes=VMEM_LIMIT)


def _rms(x, g):
    return x * lax.rsqrt(jnp.mean(x * x, axis=-1, keepdims=True) + EPS) * g


def _rmsnorm_kernel(x_ref, g_ref, o_ref):
    o_ref[...] = _rms(x_ref[...], g_ref[...]).astype(o_ref.dtype)


def rmsnorm_bf16(x, g, tm):
    m, d = x.shape
    return pl.pallas_call(
        _rmsnorm_kernel,
        grid=(m // tm,),
        in_specs=[pl.BlockSpec((tm, d), lambda i: (i, 0)), pl.BlockSpec((1, d), lambda i: (0, 0))],
        out_specs=pl.BlockSpec((tm, d), lambda i: (i, 0)),
        out_shape=jax.ShapeDtypeStruct((m, d), BF16),
        compiler_params=_cparams("parallel"),
        name="rmsnorm",
    )(x, g.reshape(1, d))


def _post_res_kernel(y_ref, x_ref, gp_ref, gn_ref, xo_ref, ho_ref):
    x = x_ref[...] + _rms(y_ref[...], gp_ref[...])
    xo_ref[...] = x
    ho_ref[...] = _rms(x, gn_ref[...]).astype(ho_ref.dtype)


def _post_res_last_kernel(y_ref, x_ref, gp_ref, xo_ref):
    xo_ref[...] = x_ref[...] + _rms(y_ref[...], gp_ref[...])


def post_res(y, x, g_post, g_next, tm):
    m, d = x.shape
    row = pl.BlockSpec((tm, d), lambda i: (i, 0))
    vec = pl.BlockSpec((1, d), lambda i: (0, 0))
    if g_next is None:
        return pl.pallas_call(
            _post_res_last_kernel, grid=(m // tm,), in_specs=[row, row, vec], out_specs=row,
            out_shape=jax.ShapeDtypeStruct((m, d), F32), compiler_params=_cparams("parallel"),
            name="post_res_last",
        )(y, x, g_post.reshape(1, d)), None
    return pl.pallas_call(
        _post_res_kernel, grid=(m // tm,), in_specs=[row, row, vec, vec], out_specs=[row, row],
        out_shape=[jax.ShapeDtypeStruct((m, d), F32), jax.ShapeDtypeStruct((m, d), BF16)],
        compiler_params=_cparams("parallel"), name="post_res",
    )(y, x, g_post.reshape(1, d), g_next.reshape(1, d))


def _mm_kernel(x_ref, w_ref, o_ref):
    o_ref[...] = jnp.dot(x_ref[...], w_ref[...], preferred_element_type=F32).astype(o_ref.dtype)


def matmul(x, w, l, out_dtype, tm, tn):
    m, k = x.shape
    n = w.shape[2]
    tm, tn = min(tm, m), min(tn, n)
    return pl.pallas_call(
        _mm_kernel, grid=(n // tn, m // tm),
        in_specs=[pl.BlockSpec((tm, k), lambda j, i: (i, 0)), pl.BlockSpec((None, k, tn), lambda j, i: (l, 0, j))],
        out_specs=pl.BlockSpec((tm, tn), lambda j, i: (i, j)),
        out_shape=jax.ShapeDtypeStruct((m, n), out_dtype),
        compiler_params=_cparams("parallel", "parallel"), name="matmul",
    )(x, w)


def _mm_cat_kernel(*refs):
    n_in = (len(refs) - 1) // 2
    acc = jnp.dot(refs[0][...], refs[n_in][...], preferred_element_type=F32)
    for a in range(1, n_in):
        acc += jnp.dot(refs[a][...], refs[n_in + a][...], preferred_element_type=F32)
    refs[-1][...] = acc.astype(refs[-1].dtype)


def matmul_cat(xs, w, l, out_dtype, tm, tn):
    m = xs[0].shape[0]
    n = w.shape[2]
    tm, tn = min(tm, m), min(tn, n)
    x_specs, w_specs, off = [], [], 0
    for x in xs:
        k = x.shape[1]
        assert off % k == 0, "each input's row band of w must start on a multiple of its own width"
        x_specs.append(pl.BlockSpec((tm, k), lambda j, i: (i, 0)))
        w_specs.append(pl.BlockSpec((None, k, tn), functools.partial(lambda j, i, r: (l, r, j), r=off // k)))
        off += k
    return pl.pallas_call(
        _mm_cat_kernel, grid=(n // tn, m // tm),
        in_specs=x_specs + w_specs,
        out_specs=pl.BlockSpec((tm, tn), lambda j, i: (i, j)),
        out_shape=jax.ShapeDtypeStruct((m, n), out_dtype),
        compiler_params=_cparams("parallel", "parallel"), name="matmul_cat",
    )(*xs, *([w] * len(xs)))


def _swiglu_kernel(x_ref, wg_ref, wu_ref, o_ref):
    x = x_ref[...]
    g = jnp.dot(x, wg_ref[...], preferred_element_type=F32)
    u = jnp.dot(x, wu_ref[...], preferred_element_type=F32)
    o_ref[...] = (g * jax.nn.sigmoid(g) * u).astype(o_ref.dtype)


def matmul_swiglu(x, wg, wu, l, tm, tn):
    m, k = x.shape
    n = wg.shape[2]
    tm = min(tm, m)
    wspec = pl.BlockSpec((None, k, tn), lambda i, j: (l, 0, j))
    return pl.pallas_call(
        _swiglu_kernel, grid=(m // tm, n // tn),
        in_specs=[pl.BlockSpec((tm, k), lambda i, j: (i, 0)), wspec, wspec],
        out_specs=pl.BlockSpec((tm, tn), lambda i, j: (i, j)),
        out_shape=jax.ShapeDtypeStruct((m, n), BF16),
        compiler_params=_cparams("parallel", "parallel"), name="matmul_swiglu",
    )(x, wg, wu)


def _rope_chunk(x, cos, sa, sb):
    return x * cos + pltpu.roll(x, LANES - 32, 1) * sa + pltpu.roll(x, 32, 1) * sb


_SPLIT_INPUTS = 6


def _split_kernel(*refs, offs, q_scale):
    z_ref, cos_ref, sa_ref, sb_ref, gq_ref, gkv_ref = refs[:_SPLIT_INPUTS]
    (qd_ref, kd32_ref, kd16_ref, vd32_ref, vd16_ref, cq_ref, ckv32_ref, ckv16_ref,
     kpe32_ref, kpe16_ref) = refs[-10:]
    o_qd, o_kd, o_vd, o_cq, o_ckv, o_kpe = offs
    cos, sa, sb = cos_ref[...], sa_ref[...], sb_ref[...]
    width = vd32_ref.shape[1]
    for c in range(width // LANES):
        sl = slice(c * LANES, (c + 1) * LANES)
        q = _rope_chunk(z_ref[:, o_qd + c * LANES:o_qd + (c + 1) * LANES], cos, sa, sb)
        qd_ref[:, sl] = (q * q_scale).astype(BF16)
        k = _rope_chunk(z_ref[:, o_kd + c * LANES:o_kd + (c + 1) * LANES], cos, sa, sb)
        kd32_ref[:, sl] = k
        kd16_ref[:, sl] = k.astype(BF16)
    v = z_ref[:, o_vd:o_vd + width]
    vd32_ref[...] = v
    vd16_ref[...] = v.astype(BF16)
    cq_ref[...] = _rms(z_ref[:, o_cq:o_cq + cq_ref.shape[1]], gq_ref[...]).astype(BF16)
    ckv = _rms(z_ref[:, o_ckv:o_ckv + ckv32_ref.shape[1]], gkv_ref[...])
    ckv32_ref[...] = ckv
    ckv16_ref[...] = ckv.astype(BF16)
    kpe = _rope_chunk(z_ref[:, o_kpe:o_kpe + LANES], cos, sa, sb)[:, :MLA_ROPE]
    kpe32_ref[...] = kpe
    kpe16_ref[...] = kpe.astype(BF16)


def split_projection(z, tabs, g_q, g_kv, dims, t_len, tt, l, depth, state):
    m, nz = z.shape
    pool_w, diff_w, q_rank, kv_rank = dims
    offs = (pool_w, pool_w + diff_w, pool_w + 2 * diff_w, pool_w + 3 * diff_w,
            pool_w + 3 * diff_w + q_rank, pool_w + 3 * diff_w + q_rank + kv_rank)
    nt = t_len // tt
    row = lambda w: pl.BlockSpec((tt, w), lambda i: (i, 0))
    lay = lambda w: pl.BlockSpec((None, tt, w), lambda i: (l, i, 0))
    tab = pl.BlockSpec((tt, LANES), lambda i: (i % nt, 0))
    vec = lambda w: pl.BlockSpec((1, w), lambda i: (0, 0))
    sds = lambda w, dt: jax.ShapeDtypeStruct((m, w), dt)
    stk = lambda w: jax.ShapeDtypeStruct((depth, m, w), F32)
    state = () if state is None else tuple(state)
    state_out = (1, 3, 6, 8)
    outs = pl.pallas_call(
        functools.partial(_split_kernel, offs=offs, q_scale=DIFF_D ** -0.5),
        grid=(m // tt,),
        in_specs=[row(nz), tab, tab, tab, vec(q_rank), vec(kv_rank)]
                 + [pl.BlockSpec(memory_space=pl.ANY)] * len(state),
        out_specs=[row(diff_w), lay(diff_w), row(diff_w), lay(diff_w), row(diff_w), row(q_rank),
                   lay(kv_rank), row(kv_rank), lay(MLA_ROPE), row(MLA_ROPE)],
        out_shape=[sds(diff_w, BF16), stk(diff_w), sds(diff_w, BF16), stk(diff_w), sds(diff_w, BF16),
                   sds(q_rank, BF16), stk(kv_rank), sds(kv_rank, BF16), stk(MLA_ROPE), sds(MLA_ROPE, BF16)],
        input_output_aliases={_SPLIT_INPUTS + a: o for a, o in enumerate(state_out[:len(state)])},
        compiler_params=_cparams("parallel"), name="split_projection",
    )(z, *tabs, g_q.reshape(1, q_rank), g_kv.reshape(1, kv_rank), *state)
    qd, kd32, kd16, vd32, vd16, cqn, ckv32, ckv16, kpe32, kpe16 = outs
    return (qd, kd16, vd16, cqn, ckv16, kpe16), (kd32, vd32, ckv32, kpe32)


def _pool_kernel(z_ref, halo_ref, prev_ref, wp_ref, sc_ref, o_ref, ext_ref, *, tt, pos0):
    i = pl.program_id(1)
    ext_ref[0:POOL_HALO, :] = jnp.where(i == 0, prev_ref[0], halo_ref[...])
    ext_ref[POOL_HALO:POOL_HALO + tt, :] = z_ref[...]
    pos = pos0 + i * tt + lax.broadcasted_iota(jnp.int32, (tt, 1), 0)
    group = wp_ref.shape[1]
    for g, w in enumerate(POOL_WINDOWS):
        cs = slice(g * group, (g + 1) * group)
        s = ext_ref[POOL_HALO:POOL_HALO + tt, cs]
        u = s
        for j in range(1, w):
            s = s + ext_ref[POOL_HALO - j:POOL_HALO - j + tt, cs]
        cnt = jnp.minimum(w, pos + 1).astype(F32)
        d = s / cnt - u
        y = jnp.dot(d.astype(BF16), wp_ref[g], preferred_element_type=F32) * sc_ref[:, cs]
        o_ref[:, cs] = y.astype(o_ref.dtype)


def pool_mix(z, prev, w_pool, l, scale, batch, t_len, pos0, tt):
    m = z.shape[0]
    c = prev.shape[2]
    nt = t_len // tt
    hb = tt // POOL_HALO
    return pl.pallas_call(
        functools.partial(_pool_kernel, tt=tt, pos0=pos0),
        grid=(batch, nt),
        in_specs=[pl.BlockSpec((tt, c), lambda b, i: (b * nt + i, 0)),
                  pl.BlockSpec((POOL_HALO, c), lambda b, i: (jnp.maximum((b * nt + i) * hb - 1, 0), 0)),
                  pl.BlockSpec((1, POOL_HALO, c), lambda b, i: (b, 0, 0)),
                  pl.BlockSpec((None,) + w_pool.shape[1:], lambda b, i: (l, 0, 0, 0)),
                  pl.BlockSpec((1, c), lambda b, i: (0, 0))],
        out_specs=pl.BlockSpec((tt, c), lambda b, i: (b * nt + i, 0)),
        out_shape=jax.ShapeDtypeStruct((m, c), BF16),
        scratch_shapes=[pltpu.VMEM((POOL_HALO + tt, c), F32)],
        compiler_params=_cparams("parallel", "parallel"), name="pool_mix",
    )(z, z, prev, w_pool, scale.reshape(1, c))


def _last_kv_block(i, tq, tkv, q0, s_len):
    last_q = q0 + (i + 1) * tq - 1
    visible = jnp.minimum((last_q // CHUNK + 1) * CHUNK, s_len)
    return (visible - 1) // tkv


def _mask_bias(q_start, tq, k_start, tkv, s_len):
    qpos = q_start + lax.broadcasted_iota(jnp.int32, (tq, 1), 0)
    kpos = k_start + lax.broadcasted_iota(jnp.int32, (1, tkv), 1)
    shift = CHUNK.bit_length() - 1
    ok = (jnp.right_shift(kpos, shift) <= jnp.right_shift(qpos, shift)) & (kpos < s_len)
    return jnp.where(ok, 0.0, -jnp.inf).astype(F32)


def _add_bias(s, bias, groups):
    rows, tkv = s.shape
    return (s.reshape(groups, rows // groups, tkv) + bias[None]).reshape(rows, tkv)


def _lane_tile(x, width):
    return x if width == LANES else jnp.concatenate([x] * (width // LANES), axis=1)


def _flash_update(s, v, m_ref, l_ref, acc_ref):
    tkv = s.shape[1]
    m_prev = m_ref[...]
    m_new = jnp.maximum(m_prev, jnp.max(s, axis=-1, keepdims=True))
    alpha = jnp.exp(m_prev - m_new)
    p = jnp.exp(s - _lane_tile(m_new, tkv))
    psum = p[:, :LANES]
    for c in range(1, tkv // LANES):
        psum = psum + p[:, c * LANES:(c + 1) * LANES]
    l_ref[...] = alpha * l_ref[...] + psum
    acc_ref[...] = (_lane_tile(alpha, acc_ref.shape[1]) * acc_ref[...]
                    + jnp.dot(p.astype(BF16), v, preferred_element_type=F32))
    m_ref[...] = m_new


def _flash_init(m_ref, l_ref, acc_ref):
    m_ref[...] = jnp.full(m_ref.shape, -jnp.inf, F32)
    l_ref[...] = jnp.zeros(l_ref.shape, F32)
    acc_ref[...] = jnp.zeros(acc_ref.shape, F32)


def _flash_result(l_ref, acc_ref):
    return acc_ref[...] / jnp.sum(l_ref[...], axis=-1, keepdims=True)


def _flash_scratch(groups, rows, dv):
    lead = () if groups is None else (groups,)
    return [pltpu.VMEM(lead + (rows, LANES), F32), pltpu.VMEM(lead + (rows, LANES), F32),
            pltpu.VMEM(lead + (rows, dv), F32)]


def _past_tile(p_len, cap):
    tile = max(t for t in range(LANES, cap + 1, LANES) if p_len % t == 0)
    return tile


def _diff_split_q(q_ref, qs_ref, heads, tq):
    hd = 2 * DIFF_D
    lane = lax.broadcasted_iota(jnp.int32, (tq, hd), 1)
    for h in range(heads):
        q = q_ref[:, h * hd:(h + 1) * hd]
        qs_ref[h, 0:tq, :] = jnp.where(lane < DIFF_D, q, jnp.zeros_like(q))
        qs_ref[h, tq:2 * tq, :] = jnp.where(lane >= DIFF_D, q, jnp.zeros_like(q))


def _diff_finish(lam_ref, g_ref, o_ref, l_ref, acc_ref, heads, tq, out_scale):
    hd = 2 * DIFF_D
    lam = lam_ref[0, 0]
    for h in range(heads):
        a = _flash_result(l_ref.at[h], acc_ref.at[h])
        o = a[0:tq] - lam * a[tq:2 * tq]
        o_ref[:, h * hd:(h + 1) * hd] = (_rms(o, g_ref[...]) * out_scale).astype(o_ref.dtype)


def _diff_kernel(lam_ref, q_ref, k_ref, v_ref, g_ref, o_ref, qs_ref, m_ref, l_ref, acc_ref, *,
                 heads, tq, tkv, s_len, out_scale):
    i, j = pl.program_id(1), pl.program_id(2)
    hd = 2 * DIFF_D

    @pl.when(j == 0)
    def _():
        _flash_init(m_ref, l_ref, acc_ref)
        _diff_split_q(q_ref, qs_ref, heads, tq)

    @pl.when(j <= _last_kv_block(i, tq, tkv, 0, s_len))
    def _():
        bias = _mask_bias(i * tq, tq, j * tkv, tkv, s_len)
        for h in range(heads):
            sl = slice(h * hd, (h + 1) * hd)
            s = lax.dot_general(qs_ref[h], k_ref[:, sl], _NT, preferred_element_type=F32)
            _flash_update(_add_bias(s, bias, 2), v_ref[:, sl], m_ref.at[h], l_ref.at[h], acc_ref.at[h])

    @pl.when(j == pl.num_programs(2) - 1)
    def _():
        _diff_finish(lam_ref, g_ref, o_ref, l_ref, acc_ref, heads, tq, out_scale)


def diff_attention(q, k, v, lam, g_sub, batch, heads, t_len, out_scale, tq, tkv):
    hd = 2 * DIFF_D
    w = heads * hd
    nq, nkv = t_len // tq, t_len // tkv
    last = functools.partial(_last_kv_block, tq=tq, tkv=tkv, q0=0, s_len=t_len)
    kvmap = lambda b, i, j: (b * nkv + jnp.minimum(j, last(i)), 0)
    qmap = lambda b, i, j: (b * nq + i, 0)
    return pl.pallas_call(
        functools.partial(_diff_kernel, heads=heads, tq=tq, tkv=tkv, s_len=t_len, out_scale=out_scale),
        grid=(batch, nq, nkv),
        in_specs=[pl.BlockSpec(memory_space=pltpu.SMEM),
                  pl.BlockSpec((tq, w), qmap),
                  pl.BlockSpec((tkv, w), kvmap),
                  pl.BlockSpec((tkv, w), kvmap),
                  pl.BlockSpec((1, hd), lambda b, i, j: (0, 0))],
        out_specs=pl.BlockSpec((tq, w), qmap),
        out_shape=jax.ShapeDtypeStruct((batch * t_len, w), BF16),
        scratch_shapes=[pltpu.VMEM((heads, 2 * tq, hd), BF16)] + _flash_scratch(heads, 2 * tq, hd),
        compiler_params=_cparams("parallel", "parallel", "arbitrary"), name="diff_attention",
    )(lam.reshape(1, 1), q, k, v, g_sub.reshape(1, hd))


def _diff_past_kernel(lam_ref, q_ref, kc_ref, vc_ref, kn_ref, vn_ref, g_ref, o_ref,
                      qs_ref, m_ref, l_ref, acc_ref, *, heads, tq, tkv, p_len, s_len, out_scale):
    j = pl.program_id(1)
    n_past = pl.num_programs(1) - 1
    hd = 2 * DIFF_D

    @pl.when(j == 0)
    def _():
        _flash_init(m_ref, l_ref, acc_ref)
        _diff_split_q(q_ref, qs_ref, heads, tq)

    def sweep(k_of, v_of, k_start, width):
        bias = _mask_bias(p_len, tq, k_start, width, s_len)
        for h in range(heads):
            s = lax.dot_general(qs_ref[h], k_of(h), _NT, preferred_element_type=F32)
            _flash_update(_add_bias(s, bias, 2), v_of(h), m_ref.at[h], l_ref.at[h], acc_ref.at[h])

    @pl.when(j < n_past)
    def _():
        sweep(lambda h: kc_ref[h], lambda h: vc_ref[h], j * tkv, tkv)

    @pl.when(j == n_past)
    def _():
        sweep(lambda h: kn_ref[:, h * hd:(h + 1) * hd], lambda h: vn_ref[:, h * hd:(h + 1) * hd],
              p_len, kn_ref.shape[0])
        _diff_finish(lam_ref, g_ref, o_ref, l_ref, acc_ref, heads, tq, out_scale)


def diff_attention_past(q, kc, vc, kn, vn, l, lam, g_sub, batch, heads, t_len, out_scale):
    hd = 2 * DIFF_D
    w = heads * hd
    p_len = kc.shape[3]
    tn = kn.shape[0] // batch
    tkv = _past_tile(p_len, 1024)
    n_past = p_len // tkv
    cache = pl.BlockSpec((None, None, heads, tkv, hd), lambda b, j: (l, b, 0, jnp.minimum(j, n_past - 1), 0))
    new = pl.BlockSpec((tn, w), lambda b, j: (b, 0))
    qmap = lambda b, j: (b, 0)
    return pl.pallas_call(
        functools.partial(_diff_past_kernel, heads=heads, tq=t_len, tkv=tkv, p_len=p_len, s_len=p_len + t_len,
                          out_scale=out_scale),
        grid=(batch, n_past + 1),
        in_specs=[pl.BlockSpec(memory_space=pltpu.SMEM), pl.BlockSpec((t_len, w), qmap), cache, cache, new, new,
                  pl.BlockSpec((1, hd), lambda b, j: (0, 0))],
        out_specs=pl.BlockSpec((t_len, w), qmap),
        out_shape=jax.ShapeDtypeStruct((batch * t_len, w), BF16),
        scratch_shapes=[pltpu.VMEM((heads, 2 * t_len, hd), BF16)] + _flash_scratch(heads, 2 * t_len, hd),
        compiler_params=_cparams("parallel", "arbitrary"), name="diff_attention_past",
    )(lam.reshape(1, 1), q, kc, vc, kn, vn, g_sub.reshape(1, hd))


def _mla_rope_q(q_ref, cos_ref, sa_ref, sb_ref, heads, scale, store):
    cos, sa, sb = cos_ref[...], sa_ref[...], sb_ref[...]
    nope_w = heads * MLA_NOPE
    for c in range(heads * MLA_ROPE // LANES):
        pe = _rope_chunk(q_ref[:, nope_w + c * LANES:nope_w + (c + 1) * LANES], cos, sa, sb) * scale
        store(2 * c, pe[:, :MLA_ROPE].astype(BF16))
        store(2 * c + 1, pe[:, MLA_ROPE:].astype(BF16))


def _mla_kernel(q_ref, cos_ref, sa_ref, sb_ref, kn_ref, kpe_ref, v_ref, o_ref,
                qn_ref, qpe_ref, m_ref, l_ref, acc_ref, *, heads, tq, tkv, s_len, scale):
    i, j = pl.program_id(1), pl.program_id(2)

    @pl.when(j == 0)
    def _():
        _flash_init(m_ref, l_ref, acc_ref)
        for h in range(heads):
            qn_ref[h] = (q_ref[:, h * MLA_NOPE:(h + 1) * MLA_NOPE] * scale).astype(BF16)

        def store(h, pe):
            qpe_ref[h] = pe
        _mla_rope_q(q_ref, cos_ref, sa_ref, sb_ref, heads, scale, store)

    @pl.when(j <= _last_kv_block(i, tq, tkv, 0, s_len))
    def _():
        bias = _mask_bias(i * tq, tq, j * tkv, tkv, s_len)
        kpe = kpe_ref[...]
        for h in range(heads):
            sl = slice(h * MLA_NOPE, (h + 1) * MLA_NOPE)
            s = (lax.dot_general(qn_ref[h], kn_ref[:, sl], _NT, preferred_element_type=F32)
                 + lax.dot_general(qpe_ref[h], kpe, _NT, preferred_element_type=F32))
            _flash_update(s + bias, v_ref[:, sl], m_ref.at[h], l_ref.at[h], acc_ref.at[h])

    @pl.when(j == pl.num_programs(2) - 1)
    def _():
        for h in range(heads):
            o_ref[:, h * MLA_NOPE:(h + 1) * MLA_NOPE] = _flash_result(l_ref.at[h], acc_ref.at[h]).astype(o_ref.dtype)


def mla_attention(q, tabs, kv_up, kpe, batch, heads, t_len, tq, tkv):
    hw = heads * MLA_NOPE
    nq, nkv = t_len // tq, t_len // tkv
    last = functools.partial(_last_kv_block, tq=tq, tkv=tkv, q0=0, s_len=t_len)
    kvrow = lambda b, i, j: b * nkv + jnp.minimum(j, last(i))
    tab = pl.BlockSpec((tq, LANES), lambda b, i, j: (i, 0))
    return pl.pallas_call(
        functools.partial(_mla_kernel, heads=heads, tq=tq, tkv=tkv, s_len=t_len,
                          scale=(MLA_NOPE + MLA_ROPE) ** -0.5),
        grid=(batch, nq, nkv),
        in_specs=[pl.BlockSpec((tq, q.shape[1]), lambda b, i, j: (b * nq + i, 0)),
                  tab, tab, tab,
                  pl.BlockSpec((tkv, hw), lambda b, i, j: (kvrow(b, i, j), 0)),
                  pl.BlockSpec((tkv, MLA_ROPE), lambda b, i, j: (kvrow(b, i, j), 0)),
                  pl.BlockSpec((tkv, hw), lambda b, i, j: (kvrow(b, i, j), 1))],
        out_specs=pl.BlockSpec((tq, hw), lambda b, i, j: (b * nq + i, 0)),
        out_shape=jax.ShapeDtypeStruct((batch * t_len, hw), BF16),
        scratch_shapes=[pltpu.VMEM((heads, tq, MLA_NOPE), BF16), pltpu.VMEM((heads, tq, MLA_ROPE), BF16)]
                       + _flash_scratch(heads, tq, MLA_NOPE),
        compiler_params=_cparams("parallel", "parallel", "arbitrary"), name="mla_attention",
    )(q, *tabs, kv_up, kpe, kv_up)


def _mla_past_kernel(q_ref, cos_ref, sa_ref, sb_ref, wuk_ref, ckvc_ref, kpec_ref, ckvn_ref, kpen_ref, wuv_ref,
                     o_ref, qlat_ref, qpe_ref, m_ref, l_ref, acc_ref, *, heads, tq, tkv, p_len, s_len, scale):
    j = pl.program_id(1)
    n_past = pl.num_programs(1) - 1

    @pl.when(j == 0)
    def _():
        _flash_init(m_ref, l_ref, acc_ref)
        for h in range(heads):
            qn = q_ref[:, h * MLA_NOPE:(h + 1) * MLA_NOPE].astype(BF16)
            qlat = jnp.dot(qn, wuk_ref[h], preferred_element_type=F32) * scale
            qlat_ref[h * tq:(h + 1) * tq, :] = qlat.astype(BF16)

        def store(h, pe):
            qpe_ref[h * tq:(h + 1) * tq, :] = pe
        _mla_rope_q(q_ref, cos_ref, sa_ref, sb_ref, heads, scale, store)

    def sweep(ckv, kpe, k_start):
        s = (lax.dot_general(qlat_ref[...], ckv, _NT, preferred_element_type=F32)
             + lax.dot_general(qpe_ref[...], kpe, _NT, preferred_element_type=F32))
        s = _add_bias(s, _mask_bias(p_len, tq, k_start, ckv.shape[0], s_len), heads)
        _flash_update(s, ckv, m_ref, l_ref, acc_ref)

    @pl.when(j < n_past)
    def _():
        sweep(ckvc_ref[...].astype(BF16), kpec_ref[...].astype(BF16), j * tkv)

    @pl.when(j == n_past)
    def _():
        sweep(ckvn_ref[...], kpen_ref[...], p_len)
        v_w = wuv_ref.shape[2]
        o_lat = _flash_result(l_ref, acc_ref).astype(BF16)
        for h in range(heads):
            o_h = jnp.dot(o_lat[h * tq:(h + 1) * tq], wuv_ref[h], preferred_element_type=F32)
            o_ref[:, h * v_w:(h + 1) * v_w] = o_h.astype(o_ref.dtype)


def mla_attention_past(q, tabs, wuk_t, wuv, l, ckv_c, kpe_c, ckv_n, kpe_n, batch, t_len):
    _, heads, _, rank = wuk_t.shape
    v_w = wuv.shape[3]
    p_len = ckv_c.shape[2]
    tn = ckv_n.shape[0] // batch
    tkv = _past_tile(p_len, 1024)
    n_past = p_len // tkv
    rows = heads * t_len
    tab = pl.BlockSpec((t_len, LANES), lambda b, j: (0, 0))
    layer4 = lambda a: pl.BlockSpec((None,) + a.shape[1:], lambda b, j: (l, 0, 0, 0))
    cache = lambda w: pl.BlockSpec((None, None, tkv, w), lambda b, j: (l, b, jnp.minimum(j, n_past - 1), 0))
    new = lambda w: pl.BlockSpec((tn, w), lambda b, j: (b, 0))
    return pl.pallas_call(
        functools.partial(_mla_past_kernel, heads=heads, tq=t_len, tkv=tkv, p_len=p_len, s_len=p_len + t_len,
                          scale=(MLA_NOPE + MLA_ROPE) ** -0.5),
        grid=(batch, n_past + 1),
        in_specs=[pl.BlockSpec((t_len, q.shape[1]), lambda b, j: (b, 0)), tab, tab, tab, layer4(wuk_t),
                  cache(rank), cache(MLA_ROPE), new(rank), new(MLA_ROPE), layer4(wuv)],
        out_specs=pl.BlockSpec((t_len, heads * v_w), lambda b, j: (b, 0)),
        out_shape=jax.ShapeDtypeStruct((batch * t_len, heads * v_w), BF16),
        scratch_shapes=[pltpu.VMEM((rows, rank), BF16), pltpu.VMEM((rows, MLA_ROPE), BF16)]
                       + _flash_scratch(None, rows, rank),
        compiler_params=_cparams("parallel", "arbitrary"), name="mla_attention_past",
    )(q, *tabs, wuk_t, ckv_c, kpe_c, ckv_n, kpe_n, wuv)


def _cross_kernel(q_ref, k_ref, v_ref, o_ref, *, heads, hd):
    for h in range(heads):
        sl = slice(h * hd, (h + 1) * hd)
        s = lax.dot_general(q_ref[:, sl], k_ref[:, sl], _NT, preferred_element_type=F32) * (hd ** -0.5)
        p = jnp.exp(s - jnp.max(s, axis=-1, keepdims=True))
        o = jnp.dot(p.astype(BF16), v_ref[:, sl], preferred_element_type=F32)
        o_ref[:, sl] = (o / jnp.sum(p, axis=-1, keepdims=True)).astype(o_ref.dtype)


def cross_attention(q, mk, mv, batch, t_len, mem_len, heads, tq):
    w = q.shape[1]
    nq = t_len // tq
    kv = pl.BlockSpec((mem_len, w), lambda b, i: (b, 0))
    return pl.pallas_call(
        functools.partial(_cross_kernel, heads=heads, hd=w // heads),
        grid=(batch, nq),
        in_specs=[pl.BlockSpec((tq, w), lambda b, i: (b * nq + i, 0)), kv, kv],
        out_specs=pl.BlockSpec((tq, w), lambda b, i: (b * nq + i, 0)),
        out_shape=jax.ShapeDtypeStruct((batch * t_len, w), BF16),
        compiler_params=_cparams("parallel", "parallel"), name="cross_attention",
    )(q, mk, mv)


def _rope_tables(pos):
    half = DIFF_D // 2
    inv = ROPE_THETA ** (-jnp.arange(half, dtype=F32) / half)
    ang = pos.astype(F32)[:, None] * inv[None, :]
    cos, sin, zero = jnp.cos(ang), jnp.sin(ang), jnp.zeros_like(ang)
    reps = LANES // DIFF_D
    return (jnp.tile(cos, (1, 2 * reps)), jnp.tile(jnp.concatenate([-sin, zero], 1), (1, reps)),
            jnp.tile(jnp.concatenate([zero, sin], 1), (1, reps)))


def _round_up(a, b):
    return -(-a // b) * b


def _pad_rows(a, batch, rows):
    t_len = a.shape[0] // batch
    a = jnp.pad(a.reshape(batch, t_len, a.shape[1]), ((0, 0), (0, rows - t_len), (0, 0)))
    return a.reshape(batch * rows, a.shape[2])


def _layer(wts, l, x, h, batch, t_len, pos0, tabs, mem_k, mem_v, past, g_next, tiles, state):
    tm, tt, tq_diff, tkv_diff, tq_mla, tkv_mla, tq_x = tiles
    lam_init = 0.8 - 0.6 * math.exp(-0.3 * l)
    pool_w, diff_w, q_rank, kv_rank = wts["dims"]
    heads_d = diff_w // (2 * DIFF_D)
    heads_m = wts["w_uk_t"].shape[1]

    z = matmul(h, wts["w_in"], l, F32, tm, wts["tn_in"])
    (qd, kd16, vd16, cqn, ckv16, kpe16), state = split_projection(
        z, tabs, wts["g_mla_q"][l], wts["g_mla_kv"][l], wts["dims"], t_len, tt, l, wts["depth"], state)

    keep = POOL_HALO - 1
    if past is None:
        prev = jnp.zeros((batch, POOL_HALO, pool_w), F32)
    else:
        prev = jnp.concatenate([jnp.zeros((batch, 1, pool_w), F32), past["pool"]], axis=1)
    z3 = z.reshape(batch, t_len, z.shape[1])
    if t_len >= keep:
        new_pool = z3[:, t_len - keep:, :pool_w]
    else:
        new_pool = jnp.concatenate([prev[:, 1 + t_len:], z3[:, :, :pool_w]], axis=1)
    y_pool = pool_mix(z, prev, wts["w_pool"], l, wts["pool_scale"][l], batch, t_len, pos0, tt)

    lq = wts["diff_lambda"][l]
    lam = jnp.exp(jnp.sum(lq[0] * lq[1])) - jnp.exp(jnp.sum(lq[2] * lq[3])) + lam_init
    q = matmul(cqn, wts["w_uq"], l, F32, tm, 1024)
    if past is None:
        o_diff = diff_attention(qd, kd16, vd16, lam, wts["g_diff_sub"][l], batch, heads_d, t_len,
                                1.0 - lam_init, tq_diff, tkv_diff)
        kv_up = matmul(ckv16, wts["w_kv_up"], l, BF16, tm, 1024)
        o_mla = mla_attention(q, tabs, kv_up, kpe16, batch, heads_m, t_len, tq_mla, tkv_mla)
    else:
        tn = _round_up(t_len, LANES)
        o_diff = diff_attention_past(qd, past["diff_k"], past["diff_v"], _pad_rows(kd16, batch, tn),
                                     _pad_rows(vd16, batch, tn), l, lam, wts["g_diff_sub"][l], batch, heads_d,
                                     t_len, 1.0 - lam_init)
        o_mla = mla_attention_past(q, tabs, wts["w_uk_t"], wts["w_uv"], l, past["ckv"], past["kpe"],
                                   _pad_rows(ckv16, batch, tn), _pad_rows(kpe16, batch, tn), batch, t_len)

    y = matmul_cat([y_pool, o_diff, o_mla], wts["w_out"], l, F32, tm, 1024)
    x, h = post_res(y, x, wts["g_mix_post"][l], wts["g_x_pre"][l], tt)

    qx = matmul(h, wts["w_mem_q"], l, BF16, tm, 512)
    mem_len, mem_heads = mem_k.shape[0] // batch, wts["mem_heads"]
    oc = cross_attention(qx, mem_k, mem_v, batch, t_len, mem_len, mem_heads, tq_x)
    y = matmul(oc, wts["w_mem_o"], l, F32, tm, 1024)
    x, h = post_res(y, x, wts["g_x_post"][l], wts["g_ff_pre"][l], tt)

    a = matmul_swiglu(h, wts["w_gate"], wts["w_up"], l, 2 * tm, 256)
    y = matmul(a, wts["w_down"], l, F32, 512, 512)
    x, h = post_res(y, x, wts["g_ff_post"][l], g_next, tt)
    return x, h, state, new_pool


def kernel(x_prompt, x_sample, cache_diff_k, cache_diff_v, cache_mla_ckv, cache_mla_kpe, cache_pool, cache_mem_k, cache_mem_v, mem_prompt, g_mix_pre, w_in, w_pool, pool_scale, diff_lambda, g_diff_sub, g_mla_q, w_mla_uq, w_mla_uk, w_mla_uv, g_mla_kv, w_out, g_mix_post, g_mem, w_mem_k, w_mem_v, w_mem_q, w_mem_o, g_x_pre, g_x_post, g_ff_pre, w_gate, w_up, w_down, g_ff_post):
    depth = w_in.shape[0]
    bp, tp, d = x_prompt.shape
    bs, ts, _ = x_sample.shape
    past_len = cache_mla_ckv.shape[2]
    pool_w = cache_pool.shape[3]
    heads_d, diff_w = cache_diff_k.shape[3], cache_diff_k.shape[3] * cache_diff_k.shape[4]
    q_rank, kv_rank = g_mla_q.shape[1], g_mla_kv.shape[1]
    mla_heads = w_mla_uk.shape[2]
    mem_len, mem_heads, mem_hd = cache_mem_k.shape[2:]
    mem_w = mem_heads * mem_hd

    in_w = w_in.shape[2]
    nz = _round_up(in_w, 5 * LANES)
    uq = w_mla_uq.reshape(depth, q_rank, mla_heads, MLA_NOPE + MLA_ROPE)
    wts = {
        "depth": depth, "dims": (pool_w, diff_w, q_rank, kv_rank), "tn_in": nz // 5, "mem_heads": mem_heads,
        "w_in": jnp.pad(w_in, ((0, 0), (0, 0), (0, nz - in_w))).astype(BF16),
        "w_pool": w_pool.astype(BF16), "pool_scale": pool_scale, "diff_lambda": diff_lambda,
        "g_diff_sub": g_diff_sub, "g_mla_q": g_mla_q, "g_mla_kv": g_mla_kv,
        "w_uq": jnp.concatenate([uq[..., :MLA_NOPE].reshape(depth, q_rank, -1),
                                 uq[..., MLA_NOPE:].reshape(depth, q_rank, -1)], axis=2).astype(BF16),
        "w_uk_t": jnp.transpose(w_mla_uk, (0, 2, 3, 1)).astype(BF16),
        "w_uv": jnp.transpose(w_mla_uv, (0, 2, 1, 3)).astype(BF16),
        "w_kv_up": jnp.concatenate([w_mla_uk.reshape(depth, kv_rank, -1),
                                    w_mla_uv.reshape(depth, kv_rank, -1)], axis=2).astype(BF16),
        "w_out": w_out.astype(BF16), "g_mix_post": g_mix_post,
        "w_mem_q": w_mem_q.astype(BF16), "w_mem_o": w_mem_o.astype(BF16),
        "g_x_pre": g_x_pre, "g_x_post": g_x_post, "g_ff_pre": g_ff_pre,
        "w_gate": w_gate.astype(BF16), "w_up": w_up.astype(BF16), "w_down": w_down.astype(BF16),
        "g_ff_post": g_ff_post,
    }
    w_mem_kv = jnp.concatenate([w_mem_k, w_mem_v], axis=2).astype(BF16)

    tabs_p = _rope_tables(jnp.arange(tp))
    tabs_s = _rope_tables(past_len + jnp.arange(ts))
    tiles_p = (1024, 256, 512, 512, 512, 512, 512)
    tiles_s = (bs * ts, ts, ts, 0, ts, 0, ts)

    xp = x_prompt.reshape(bp * tp, d)
    xs = x_sample.reshape(bs * ts, d)
    hp = rmsnorm_bf16(xp, g_mix_pre[0], tiles_p[1])
    hs = rmsnorm_bf16(xs, g_mix_pre[0], tiles_s[1])
    mem_n = mem_prompt.reshape(bp * mem_len, d)
    mem_k_s = cache_mem_k.reshape(depth, bs * mem_len, mem_w).astype(BF16)
    mem_v_s = cache_mem_v.reshape(depth, bs * mem_len, mem_w).astype(BF16)
    past = {"diff_k": jnp.transpose(cache_diff_k, (0, 1, 3, 2, 4)).astype(BF16),
            "diff_v": jnp.transpose(cache_diff_v, (0, 1, 3, 2, 4)).astype(BF16),
            "ckv": cache_mla_ckv, "kpe": cache_mla_kpe}

    state_p = state_s = None
    pools_p, pools_s, mem_ks, mem_vs = [], [], [], []
    for l in range(depth):
        g_next = g_mix_pre[l + 1] if l + 1 < depth else None
        m = rmsnorm_bf16(mem_n, g_mem[l], 256)
        mkv = matmul(m, w_mem_kv, l, F32, 1024, 1024)
        mk, mv = mkv[:, :mem_w], mkv[:, mem_w:]
        mem_ks.append(mk.reshape(bp, mem_len, mem_heads, mem_hd))
        mem_vs.append(mv.reshape(bp, mem_len, mem_heads, mem_hd))
        xp, hp, state_p, pool_p = _layer(wts, l, xp, hp, bp, tp, 0, tabs_p, mk.astype(BF16), mv.astype(BF16),
                                         None, g_next, tiles_p, state_p)
        pools_p.append(pool_p)
        xs, hs, state_s, pool_s = _layer(wts, l, xs, hs, bs, ts, past_len, tabs_s, mem_k_s[l], mem_v_s[l],
                                         dict(past, pool=cache_pool[l]), g_next, tiles_s, state_s)
        pools_s.append(pool_s)

    hd = 2 * DIFF_D
    shape_p = lambda a, tail: a.reshape((depth, bp, tp) + tail)
    shape_s = lambda a, tail: a.reshape((depth, bs, ts) + tail)
    return (xp.reshape(bp, tp, d), xs.reshape(bs, ts, d),
            shape_p(state_p[0], (heads_d, hd)), shape_p(state_p[1], (heads_d, hd)),
            shape_p(state_p[2], (kv_rank,)), shape_p(state_p[3], (MLA_ROPE,)),
            jnp.stack(pools_p), jnp.stack(mem_ks), jnp.stack(mem_vs),
            shape_s(state_s[0], (heads_d, hd)), shape_s(state_s[1], (heads_d, hd)),
            shape_s(state_s[2], (kv_rank,)), shape_s(state_s[3], (MLA_ROPE,)),
            jnp.stack(pools_s))
```

```python
import functools
import math

import jax
import jax.numpy as jnp
from jax import lax
from jax.experimental import pallas as pl
from jax.experimental.pallas import tpu as pltpu

F32 = jnp.float32
BF16 = jnp.bfloat16

EPS = 1e-6
CHUNK = 64
ROPE_THETA = 10000.0
POOL_WINDOWS = (2, 4, 8, 16)
POOL_HALO = 16
DIFF_D = 64
MLA_NOPE = 128
MLA_ROPE = 64
LANES = 128
MIB = 1024 * 1024
VMEM_LIMIT = 56 * MIB

_NT = (((1,), (1,)), ((), ()))


def _cparams(*sem):
    return pltpu.CompilerParams(dimension_semantics=sem, vmem_limit_bytes=VMEM_LIMIT)


def _rms(x, g):
    return x * lax.rsqrt(jnp.mean(x * x, axis=-1, keepdims=True) + EPS) * g


def _rmsnorm_kernel(x_ref, g_ref, o_ref):
    o_ref[...] = _rms(x_ref[...], g_ref[...]).astype(o_ref.dtype)


def rmsnorm_bf16(x, g, tm):
    m, d = x.shape
    return pl.pallas_call(
        _rmsnorm_kernel,
        grid=(m // tm,),
        in_specs=[pl.BlockSpec((tm, d), lambda i: (i, 0)), pl.BlockSpec((1, d), lambda i: (0, 0))],
        out_specs=pl.BlockSpec((tm, d), lambda i: (i, 0)),
        out_shape=jax.ShapeDtypeStruct((m, d), BF16),
        compiler_params=_cparams("parallel"),
        name="rmsnorm",
    )(x, g.reshape(1, d))


def _post_res_kernel(y_ref, x_ref, gp_ref, gn_ref, xo_ref, ho_ref):
    x = x_ref[...] + _rms(y_ref[...], gp_ref[...])
    xo_ref[...] = x
    ho_ref[...] = _rms(x, gn_ref[...]).astype(ho_ref.dtype)


def _post_res_last_kernel(y_ref, x_ref, gp_ref, xo_ref):
    xo_ref[...] = x_ref[...] + _rms(y_ref[...], gp_ref[...])


def post_res(y, x, g_post, g_next, tm):
    m, d = x.shape
    row = pl.BlockSpec((tm, d), lambda i: (i, 0))
    vec = pl.BlockSpec((1, d), lambda i: (0, 0))
    if g_next is None:
        return pl.pallas_call(
            _post_res_last_kernel, grid=(m // tm,), in_specs=[row, row, vec], out_specs=row,
            out_shape=jax.ShapeDtypeStruct((m, d), F32), compiler_params=_cparams("parallel"),
            name="post_res_last",
        )(y, x, g_post.reshape(1, d)), None
    return pl.pallas_call(
        _post_res_kernel, grid=(m // tm,), in_specs=[row, row, vec, vec], out_specs=[row, row],
        out_shape=[jax.ShapeDtypeStruct((m, d), F32), jax.ShapeDtypeStruct((m, d), BF16)],
        compiler_params=_cparams("parallel"), name="post_res",
    )(y, x, g_post.reshape(1, d), g_next.reshape(1, d))


def _mm_kernel(x_ref, w_ref, o_ref):
    o_ref[...] = jnp.dot(x_ref[...], w_ref[...], preferred_element_type=F32).astype(o_ref.dtype)


def matmul(x, w, l, out_dtype, tm, tn):
    m, k = x.shape
    n = w.shape[2]
    tm, tn = min(tm, m), min(tn, n)
    return pl.pallas_call(
        _mm_kernel, grid=(n // tn, m // tm),
        in_specs=[pl.BlockSpec((tm, k), lambda j, i: (i, 0)), pl.BlockSpec((None, k, tn), lambda j, i: (l, 0, j))],
        out_specs=pl.BlockSpec((tm, tn), lambda j, i: (i, j)),
        out_shape=jax.ShapeDtypeStruct((m, n), out_dtype),
        compiler_params=_cparams("parallel", "parallel"), name="matmul",
    )(x, w)


def _mm_cat_kernel(*refs):
    n_in = (len(refs) - 1) // 2
    acc = jnp.dot(refs[0][...], refs[n_in][...], preferred_element_type=F32)
    for a in range(1, n_in):
        acc += jnp.dot(refs[a][...], refs[n_in + a][...], preferred_element_type=F32)
    refs[-1][...] = acc.astype(refs[-1].dtype)


def matmul_cat(xs, w, l, out_dtype, tm, tn):
    m = xs[0].shape[0]
    n = w.shape[2]
    tm, tn = min(tm, m), min(tn, n)
    x_specs, w_specs, off = [], [], 0
    for x in xs:
        k = x.shape[1]
        assert off % k == 0, "each input's row band of w must start on a multiple of its own width"
        x_specs.append(pl.BlockSpec((tm, k), lambda j, i: (i, 0)))
        w_specs.append(pl.BlockSpec((None, k, tn), functools.partial(lambda j, i, r: (l, r, j), r=off // k)))
        off += k
    return pl.pallas_call(
        _mm_cat_kernel, grid=(n // tn, m // tm),
        in_specs=x_specs + w_specs,
        out_specs=pl.BlockSpec((tm, tn), lambda j, i: (i, j)),
        out_shape=jax.ShapeDtypeStruct((m, n), out_dtype),
        compiler_params=_cparams("parallel", "parallel"), name="matmul_cat",
    )(*xs, *([w] * len(xs)))


def _swiglu_kernel(x_ref, wg_ref, wu_ref, o_ref):
    x = x_ref[...]
    g = jnp.dot(x, wg_ref[...], preferred_element_type=F32)
    u = jnp.dot(x, wu_ref[...], preferred_element_type=F32)
    o_ref[...] = (g * jax.nn.sigmoid(g) * u).astype(o_ref.dtype)


def matmul_swiglu(x, wg, wu, l, tm, tn):
    m, k = x.shape
    n = wg.shape[2]
    tm = min(tm, m)
    wspec = pl.BlockSpec((None, k, tn), lambda i, j: (l, 0, j))
    return pl.pallas_call(
        _swiglu_kernel, grid=(m // tm, n // tn),
        in_specs=[pl.BlockSpec((tm, k), lambda i, j: (i, 0)), wspec, wspec],
        out_specs=pl.BlockSpec((tm, tn), lambda i, j: (i, j)),
        out_shape=jax.ShapeDtypeStruct((m, n), BF16),
        compiler_params=_cparams("parallel", "parallel"), name="matmul_swiglu",
    )(x, wg, wu)


def _rope_chunk(x, cos, sa, sb):
    return x * cos + pltpu.roll(x, LANES - 32, 1) * sa + pltpu.roll(x, 32, 1) * sb


_SPLIT_INPUTS = 6


def _split_kernel(*refs, offs, q_scale):
    z_ref, cos_ref, sa_ref, sb_ref, gq_ref, gkv_ref = refs[:_SPLIT_INPUTS]
    (qd_ref, kd32_ref, kd16_ref, vd32_ref, vd16_ref, cq_ref, ckv32_ref, ckv16_ref,
     kpe32_ref, kpe16_ref, kc16_ref) = refs[-11:]
    o_qd, o_kd, o_vd, o_cq, o_ckv, o_kpe = offs
    cos, sa, sb = cos_ref[...], sa_ref[...], sb_ref[...]
    width = vd32_ref.shape[1]
    for c in range(width // LANES):
        sl = slice(c * LANES, (c + 1) * LANES)
        q = _rope_chunk(z_ref[:, o_qd + c * LANES:o_qd + (c + 1) * LANES], cos, sa, sb)
        qd_ref[:, sl] = (q * q_scale).astype(BF16)
        k = _rope_chunk(z_ref[:, o_kd + c * LANES:o_kd + (c + 1) * LANES], cos, sa, sb)
        kd32_ref[:, sl] = k
        kd16_ref[:, sl] = k.astype(BF16)
    v = z_ref[:, o_vd:o_vd + width]
    vd32_ref[...] = v
    vd16_ref[...] = v.astype(BF16)
    cq_ref[...] = _rms(z_ref[:, o_cq:o_cq + cq_ref.shape[1]], gq_ref[...]).astype(BF16)
    ckv = _rms(z_ref[:, o_ckv:o_ckv + ckv32_ref.shape[1]], gkv_ref[...])
    ckv32_ref[...] = ckv
    ckv16_ref[...] = ckv.astype(BF16)
    kpe_chunk = _rope_chunk(z_ref[:, o_kpe:o_kpe + LANES], cos, sa, sb)
    kpe = kpe_chunk[:, :MLA_ROPE]
    kpe32_ref[...] = kpe
    kpe16_ref[...] = kpe.astype(BF16)
    rank = ckv32_ref.shape[1]
    kc16_ref[:, 0:rank] = ckv.astype(BF16)
    kc16_ref[:, rank:rank + LANES] = kpe_chunk.astype(BF16)


def split_projection(z, tabs, g_q, g_kv, dims, t_len, tt, l, depth, state):
    m, nz = z.shape
    pool_w, diff_w, q_rank, kv_rank = dims
    offs = (pool_w, pool_w + diff_w, pool_w + 2 * diff_w, pool_w + 3 * diff_w,
            pool_w + 3 * diff_w + q_rank, pool_w + 3 * diff_w + q_rank + kv_rank)
    nt = t_len // tt
    row = lambda w: pl.BlockSpec((tt, w), lambda i: (i, 0))
    lay = lambda w: pl.BlockSpec((None, tt, w), lambda i: (l, i, 0))
    tab = pl.BlockSpec((tt, LANES), lambda i: (i % nt, 0))
    vec = lambda w: pl.BlockSpec((1, w), lambda i: (0, 0))
    sds = lambda w, dt: jax.ShapeDtypeStruct((m, w), dt)
    stk = lambda w: jax.ShapeDtypeStruct((depth, m, w), F32)
    state = () if state is None else tuple(state)
    state_out = (1, 3, 6, 8)
    outs = pl.pallas_call(
        functools.partial(_split_kernel, offs=offs, q_scale=DIFF_D ** -0.5),
        grid=(m // tt,),
        in_specs=[row(nz), tab, tab, tab, vec(q_rank), vec(kv_rank)]
                 + [pl.BlockSpec(memory_space=pl.ANY)] * len(state),
        out_specs=[row(diff_w), lay(diff_w), row(diff_w), lay(diff_w), row(diff_w), row(q_rank),
                   lay(kv_rank), row(kv_rank), lay(MLA_ROPE), row(MLA_ROPE), row(kv_rank + LANES)],
        out_shape=[sds(diff_w, BF16), stk(diff_w), sds(diff_w, BF16), stk(diff_w), sds(diff_w, BF16),
                   sds(q_rank, BF16), stk(kv_rank), sds(kv_rank, BF16), stk(MLA_ROPE), sds(MLA_ROPE, BF16),
                   sds(kv_rank + LANES, BF16)],
        input_output_aliases={_SPLIT_INPUTS + a: o for a, o in enumerate(state_out[:len(state)])},
        compiler_params=_cparams("parallel"), name="split_projection",
    )(z, *tabs, g_q.reshape(1, q_rank), g_kv.reshape(1, kv_rank), *state)
    qd, kd32, kd16, vd32, vd16, cqn, ckv32, ckv16, kpe32, kpe16, kc16 = outs
    return (qd, kd16, vd16, cqn, ckv16, kpe16, kc16), (kd32, vd32, ckv32, kpe32)


def _pool_kernel(z_ref, halo_ref, prev_ref, wp_ref, sc_ref, o_ref, ext_ref, *, tt, pos0):
    i = pl.program_id(1)
    ext_ref[0:POOL_HALO, :] = jnp.where(i == 0, prev_ref[0], halo_ref[...])
    ext_ref[POOL_HALO:POOL_HALO + tt, :] = z_ref[...]
    pos = pos0 + i * tt + lax.broadcasted_iota(jnp.int32, (tt, 1), 0)
    group = wp_ref.shape[1]
    for g, w in enumerate(POOL_WINDOWS):
        cs = slice(g * group, (g + 1) * group)
        s = ext_ref[POOL_HALO:POOL_HALO + tt, cs]
        u = s
        for j in range(1, w):
            s = s + ext_ref[POOL_HALO - j:POOL_HALO - j + tt, cs]
        cnt = jnp.minimum(w, pos + 1).astype(F32)
        d = s / cnt - u
        y = jnp.dot(d.astype(BF16), wp_ref[g], preferred_element_type=F32) * sc_ref[:, cs]
        o_ref[:, cs] = y.astype(o_ref.dtype)


def pool_mix(z, prev, w_pool, l, scale, batch, t_len, pos0, tt):
    m = z.shape[0]
    c = prev.shape[2]
    nt = t_len // tt
    hb = tt // POOL_HALO
    return pl.pallas_call(
        functools.partial(_pool_kernel, tt=tt, pos0=pos0),
        grid=(batch, nt),
        in_specs=[pl.BlockSpec((tt, c), lambda b, i: (b * nt + i, 0)),
                  pl.BlockSpec((POOL_HALO, c), lambda b, i: (jnp.maximum((b * nt + i) * hb - 1, 0), 0)),
                  pl.BlockSpec((1, POOL_HALO, c), lambda b, i: (b, 0, 0)),
                  pl.BlockSpec((None,) + w_pool.shape[1:], lambda b, i: (l, 0, 0, 0)),
                  pl.BlockSpec((1, c), lambda b, i: (0, 0))],
        out_specs=pl.BlockSpec((tt, c), lambda b, i: (b * nt + i, 0)),
        out_shape=jax.ShapeDtypeStruct((m, c), BF16),
        scratch_shapes=[pltpu.VMEM((POOL_HALO + tt, c), F32)],
        compiler_params=_cparams("parallel", "parallel"), name="pool_mix",
    )(z, z, prev, w_pool, scale.reshape(1, c))


def _last_kv_block(i, tq, tkv, q0, s_len):
    last_q = q0 + (i + 1) * tq - 1
    visible = jnp.minimum((last_q // CHUNK + 1) * CHUNK, s_len)
    return (visible - 1) // tkv


def _mask_bias(q_start, tq, k_start, tkv, s_len):
    qpos = q_start + lax.broadcasted_iota(jnp.int32, (tq, 1), 0)
    kpos = k_start + lax.broadcasted_iota(jnp.int32, (1, tkv), 1)
    shift = CHUNK.bit_length() - 1
    ok = (jnp.right_shift(kpos, shift) <= jnp.right_shift(qpos, shift)) & (kpos < s_len)
    return jnp.where(ok, 0.0, -jnp.inf).astype(F32)


def _add_bias(s, bias, groups):
    rows, tkv = s.shape
    return (s.reshape(groups, rows // groups, tkv) + bias[None]).reshape(rows, tkv)


def _lane_tile(x, width):
    return x if width == LANES else jnp.concatenate([x] * (width // LANES), axis=1)


def _flash_update(s, v, m_ref, l_ref, acc_ref):
    tkv = s.shape[1]
    m_prev = m_ref[...]
    m_new = jnp.maximum(m_prev, jnp.max(s, axis=-1, keepdims=True))
    alpha = jnp.exp(m_prev - m_new)
    p = jnp.exp(s - _lane_tile(m_new, tkv))
    psum = p[:, :LANES]
    for c in range(1, tkv // LANES):
        psum = psum + p[:, c * LANES:(c + 1) * LANES]
    l_ref[...] = alpha * l_ref[...] + psum
    acc_ref[...] = (_lane_tile(alpha, acc_ref.shape[1]) * acc_ref[...]
                    + jnp.dot(p.astype(BF16), v, preferred_element_type=F32))
    m_ref[...] = m_new


def _flash_init(m_ref, l_ref, acc_ref):
    m_ref[...] = jnp.full(m_ref.shape, -jnp.inf, F32)
    l_ref[...] = jnp.zeros(l_ref.shape, F32)
    acc_ref[...] = jnp.zeros(acc_ref.shape, F32)


def _flash_result(l_ref, acc_ref):
    return acc_ref[...] / jnp.sum(l_ref[...], axis=-1, keepdims=True)


def _flash_scratch(groups, rows, dv):
    lead = () if groups is None else (groups,)
    return [pltpu.VMEM(lead + (rows, LANES), F32), pltpu.VMEM(lead + (rows, LANES), F32),
            pltpu.VMEM(lead + (rows, dv), F32)]


def _full_kv_blocks(i, tq, tkv, q0, s_len):
    first_q = q0 + i * tq
    return jnp.minimum((first_q // CHUNK + 1) * CHUNK, s_len) // tkv


def _with_ones(v):
    return jnp.concatenate([v, jnp.ones((v.shape[0], LANES), v.dtype)], axis=1)


def _flash_update_wide(s, v_ones, m_ref, acc_ref):
    tkv = s.shape[1]
    m_prev = m_ref[...]
    m_new = jnp.maximum(m_prev, jnp.max(s, axis=-1, keepdims=True))
    alpha = jnp.exp(m_prev - m_new)
    p = jnp.exp((s - _lane_tile(m_new, tkv)).astype(BF16))
    acc_ref[...] = (_lane_tile(alpha, acc_ref.shape[1]) * acc_ref[...]
                    + jnp.dot(p, v_ones, preferred_element_type=F32))
    m_ref[...] = m_new


def _flash_init_wide(m_ref, acc_ref):
    m_ref[...] = jnp.full(m_ref.shape, -jnp.inf, F32)
    acc_ref[...] = jnp.zeros(acc_ref.shape, F32)


def _flash_result_wide(acc_ref):
    dv = acc_ref.shape[-1] - LANES
    assert dv == LANES
    return acc_ref[:, 0:dv] / acc_ref[:, dv:dv + LANES]


def _flash_scratch_wide(groups, rows, dv):
    return [pltpu.VMEM((groups, rows, LANES), F32), pltpu.VMEM((groups, rows, dv + LANES), F32)]


def _past_tile(p_len, cap):
    tile = max(t for t in range(LANES, cap + 1, LANES) if p_len % t == 0)
    return tile


def _diff_split_q(q_ref, qs_ref, heads, tq):
    hd = 2 * DIFF_D
    lane = lax.broadcasted_iota(jnp.int32, (tq, hd), 1)
    for h in range(heads):
        q = q_ref[:, h * hd:(h + 1) * hd]
        qs_ref[h, 0:tq, :] = jnp.where(lane < DIFF_D, q, jnp.zeros_like(q))
        qs_ref[h, tq:2 * tq, :] = jnp.where(lane >= DIFF_D, q, jnp.zeros_like(q))


def _heads_sweep(qs_ref, k_of, v_of, bias, groups, m_ref, acc_ref, heads):
    for h in range(heads):
        s = lax.dot_general(qs_ref[h], k_of(h), _NT, preferred_element_type=F32)
        if bias is not None:
            s = s + bias if groups == 1 else _add_bias(s, bias, groups)
        _flash_update_wide(s, _with_ones(v_of(h)), m_ref.at[h], acc_ref.at[h])


def _diff_finish(lam_ref, g_ref, o_ref, acc_ref, heads, tq, out_scale):
    hd = 2 * DIFF_D
    lam = lam_ref[0, 0]
    for h in range(heads):
        a = _flash_result_wide(acc_ref.at[h])
        o = a[0:tq] - lam * a[tq:2 * tq]
        o_ref[:, h * hd:(h + 1) * hd] = (_rms(o, g_ref[...]) * out_scale).astype(o_ref.dtype)


def _diff_kernel(lam_ref, q_ref, k_ref, v_ref, g_ref, o_ref, qs_ref, m_ref, acc_ref, *,
                 heads, tq, tkv, s_len, out_scale):
    i, j = pl.program_id(1), pl.program_id(2)
    hd = 2 * DIFF_D
    k_of = lambda h: k_ref[:, h * hd:(h + 1) * hd]
    v_of = lambda h: v_ref[:, h * hd:(h + 1) * hd]
    full = _full_kv_blocks(i, tq, tkv, 0, s_len)

    @pl.when(j == 0)
    def _():
        _flash_init_wide(m_ref, acc_ref)
        _diff_split_q(q_ref, qs_ref, heads, tq)

    @pl.when(j < full)
    def _():
        _heads_sweep(qs_ref, k_of, v_of, None, 2, m_ref, acc_ref, heads)

    @pl.when((j >= full) & (j <= _last_kv_block(i, tq, tkv, 0, s_len)))
    def _():
        _heads_sweep(qs_ref, k_of, v_of, _mask_bias(i * tq, tq, j * tkv, tkv, s_len), 2, m_ref, acc_ref, heads)

    @pl.when(j == pl.num_programs(2) - 1)
    def _():
        _diff_finish(lam_ref, g_ref, o_ref, acc_ref, heads, tq, out_scale)


def diff_attention(q, k, v, lam, g_sub, batch, heads, t_len, out_scale, tq, tkv):
    hd = 2 * DIFF_D
    w = heads * hd
    nq, nkv = t_len // tq, t_len // tkv
    last = functools.partial(_last_kv_block, tq=tq, tkv=tkv, q0=0, s_len=t_len)
    kvmap = lambda b, i, j: (b * nkv + jnp.minimum(j, last(i)), 0)
    qmap = lambda b, i, j: (b * nq + i, 0)
    return pl.pallas_call(
        functools.partial(_diff_kernel, heads=heads, tq=tq, tkv=tkv, s_len=t_len, out_scale=out_scale),
        grid=(batch, nq, nkv),
        in_specs=[pl.BlockSpec(memory_space=pltpu.SMEM),
                  pl.BlockSpec((tq, w), qmap),
                  pl.BlockSpec((tkv, w), kvmap),
                  pl.BlockSpec((tkv, w), kvmap),
                  pl.BlockSpec((1, hd), lambda b, i, j: (0, 0))],
        out_specs=pl.BlockSpec((tq, w), qmap),
        out_shape=jax.ShapeDtypeStruct((batch * t_len, w), BF16),
        scratch_shapes=[pltpu.VMEM((heads, 2 * tq, hd), BF16)] + _flash_scratch_wide(heads, 2 * tq, hd),
        compiler_params=_cparams("parallel", "parallel", "arbitrary"), name="diff_attention",
    )(lam.reshape(1, 1), q, k, v, g_sub.reshape(1, hd))


def _diff_past_kernel(lam_ref, q_ref, kc_ref, vc_ref, kn_ref, vn_ref, g_ref, o_ref,
                      qs_ref, m_ref, acc_ref, *, heads, tq, tkv, p_len, s_len, out_scale):
    j = pl.program_id(1)
    n_past = pl.num_programs(1) - 1
    hd = 2 * DIFF_D

    @pl.when(j == 0)
    def _():
        _flash_init_wide(m_ref, acc_ref)
        _diff_split_q(q_ref, qs_ref, heads, tq)

    @pl.when(j < n_past)
    def _():
        _heads_sweep(qs_ref, lambda h: kc_ref[h], lambda h: vc_ref[h],
                     _mask_bias(p_len, tq, j * tkv, tkv, s_len), 2, m_ref, acc_ref, heads)

    @pl.when(j == n_past)
    def _():
        _heads_sweep(qs_ref, lambda h: kn_ref[:, h * hd:(h + 1) * hd], lambda h: vn_ref[:, h * hd:(h + 1) * hd],
                     _mask_bias(p_len, tq, p_len, kn_ref.shape[0], s_len), 2, m_ref, acc_ref, heads)
        _diff_finish(lam_ref, g_ref, o_ref, acc_ref, heads, tq, out_scale)


def diff_attention_past(q, kc, vc, kn, vn, l, lam, g_sub, batch, heads, t_len, out_scale):
    hd = 2 * DIFF_D
    w = heads * hd
    p_len = kc.shape[3]
    tn = kn.shape[0] // batch
    tkv = _past_tile(p_len, 1024)
    n_past = p_len // tkv
    cache = pl.BlockSpec((None, None, heads, tkv, hd), lambda b, j: (l, b, 0, jnp.minimum(j, n_past - 1), 0))
    new = pl.BlockSpec((tn, w), lambda b, j: (b, 0))
    qmap = lambda b, j: (b, 0)
    return pl.pallas_call(
        functools.partial(_diff_past_kernel, heads=heads, tq=t_len, tkv=tkv, p_len=p_len, s_len=p_len + t_len,
                          out_scale=out_scale),
        grid=(batch, n_past + 1),
        in_specs=[pl.BlockSpec(memory_space=pltpu.SMEM), pl.BlockSpec((t_len, w), qmap), cache, cache, new, new,
                  pl.BlockSpec((1, hd), lambda b, j: (0, 0))],
        out_specs=pl.BlockSpec((t_len, w), qmap),
        out_shape=jax.ShapeDtypeStruct((batch * t_len, w), BF16),
        scratch_shapes=[pltpu.VMEM((heads, 2 * t_len, hd), BF16)] + _flash_scratch_wide(heads, 2 * t_len, hd),
        compiler_params=_cparams("parallel", "arbitrary"), name="diff_attention_past",
    )(lam.reshape(1, 1), q, kc, vc, kn, vn, g_sub.reshape(1, hd))


def _mla_rope_q(q_ref, cos_ref, sa_ref, sb_ref, heads, scale, store):
    cos, sa, sb = cos_ref[...], sa_ref[...], sb_ref[...]
    nope_w = heads * MLA_NOPE
    for c in range(heads * MLA_ROPE // LANES):
        pe = _rope_chunk(q_ref[:, nope_w + c * LANES:nope_w + (c + 1) * LANES], cos, sa, sb) * scale
        store(2 * c, pe[:, :MLA_ROPE].astype(BF16))
        store(2 * c + 1, pe[:, MLA_ROPE:].astype(BF16))


MLA_KEY_W = 2 * MLA_NOPE


def _mla_kernel(q_ref, cos_ref, sa_ref, sb_ref, k_ref, v_ref, o_ref, qc_ref, m_ref, acc_ref, *,
                heads, tq, tkv, s_len, scale):
    i, j = pl.program_id(1), pl.program_id(2)
    k_of = lambda h: k_ref[:, h * MLA_KEY_W:(h + 1) * MLA_KEY_W]
    v_of = lambda h: v_ref[:, h * MLA_NOPE:(h + 1) * MLA_NOPE]
    full = _full_kv_blocks(i, tq, tkv, 0, s_len)

    @pl.when(j == 0)
    def _():
        _flash_init_wide(m_ref, acc_ref)
        cos, sa, sb = cos_ref[...], sa_ref[...], sb_ref[...]
        nope_w = heads * MLA_NOPE
        lane = lax.broadcasted_iota(jnp.int32, (tq, LANES), 1)
        for h in range(heads):
            qc_ref[h, :, 0:MLA_NOPE] = (q_ref[:, h * MLA_NOPE:(h + 1) * MLA_NOPE] * scale).astype(BF16)
        for c in range(heads * MLA_ROPE // LANES):
            pe = _rope_chunk(q_ref[:, nope_w + c * LANES:nope_w + (c + 1) * LANES], cos, sa, sb) * scale
            qc_ref[2 * c, :, MLA_NOPE:MLA_KEY_W] = jnp.where(lane < MLA_ROPE, pe, 0.0).astype(BF16)
            qc_ref[2 * c + 1, :, MLA_NOPE:MLA_KEY_W] = jnp.where(
                lane < MLA_ROPE, pltpu.roll(pe, MLA_ROPE, 1), 0.0).astype(BF16)

    @pl.when(j < full)
    def _():
        _heads_sweep(qc_ref, k_of, v_of, None, 1, m_ref, acc_ref, heads)

    @pl.when((j >= full) & (j <= _last_kv_block(i, tq, tkv, 0, s_len)))
    def _():
        _heads_sweep(qc_ref, k_of, v_of, _mask_bias(i * tq, tq, j * tkv, tkv, s_len), 1, m_ref, acc_ref, heads)

    @pl.when(j == pl.num_programs(2) - 1)
    def _():
        for h in range(heads):
            o_ref[:, h * MLA_NOPE:(h + 1) * MLA_NOPE] = _flash_result_wide(acc_ref.at[h]).astype(o_ref.dtype)


def mla_attention(q, tabs, kv_up, batch, heads, t_len, tq, tkv):
    kw, vw = heads * MLA_KEY_W, heads * MLA_NOPE
    nq, nkv = t_len // tq, t_len // tkv
    last = functools.partial(_last_kv_block, tq=tq, tkv=tkv, q0=0, s_len=t_len)
    kvrow = lambda b, i, j: b * nkv + jnp.minimum(j, last(i))
    tab = pl.BlockSpec((tq, LANES), lambda b, i, j: (i, 0))
    return pl.pallas_call(
        functools.partial(_mla_kernel, heads=heads, tq=tq, tkv=tkv, s_len=t_len,
                          scale=(MLA_NOPE + MLA_ROPE) ** -0.5),
        grid=(batch, nq, nkv),
        in_specs=[pl.BlockSpec((tq, q.shape[1]), lambda b, i, j: (b * nq + i, 0)),
                  tab, tab, tab,
                  pl.BlockSpec((tkv, kw), lambda b, i, j: (kvrow(b, i, j), 0)),
                  pl.BlockSpec((tkv, vw), lambda b, i, j: (kvrow(b, i, j), kw // vw))],
        out_specs=pl.BlockSpec((tq, vw), lambda b, i, j: (b * nq + i, 0)),
        out_shape=jax.ShapeDtypeStruct((batch * t_len, vw), BF16),
        scratch_shapes=[pltpu.VMEM((heads, tq, MLA_KEY_W), BF16)] + _flash_scratch_wide(heads, tq, MLA_NOPE),
        compiler_params=_cparams("parallel", "parallel", "arbitrary"), name="mla_attention",
    )(q, *tabs, kv_up, kv_up)


def _mla_past_kernel(q_ref, cos_ref, sa_ref, sb_ref, wuk_ref, ckvc_ref, kpec_ref, ckvn_ref, kpen_ref, wuv_ref,
                     o_ref, qlat_ref, qpe_ref, m_ref, l_ref, acc_ref, *, heads, tq, tkv, p_len, s_len, scale):
    j = pl.program_id(1)
    n_past = pl.num_programs(1) - 1

    @pl.when(j == 0)
    def _():
        _flash_init(m_ref, l_ref, acc_ref)
        for h in range(heads):
            qn = q_ref[:, h * MLA_NOPE:(h + 1) * MLA_NOPE].astype(BF16)
            qlat = jnp.dot(qn, wuk_ref[h], preferred_element_type=F32) * scale
            qlat_ref[h * tq:(h + 1) * tq, :] = qlat.astype(BF16)

        def store(h, pe):
            qpe_ref[h * tq:(h + 1) * tq, :] = pe
        _mla_rope_q(q_ref, cos_ref, sa_ref, sb_ref, heads, scale, store)

    def sweep(ckv, kpe, k_start):
        s = (lax.dot_general(qlat_ref[...], ckv, _NT, preferred_element_type=F32)
             + lax.dot_general(qpe_ref[...], kpe, _NT, preferred_element_type=F32))
        s = _add_bias(s, _mask_bias(p_len, tq, k_start, ckv.shape[0], s_len), heads)
        _flash_update(s, ckv, m_ref, l_ref, acc_ref)

    @pl.when(j < n_past)
    def _():
        sweep(ckvc_ref[...].astype(BF16), kpec_ref[...].astype(BF16), j * tkv)

    @pl.when(j == n_past)
    def _():
        sweep(ckvn_ref[...], kpen_ref[...], p_len)
        v_w = wuv_ref.shape[2]
        o_lat = _flash_result(l_ref, acc_ref).astype(BF16)
        for h in range(heads):
            o_h = jnp.dot(o_lat[h * tq:(h + 1) * tq], wuv_ref[h], preferred_element_type=F32)
            o_ref[:, h * v_w:(h + 1) * v_w] = o_h.astype(o_ref.dtype)


def mla_attention_past(q, tabs, wuk_t, wuv, l, ckv_c, kpe_c, ckv_n, kpe_n, batch, t_len):
    _, heads, _, rank = wuk_t.shape
    v_w = wuv.shape[3]
    p_len = ckv_c.shape[2]
    tn = ckv_n.shape[0] // batch
    tkv = _past_tile(p_len, 1024)
    n_past = p_len // tkv
    rows = heads * t_len
    tab = pl.BlockSpec((t_len, LANES), lambda b, j: (0, 0))
    layer4 = lambda a: pl.BlockSpec((None,) + a.shape[1:], lambda b, j: (l, 0, 0, 0))
    cache = lambda w: pl.BlockSpec((None, None, tkv, w), lambda b, j: (l, b, jnp.minimum(j, n_past - 1), 0))
    new = lambda w: pl.BlockSpec((tn, w), lambda b, j: (b, 0))
    return pl.pallas_call(
        functools.partial(_mla_past_kernel, heads=heads, tq=t_len, tkv=tkv, p_len=p_len, s_len=p_len + t_len,
                          scale=(MLA_NOPE + MLA_ROPE) ** -0.5),
        grid=(batch, n_past + 1),
        in_specs=[pl.BlockSpec((t_len, q.shape[1]), lambda b, j: (b, 0)), tab, tab, tab, layer4(wuk_t),
                  cache(rank), cache(MLA_ROPE), new(rank), new(MLA_ROPE), layer4(wuv)],
        out_specs=pl.BlockSpec((t_len, heads * v_w), lambda b, j: (b, 0)),
        out_shape=jax.ShapeDtypeStruct((batch * t_len, heads * v_w), BF16),
        scratch_shapes=[pltpu.VMEM((rows, rank), BF16), pltpu.VMEM((rows, MLA_ROPE), BF16)]
                       + _flash_scratch(None, rows, rank),
        compiler_params=_cparams("parallel", "arbitrary"), name="mla_attention_past",
    )(q, *tabs, wuk_t, ckv_c, kpe_c, ckv_n, kpe_n, wuv)


def _cross_kernel(q_ref, k_ref, v_ref, o_ref, *, heads, hd):
    for h in range(heads):
        sl = slice(h * hd, (h + 1) * hd)
        s = lax.dot_general(q_ref[:, sl], k_ref[:, sl], _NT, preferred_element_type=F32) * (hd ** -0.5)
        p = jnp.exp(s - jnp.max(s, axis=-1, keepdims=True))
        o = jnp.dot(p.astype(BF16), v_ref[:, sl], preferred_element_type=F32)
        o_ref[:, sl] = (o / jnp.sum(p, axis=-1, keepdims=True)).astype(o_ref.dtype)


def cross_attention(q, mk, mv, batch, t_len, mem_len, heads, tq):
    w = q.shape[1]
    nq = t_len // tq
    kv = pl.BlockSpec((mem_len, w), lambda b, i: (b, 0))
    return pl.pallas_call(
        functools.partial(_cross_kernel, heads=heads, hd=w // heads),
        grid=(batch, nq),
        in_specs=[pl.BlockSpec((tq, w), lambda b, i: (b * nq + i, 0)), kv, kv],
        out_specs=pl.BlockSpec((tq, w), lambda b, i: (b * nq + i, 0)),
        out_shape=jax.ShapeDtypeStruct((batch * t_len, w), BF16),
        compiler_params=_cparams("parallel", "parallel"), name="cross_attention",
    )(q, mk, mv)


def _rope_tables(pos):
    half = DIFF_D // 2
    inv = ROPE_THETA ** (-jnp.arange(half, dtype=F32) / half)
    ang = pos.astype(F32)[:, None] * inv[None, :]
    cos, sin, zero = jnp.cos(ang), jnp.sin(ang), jnp.zeros_like(ang)
    reps = LANES // DIFF_D
    return (jnp.tile(cos, (1, 2 * reps)), jnp.tile(jnp.concatenate([-sin, zero], 1), (1, reps)),
            jnp.tile(jnp.concatenate([zero, sin], 1), (1, reps)))


def _round_up(a, b):
    return -(-a // b) * b


def _kv_up_weight(w_uk, w_uv):
    depth, rank, heads, nope = w_uk.shape
    k_rows = jnp.concatenate([w_uk, jnp.zeros_like(w_uk)], axis=3).reshape(depth, rank, heads * MLA_KEY_W)
    slot = jnp.concatenate([jnp.zeros((MLA_ROPE, nope), F32), jnp.eye(MLA_ROPE, dtype=F32),
                            jnp.zeros((MLA_ROPE, MLA_KEY_W - nope - MLA_ROPE), F32)], axis=1)
    rope_rows = jnp.broadcast_to(jnp.tile(slot, (1, heads)), (depth, MLA_ROPE, heads * MLA_KEY_W))
    pad = LANES - MLA_ROPE
    w_k = jnp.concatenate([k_rows, rope_rows, jnp.zeros((depth, pad, heads * MLA_KEY_W), F32)], axis=1)
    w_v = jnp.concatenate([w_uv.reshape(depth, rank, -1), jnp.zeros((depth, LANES, heads * w_uv.shape[3]), F32)],
                          axis=1)
    return jnp.concatenate([w_k, w_v], axis=2).astype(BF16)


def _pad_rows(a, batch, rows):
    t_len = a.shape[0] // batch
    a = jnp.pad(a.reshape(batch, t_len, a.shape[1]), ((0, 0), (0, rows - t_len), (0, 0)))
    return a.reshape(batch * rows, a.shape[2])


def _layer(wts, l, x, h, batch, t_len, pos0, tabs, mem_k, mem_v, past, g_next, tiles, state):
    tm, tt, tq_diff, tkv_diff, tq_mla, tkv_mla, tq_x = tiles
    lam_init = 0.8 - 0.6 * math.exp(-0.3 * l)
    pool_w, diff_w, q_rank, kv_rank = wts["dims"]
    heads_d = diff_w // (2 * DIFF_D)
    heads_m = wts["w_uk_t"].shape[1]

    z = matmul(h, wts["w_in"], l, F32, tm, wts["tn_in"])
    (qd, kd16, vd16, cqn, ckv16, kpe16, kc16), state = split_projection(
        z, tabs, wts["g_mla_q"][l], wts["g_mla_kv"][l], wts["dims"], t_len, tt, l, wts["depth"], state)

    keep = POOL_HALO - 1
    if past is None:
        prev = jnp.zeros((batch, POOL_HALO, pool_w), F32)
    else:
        prev = jnp.concatenate([jnp.zeros((batch, 1, pool_w), F32), past["pool"]], axis=1)
    z3 = z.reshape(batch, t_len, z.shape[1])
    if t_len >= keep:
        new_pool = z3[:, t_len - keep:, :pool_w]
    else:
        new_pool = jnp.concatenate([prev[:, 1 + t_len:], z3[:, :, :pool_w]], axis=1)
    y_pool = pool_mix(z, prev, wts["w_pool"], l, wts["pool_scale"][l], batch, t_len, pos0, tt)

    lq = wts["diff_lambda"][l]
    lam = jnp.exp(jnp.sum(lq[0] * lq[1])) - jnp.exp(jnp.sum(lq[2] * lq[3])) + lam_init
    q = matmul(cqn, wts["w_uq"], l, F32, tm, 1024)
    if past is None:
        o_diff = diff_attention(qd, kd16, vd16, lam, wts["g_diff_sub"][l], batch, heads_d, t_len,
                                1.0 - lam_init, tq_diff, tkv_diff)
        kv_up = matmul(kc16, wts["w_kv_up"], l, BF16, tm, heads_m * MLA_NOPE)
        o_mla = mla_attention(q, tabs, kv_up, batch, heads_m, t_len, tq_mla, tkv_mla)
    else:
        tn = _round_up(t_len, LANES)
        o_diff = diff_attention_past(qd, past["diff_k"], past["diff_v"], _pad_rows(kd16, batch, tn),
                                     _pad_rows(vd16, batch, tn), l, lam, wts["g_diff_sub"][l], batch, heads_d,
                                     t_len, 1.0 - lam_init)
        o_mla = mla_attention_past(q, tabs, wts["w_uk_t"], wts["w_uv"], l, past["ckv"], past["kpe"],
                                   _pad_rows(ckv16, batch, tn), _pad_rows(kpe16, batch, tn), batch, t_len)

    y = matmul_cat([y_pool, o_diff, o_mla], wts["w_out"], l, F32, tm, 1024)
    x, h = post_res(y, x, wts["g_mix_post"][l], wts["g_x_pre"][l], tt)

    qx = matmul(h, wts["w_mem_q"], l, BF16, tm, 512)
    mem_len, mem_heads = mem_k.shape[0] // batch, wts["mem_heads"]
    oc = cross_attention(qx, mem_k, mem_v, batch, t_len, mem_len, mem_heads, tq_x)
    y = matmul(oc, wts["w_mem_o"], l, F32, tm, 1024)
    x, h = post_res(y, x, wts["g_x_post"][l], wts["g_ff_pre"][l], tt)

    a = matmul_swiglu(h, wts["w_gate"], wts["w_up"], l, 2 * tm, 256)
    y = matmul(a, wts["w_down"], l, F32, 512, 512)
    x, h = post_res(y, x, wts["g_ff_post"][l], g_next, tt)
    return x, h, state, new_pool


def kernel(x_prompt, x_sample, cache_diff_k, cache_diff_v, cache_mla_ckv, cache_mla_kpe, cache_pool, cache_mem_k, cache_mem_v, mem_prompt, g_mix_pre, w_in, w_pool, pool_scale, diff_lambda, g_diff_sub, g_mla_q, w_mla_uq, w_mla_uk, w_mla_uv, g_mla_kv, w_out, g_mix_post, g_mem, w_mem_k, w_mem_v, w_mem_q, w_mem_o, g_x_pre, g_x_post, g_ff_pre, w_gate, w_up, w_down, g_ff_post):
    depth = w_in.shape[0]
    bp, tp, d = x_prompt.shape
    bs, ts, _ = x_sample.shape
    past_len = cache_mla_ckv.shape[2]
    pool_w = cache_pool.shape[3]
    heads_d, diff_w = cache_diff_k.shape[3], cache_diff_k.shape[3] * cache_diff_k.shape[4]
    q_rank, kv_rank = g_mla_q.shape[1], g_mla_kv.shape[1]
    mla_heads = w_mla_uk.shape[2]
    mem_len, mem_heads, mem_hd = cache_mem_k.shape[2:]
    mem_w = mem_heads * mem_hd

    in_w = w_in.shape[2]
    nz = _round_up(in_w, 5 * LANES)
    uq = w_mla_uq.reshape(depth, q_rank, mla_heads, MLA_NOPE + MLA_ROPE)
    wts = {
        "depth": depth, "dims": (pool_w, diff_w, q_rank, kv_rank), "tn_in": nz // 5, "mem_heads": mem_heads,
        "w_in": jnp.pad(w_in, ((0, 0), (0, 0), (0, nz - in_w))).astype(BF16),
        "w_pool": w_pool.astype(BF16), "pool_scale": pool_scale, "diff_lambda": diff_lambda,
        "g_diff_sub": g_diff_sub, "g_mla_q": g_mla_q, "g_mla_kv": g_mla_kv,
        "w_uq": jnp.concatenate([uq[..., :MLA_NOPE].reshape(depth, q_rank, -1),
                                 uq[..., MLA_NOPE:].reshape(depth, q_rank, -1)], axis=2).astype(BF16),
        "w_uk_t": jnp.transpose(w_mla_uk, (0, 2, 3, 1)).astype(BF16),
        "w_uv": jnp.transpose(w_mla_uv, (0, 2, 1, 3)).astype(BF16),
        "w_kv_up": _kv_up_weight(w_mla_uk, w_mla_uv),
        "w_out": w_out.astype(BF16), "g_mix_post": g_mix_post,
        "w_mem_q": w_mem_q.astype(BF16), "w_mem_o": w_mem_o.astype(BF16),
        "g_x_pre": g_x_pre, "g_x_post": g_x_post, "g_ff_pre": g_ff_pre,
        "w_gate": w_gate.astype(BF16), "w_up": w_up.astype(BF16), "w_down": w_down.astype(BF16),
        "g_ff_post": g_ff_post,
    }
    w_mem_kv = jnp.concatenate([w_mem_k, w_mem_v], axis=2).astype(BF16)

    tabs_p = _rope_tables(jnp.arange(tp))
    tabs_s = _rope_tables(past_len + jnp.arange(ts))
    tiles_p = (1024, 256, 512, 512, 512, 512, 512)
    tiles_s = (bs * ts, ts, ts, 0, ts, 0, ts)

    xp = x_prompt.reshape(bp * tp, d)
    xs = x_sample.reshape(bs * ts, d)
    hp = rmsnorm_bf16(xp, g_mix_pre[0], tiles_p[1])
    hs = rmsnorm_bf16(xs, g_mix_pre[0], tiles_s[1])
    mem_n = mem_prompt.reshape(bp * mem_len, d)
    mem_k_s = cache_mem_k.reshape(depth, bs * mem_len, mem_w).astype(BF16)
    mem_v_s = cache_mem_v.reshape(depth, bs * mem_len, mem_w).astype(BF16)
    past = {"diff_k": jnp.transpose(cache_diff_k, (0, 1, 3, 2, 4)).astype(BF16),
            "diff_v": jnp.transpose(cache_diff_v, (0, 1, 3, 2, 4)).astype(BF16),
            "ckv": cache_mla_ckv, "kpe": cache_mla_kpe}

    state_p = state_s = None
    pools_p, pools_s, mem_ks, mem_vs = [], [], [], []
    for l in range(depth):
        g_next = g_mix_pre[l + 1] if l + 1 < depth else None
        m = rmsnorm_bf16(mem_n, g_mem[l], 256)
        mkv = matmul(m, w_mem_kv, l, F32, 1024, 1024)
        mk, mv = mkv[:, :mem_w], mkv[:, mem_w:]
        mem_ks.append(mk.reshape(bp, mem_len, mem_heads, mem_hd))
        mem_vs.append(mv.reshape(bp, mem_len, mem_heads, mem_hd))
        xp, hp, state_p, pool_p = _layer(wts, l, xp, hp, bp, tp, 0, tabs_p, mk.astype(BF16), mv.astype(BF16),
                                         None, g_next, tiles_p, state_p)
        pools_p.append(pool_p)
        xs, hs, state_s, pool_s = _layer(wts, l, xs, hs, bs, ts, past_len, tabs_s, mem_k_s[l], mem_v_s[l],
                                         dict(past, pool=cache_pool[l]), g_next, tiles_s, state_s)
        pools_s.append(pool_s)

    hd = 2 * DIFF_D
    shape_p = lambda a, tail: a.reshape((depth, bp, tp) + tail)
    shape_s = lambda a, tail: a.reshape((depth, bs, ts) + tail)
    return (xp.reshape(bp, tp, d), xs.reshape(bs, ts, d),
            shape_p(state_p[0], (heads_d, hd)), shape_p(state_p[1], (heads_d, hd)),
            shape_p(state_p[2], (kv_rank,)), shape_p(state_p[3], (MLA_ROPE,)),
            jnp.stack(pools_p), jnp.stack(mem_ks), jnp.stack(mem_vs),
            shape_s(state_s[0], (heads_d, hd)), shape_s(state_s[1], (heads_d, hd)),
            shape_s(state_s[2], (kv_rank,)), shape_s(state_s[3], (MLA_ROPE,)),
            jnp.stack(pools_s))
```

```python
import functools
import math

import jax
import jax.numpy as jnp
from jax import lax
from jax.experimental import pallas as pl
from jax.experimental.pallas import tpu as pltpu

F32 = jnp.float32
BF16 = jnp.bfloat16

EPS = 1e-6
CHUNK = 64
ROPE_THETA = 10000.0
POOL_WINDOWS = (2, 4, 8, 16)
POOL_HALO = 16
DIFF_D = 64
MLA_NOPE = 128
MLA_ROPE = 64
LANES = 128
MIB = 1024 * 1024
VMEM_LIMIT = 56 * MIB

_NT = (((1,), (1,)), ((), ()))


def _cparams(*sem):
    return pltpu.CompilerParams(dimension_semantics=sem, vmem_limit_bytes=VMEM_LIMIT)


def _rms(x, g):
    return x * lax.rsqrt(jnp.mean(x * x, axis=-1, keepdims=True) + EPS) * g


def _rmsnorm_kernel(x_ref, g_ref, o_ref):
    o_ref[...] = _rms(x_ref[...], g_ref[...]).astype(o_ref.dtype)


def rmsnorm_bf16(x, g, tm):
    m, d = x.shape
    return pl.pallas_call(
        _rmsnorm_kernel,
        grid=(m // tm,),
        in_specs=[pl.BlockSpec((tm, d), lambda i: (i, 0)), pl.BlockSpec((1, d), lambda i: (0, 0))],
        out_specs=pl.BlockSpec((tm, d), lambda i: (i, 0)),
        out_shape=jax.ShapeDtypeStruct((m, d), BF16),
        compiler_params=_cparams("parallel"),
        name="rmsnorm",
    )(x, g.reshape(1, d))


def _post_res_kernel(y_ref, x_ref, gp_ref, gn_ref, xo_ref, ho_ref):
    x = x_ref[...] + _rms(y_ref[...], gp_ref[...])
    xo_ref[...] = x
    ho_ref[...] = _rms(x, gn_ref[...]).astype(ho_ref.dtype)


def _post_res_last_kernel(y_ref, x_ref, gp_ref, xo_ref):
    xo_ref[...] = x_ref[...] + _rms(y_ref[...], gp_ref[...])


def post_res(y, x, g_post, g_next, tm):
    m, d = x.shape
    row = pl.BlockSpec((tm, d), lambda i: (i, 0))
    vec = pl.BlockSpec((1, d), lambda i: (0, 0))
    if g_next is None:
        return pl.pallas_call(
            _post_res_last_kernel, grid=(m // tm,), in_specs=[row, row, vec], out_specs=row,
            out_shape=jax.ShapeDtypeStruct((m, d), F32), compiler_params=_cparams("parallel"),
            name="post_res_last",
        )(y, x, g_post.reshape(1, d)), None
    return pl.pallas_call(
        _post_res_kernel, grid=(m // tm,), in_specs=[row, row, vec, vec], out_specs=[row, row],
        out_shape=[jax.ShapeDtypeStruct((m, d), F32), jax.ShapeDtypeStruct((m, d), BF16)],
        compiler_params=_cparams("parallel"), name="post_res",
    )(y, x, g_post.reshape(1, d), g_next.reshape(1, d))


def _mm_kernel(x_ref, w_ref, o_ref):
    o_ref[...] = jnp.dot(x_ref[...], w_ref[...], preferred_element_type=F32).astype(o_ref.dtype)


def matmul(x, w, l, out_dtype, tm, tn):
    m, k = x.shape
    n = w.shape[2]
    tm, tn = min(tm, m), min(tn, n)
    return pl.pallas_call(
        _mm_kernel, grid=(n // tn, m // tm),
        in_specs=[pl.BlockSpec((tm, k), lambda j, i: (i, 0)), pl.BlockSpec((None, k, tn), lambda j, i: (l, 0, j))],
        out_specs=pl.BlockSpec((tm, tn), lambda j, i: (i, j)),
        out_shape=jax.ShapeDtypeStruct((m, n), out_dtype),
        compiler_params=_cparams("parallel", "parallel"), name="matmul",
    )(x, w)


def _mm_cat_kernel(*refs):
    n_in = (len(refs) - 1) // 2
    acc = jnp.dot(refs[0][...], refs[n_in][...], preferred_element_type=F32)
    for a in range(1, n_in):
        acc += jnp.dot(refs[a][...], refs[n_in + a][...], preferred_element_type=F32)
    refs[-1][...] = acc.astype(refs[-1].dtype)


def matmul_cat(xs, w, l, out_dtype, tm, tn):
    m = xs[0].shape[0]
    n = w.shape[2]
    tm, tn = min(tm, m), min(tn, n)
    x_specs, w_specs, off = [], [], 0
    for x in xs:
        k = x.shape[1]
        assert off % k == 0, "each input's row band of w must start on a multiple of its own width"
        x_specs.append(pl.BlockSpec((tm, k), lambda j, i: (i, 0)))
        w_specs.append(pl.BlockSpec((None, k, tn), functools.partial(lambda j, i, r: (l, r, j), r=off // k)))
        off += k
    return pl.pallas_call(
        _mm_cat_kernel, grid=(n // tn, m // tm),
        in_specs=x_specs + w_specs,
        out_specs=pl.BlockSpec((tm, tn), lambda j, i: (i, j)),
        out_shape=jax.ShapeDtypeStruct((m, n), out_dtype),
        compiler_params=_cparams("parallel", "parallel"), name="matmul_cat",
    )(*xs, *([w] * len(xs)))


def _swiglu_kernel(x_ref, wg_ref, wu_ref, o_ref):
    x = x_ref[...]
    g = jnp.dot(x, wg_ref[...].astype(BF16), preferred_element_type=F32)
    u = jnp.dot(x, wu_ref[...].astype(BF16), preferred_element_type=F32)
    o_ref[...] = (g * jax.nn.sigmoid(g) * u).astype(o_ref.dtype)


def matmul_swiglu(x, wg, wu, l, tm, tn):
    m, k = x.shape
    n = wg.shape[2]
    tm = min(tm, m)
    wspec = pl.BlockSpec((None, k, tn), lambda i, j: (l, 0, j))
    return pl.pallas_call(
        _swiglu_kernel, grid=(m // tm, n // tn),
        in_specs=[pl.BlockSpec((tm, k), lambda i, j: (i, 0), pipeline_mode=pl.Buffered(1)), wspec, wspec],
        out_specs=pl.BlockSpec((tm, tn), lambda i, j: (i, j)),
        out_shape=jax.ShapeDtypeStruct((m, n), BF16),
        compiler_params=_cparams("parallel", "parallel"), name="matmul_swiglu",
    )(x, wg, wu)


def _rope_chunk(x, cos, sa, sb):
    return x * cos + pltpu.roll(x, LANES - 32, 1) * sa + pltpu.roll(x, 32, 1) * sb


_SPLIT_INPUTS = 6


def _split_kernel(*refs, offs, q_scale):
    z_ref, cos_ref, sa_ref, sb_ref, gq_ref, gkv_ref = refs[:_SPLIT_INPUTS]
    (qd_ref, kd32_ref, kd16_ref, vd32_ref, vd16_ref, cq_ref, ckv32_ref, ckv16_ref,
     kpe32_ref, kpe16_ref, kc16_ref) = refs[-11:]
    o_qd, o_kd, o_vd, o_cq, o_ckv, o_kpe = offs
    cos, sa, sb = cos_ref[...], sa_ref[...], sb_ref[...]
    width = vd32_ref.shape[1]
    for c in range(width // LANES):
        sl = slice(c * LANES, (c + 1) * LANES)
        q = _rope_chunk(z_ref[:, o_qd + c * LANES:o_qd + (c + 1) * LANES], cos, sa, sb)
        qd_ref[:, sl] = (q * q_scale).astype(BF16)
        k = _rope_chunk(z_ref[:, o_kd + c * LANES:o_kd + (c + 1) * LANES], cos, sa, sb)
        kd32_ref[:, sl] = k
        kd16_ref[:, sl] = k.astype(BF16)
    v = z_ref[:, o_vd:o_vd + width]
    vd32_ref[...] = v
    vd16_ref[...] = v.astype(BF16)
    cq_ref[...] = _rms(z_ref[:, o_cq:o_cq + cq_ref.shape[1]], gq_ref[...]).astype(BF16)
    ckv = _rms(z_ref[:, o_ckv:o_ckv + ckv32_ref.shape[1]], gkv_ref[...])
    ckv32_ref[...] = ckv
    ckv16_ref[...] = ckv.astype(BF16)
    kpe_chunk = _rope_chunk(z_ref[:, o_kpe:o_kpe + LANES], cos, sa, sb)
    kpe = kpe_chunk[:, :MLA_ROPE]
    kpe32_ref[...] = kpe
    kpe16_ref[...] = kpe.astype(BF16)
    rank = ckv32_ref.shape[1]
    kc16_ref[:, 0:rank] = ckv.astype(BF16)
    kc16_ref[:, rank:rank + LANES] = kpe_chunk.astype(BF16)


def split_projection(z, tabs, g_q, g_kv, dims, t_len, tt, l, depth, state):
    m, nz = z.shape
    pool_w, diff_w, q_rank, kv_rank = dims
    offs = (pool_w, pool_w + diff_w, pool_w + 2 * diff_w, pool_w + 3 * diff_w,
            pool_w + 3 * diff_w + q_rank, pool_w + 3 * diff_w + q_rank + kv_rank)
    nt = t_len // tt
    row = lambda w: pl.BlockSpec((tt, w), lambda i: (i, 0))
    lay = lambda w: pl.BlockSpec((None, tt, w), lambda i: (l, i, 0))
    tab = pl.BlockSpec((tt, LANES), lambda i: (i % nt, 0))
    vec = lambda w: pl.BlockSpec((1, w), lambda i: (0, 0))
    sds = lambda w, dt: jax.ShapeDtypeStruct((m, w), dt)
    stk = lambda w: jax.ShapeDtypeStruct((depth, m, w), F32)
    state = () if state is None else tuple(state)
    state_out = (1, 3, 6, 8)
    outs = pl.pallas_call(
        functools.partial(_split_kernel, offs=offs, q_scale=DIFF_D ** -0.5),
        grid=(m // tt,),
        in_specs=[row(nz), tab, tab, tab, vec(q_rank), vec(kv_rank)]
                 + [pl.BlockSpec(memory_space=pl.ANY)] * len(state),
        out_specs=[row(diff_w), lay(diff_w), row(diff_w), lay(diff_w), row(diff_w), row(q_rank),
                   lay(kv_rank), row(kv_rank), lay(MLA_ROPE), row(MLA_ROPE), row(kv_rank + LANES)],
        out_shape=[sds(diff_w, BF16), stk(diff_w), sds(diff_w, BF16), stk(diff_w), sds(diff_w, BF16),
                   sds(q_rank, BF16), stk(kv_rank), sds(kv_rank, BF16), stk(MLA_ROPE), sds(MLA_ROPE, BF16),
                   sds(kv_rank + LANES, BF16)],
        input_output_aliases={_SPLIT_INPUTS + a: o for a, o in enumerate(state_out[:len(state)])},
        compiler_params=_cparams("parallel"), name="split_projection",
    )(z, *tabs, g_q.reshape(1, q_rank), g_kv.reshape(1, kv_rank), *state)
    qd, kd32, kd16, vd32, vd16, cqn, ckv32, ckv16, kpe32, kpe16, kc16 = outs
    return (qd, kd16, vd16, cqn, ckv16, kpe16, kc16), (kd32, vd32, ckv32, kpe32)


def _pool_kernel(z_ref, halo_ref, prev_ref, wp_ref, sc_ref, o_ref, ext_ref, *, tt, pos0):
    i = pl.program_id(1)
    ext_ref[0:POOL_HALO, :] = jnp.where(i == 0, prev_ref[0], halo_ref[...])
    ext_ref[POOL_HALO:POOL_HALO + tt, :] = z_ref[...]
    pos = pos0 + i * tt + lax.broadcasted_iota(jnp.int32, (tt, 1), 0)
    group = wp_ref.shape[1]
    for g, w in enumerate(POOL_WINDOWS):
        cs = slice(g * group, (g + 1) * group)
        s = ext_ref[POOL_HALO:POOL_HALO + tt, cs]
        u = s
        for j in range(1, w):
            s = s + ext_ref[POOL_HALO - j:POOL_HALO - j + tt, cs]
        cnt = jnp.minimum(w, pos + 1).astype(F32)
        d = s / cnt - u
        y = jnp.dot(d.astype(BF16), wp_ref[g], preferred_element_type=F32) * sc_ref[:, cs]
        o_ref[:, cs] = y.astype(o_ref.dtype)


def pool_mix(z, prev, w_pool, l, scale, batch, t_len, pos0, tt):
    m = z.shape[0]
    c = prev.shape[2]
    nt = t_len // tt
    hb = tt // POOL_HALO
    return pl.pallas_call(
        functools.partial(_pool_kernel, tt=tt, pos0=pos0),
        grid=(batch, nt),
        in_specs=[pl.BlockSpec((tt, c), lambda b, i: (b * nt + i, 0)),
                  pl.BlockSpec((POOL_HALO, c), lambda b, i: (jnp.maximum((b * nt + i) * hb - 1, 0), 0)),
                  pl.BlockSpec((1, POOL_HALO, c), lambda b, i: (b, 0, 0)),
                  pl.BlockSpec((None,) + w_pool.shape[1:], lambda b, i: (l, 0, 0, 0)),
                  pl.BlockSpec((1, c), lambda b, i: (0, 0))],
        out_specs=pl.BlockSpec((tt, c), lambda b, i: (b * nt + i, 0)),
        out_shape=jax.ShapeDtypeStruct((m, c), BF16),
        scratch_shapes=[pltpu.VMEM((POOL_HALO + tt, c), F32)],
        compiler_params=_cparams("parallel", "parallel"), name="pool_mix",
    )(z, z, prev, w_pool, scale.reshape(1, c))


def _last_kv_block(i, tq, tkv, q0, s_len):
    last_q = q0 + (i + 1) * tq - 1
    visible = jnp.minimum((last_q // CHUNK + 1) * CHUNK, s_len)
    return (visible - 1) // tkv


def _mask_bias(q_start, tq, k_start, tkv, s_len):
    qpos = q_start + lax.broadcasted_iota(jnp.int32, (tq, 1), 0)
    kpos = k_start + lax.broadcasted_iota(jnp.int32, (1, tkv), 1)
    shift = CHUNK.bit_length() - 1
    ok = (jnp.right_shift(kpos, shift) <= jnp.right_shift(qpos, shift)) & (kpos < s_len)
    return jnp.where(ok, 0.0, -jnp.inf).astype(F32)


def _add_bias(s, bias, groups):
    rows, tkv = s.shape
    return (s.reshape(groups, rows // groups, tkv) + bias[None]).reshape(rows, tkv)


def _lane_tile(x, width):
    return x if width == LANES else jnp.concatenate([x] * (width // LANES), axis=1)


def _flash_update(s, v, m_ref, l_ref, acc_ref):
    tkv = s.shape[1]
    m_prev = m_ref[...]
    m_new = jnp.maximum(m_prev, jnp.max(s, axis=-1, keepdims=True))
    alpha = jnp.exp(m_prev - m_new)
    p = jnp.exp(s - _lane_tile(m_new, tkv))
    psum = p[:, :LANES]
    for c in range(1, tkv // LANES):
        psum = psum + p[:, c * LANES:(c + 1) * LANES]
    l_ref[...] = alpha * l_ref[...] + psum
    acc_ref[...] = (_lane_tile(alpha, acc_ref.shape[1]) * acc_ref[...]
                    + jnp.dot(p.astype(BF16), v, preferred_element_type=F32))
    m_ref[...] = m_new


def _flash_init(m_ref, l_ref, acc_ref):
    m_ref[...] = jnp.full(m_ref.shape, -jnp.inf, F32)
    l_ref[...] = jnp.zeros(l_ref.shape, F32)
    acc_ref[...] = jnp.zeros(acc_ref.shape, F32)


def _flash_result(l_ref, acc_ref):
    return acc_ref[...] / jnp.sum(l_ref[...], axis=-1, keepdims=True)


def _flash_scratch(groups, rows, dv):
    lead = () if groups is None else (groups,)
    return [pltpu.VMEM(lead + (rows, LANES), F32), pltpu.VMEM(lead + (rows, LANES), F32),
            pltpu.VMEM(lead + (rows, dv), F32)]


def _full_kv_blocks(i, tq, tkv, q0, s_len):
    first_q = q0 + i * tq
    return jnp.minimum((first_q // CHUNK + 1) * CHUNK, s_len) // tkv


def _with_ones(v):
    return jnp.concatenate([v, jnp.ones((v.shape[0], LANES), v.dtype)], axis=1)


def _flash_update_wide(s, v_ones, m_ref, acc_ref):
    tkv = s.shape[1]
    m_prev = m_ref[...]
    m_new = jnp.maximum(m_prev, jnp.max(s, axis=-1, keepdims=True))
    alpha = jnp.exp(m_prev - m_new)
    p = jnp.exp((s - _lane_tile(m_new, tkv)).astype(BF16))
    acc_ref[...] = (_lane_tile(alpha, acc_ref.shape[1]) * acc_ref[...]
                    + jnp.dot(p, v_ones, preferred_element_type=F32))
    m_ref[...] = m_new


def _flash_init_wide(m_ref, acc_ref):
    m_ref[...] = jnp.full(m_ref.shape, -jnp.inf, F32)
    acc_ref[...] = jnp.zeros(acc_ref.shape, F32)


def _flash_result_wide(acc_ref):
    dv = acc_ref.shape[-1] - LANES
    assert dv == LANES
    return acc_ref[:, 0:dv] / acc_ref[:, dv:dv + LANES]


def _flash_scratch_wide(groups, rows, dv):
    return [pltpu.VMEM((groups, rows, LANES), F32), pltpu.VMEM((groups, rows, dv + LANES), F32)]


def _past_tile(p_len, cap):
    tile = max(t for t in range(LANES, cap + 1, LANES) if p_len % t == 0)
    return tile


def _diff_split_q(q_ref, qs_ref, heads, tq):
    hd = 2 * DIFF_D
    lane = lax.broadcasted_iota(jnp.int32, (tq, hd), 1)
    for h in range(heads):
        q = q_ref[:, h * hd:(h + 1) * hd]
        qs_ref[h, 0:tq, :] = jnp.where(lane < DIFF_D, q, jnp.zeros_like(q))
        qs_ref[h, tq:2 * tq, :] = jnp.where(lane >= DIFF_D, q, jnp.zeros_like(q))


def _heads_sweep(qs_ref, k_of, v_of, bias, groups, m_ref, acc_ref, heads):
    for h in range(heads):
        s = lax.dot_general(qs_ref[h], k_of(h), _NT, preferred_element_type=F32)
        if bias is not None:
            s = s + bias if groups == 1 else _add_bias(s, bias, groups)
        _flash_update_wide(s, _with_ones(v_of(h)), m_ref.at[h], acc_ref.at[h])


def _diff_finish(lam_ref, g_ref, o_ref, acc_ref, heads, tq, out_scale):
    hd = 2 * DIFF_D
    lam = lam_ref[0, 0]
    for h in range(heads):
        a = _flash_result_wide(acc_ref.at[h])
        o = a[0:tq] - lam * a[tq:2 * tq]
        o_ref[:, h * hd:(h + 1) * hd] = (_rms(o, g_ref[...]) * out_scale).astype(o_ref.dtype)


def _diff_kernel(lam_ref, q_ref, k_ref, v_ref, g_ref, o_ref, qs_ref, m_ref, acc_ref, *,
                 heads, tq, tkv, s_len, out_scale):
    i, j = pl.program_id(1), pl.program_id(2)
    hd = 2 * DIFF_D
    k_of = lambda h: k_ref[:, h * hd:(h + 1) * hd]
    v_of = lambda h: v_ref[:, h * hd:(h + 1) * hd]
    full = _full_kv_blocks(i, tq, tkv, 0, s_len)

    @pl.when(j == 0)
    def _():
        _flash_init_wide(m_ref, acc_ref)
        _diff_split_q(q_ref, qs_ref, heads, tq)

    @pl.when(j < full)
    def _():
        _heads_sweep(qs_ref, k_of, v_of, None, 2, m_ref, acc_ref, heads)

    @pl.when((j >= full) & (j <= _last_kv_block(i, tq, tkv, 0, s_len)))
    def _():
        _heads_sweep(qs_ref, k_of, v_of, _mask_bias(i * tq, tq, j * tkv, tkv, s_len), 2, m_ref, acc_ref, heads)

    @pl.when(j == pl.num_programs(2) - 1)
    def _():
        _diff_finish(lam_ref, g_ref, o_ref, acc_ref, heads, tq, out_scale)


def diff_attention(q, k, v, lam, g_sub, batch, heads, t_len, out_scale, tq, tkv):
    hd = 2 * DIFF_D
    w = heads * hd
    nq, nkv = t_len // tq, t_len // tkv
    last = functools.partial(_last_kv_block, tq=tq, tkv=tkv, q0=0, s_len=t_len)
    kvmap = lambda b, i, j: (b * nkv + jnp.minimum(j, last(i)), 0)
    qmap = lambda b, i, j: (b * nq + i, 0)
    return pl.pallas_call(
        functools.partial(_diff_kernel, heads=heads, tq=tq, tkv=tkv, s_len=t_len, out_scale=out_scale),
        grid=(batch, nq, nkv),
        in_specs=[pl.BlockSpec(memory_space=pltpu.SMEM),
                  pl.BlockSpec((tq, w), qmap),
                  pl.BlockSpec((tkv, w), kvmap),
                  pl.BlockSpec((tkv, w), kvmap),
                  pl.BlockSpec((1, hd), lambda b, i, j: (0, 0))],
        out_specs=pl.BlockSpec((tq, w), qmap),
        out_shape=jax.ShapeDtypeStruct((batch * t_len, w), BF16),
        scratch_shapes=[pltpu.VMEM((heads, 2 * tq, hd), BF16)] + _flash_scratch_wide(heads, 2 * tq, hd),
        compiler_params=_cparams("parallel", "parallel", "arbitrary"), name="diff_attention",
    )(lam.reshape(1, 1), q, k, v, g_sub.reshape(1, hd))


def _diff_past_kernel(lam_ref, q_ref, kc_ref, vc_ref, kn_ref, vn_ref, g_ref, o_ref,
                      qs_ref, m_ref, acc_ref, *, heads, tq, tkv, p_len, s_len, out_scale):
    j = pl.program_id(1)
    n_past = pl.num_programs(1) - 1
    hd = 2 * DIFF_D

    @pl.when(j == 0)
    def _():
        _flash_init_wide(m_ref, acc_ref)
        _diff_split_q(q_ref, qs_ref, heads, tq)

    @pl.when(j < n_past)
    def _():
        _heads_sweep(qs_ref, lambda h: kc_ref[h], lambda h: vc_ref[h],
                     _mask_bias(p_len, tq, j * tkv, tkv, s_len), 2, m_ref, acc_ref, heads)

    @pl.when(j == n_past)
    def _():
        _heads_sweep(qs_ref, lambda h: kn_ref[:, h * hd:(h + 1) * hd], lambda h: vn_ref[:, h * hd:(h + 1) * hd],
                     _mask_bias(p_len, tq, p_len, kn_ref.shape[0], s_len), 2, m_ref, acc_ref, heads)
        _diff_finish(lam_ref, g_ref, o_ref, acc_ref, heads, tq, out_scale)


def diff_attention_past(q, kc, vc, kn, vn, l, lam, g_sub, batch, heads, t_len, out_scale):
    hd = 2 * DIFF_D
    w = heads * hd
    p_len = kc.shape[3]
    tn = kn.shape[0] // batch
    tkv = _past_tile(p_len, 1024)
    n_past = p_len // tkv
    cache = pl.BlockSpec((None, None, heads, tkv, hd), lambda b, j: (l, b, 0, jnp.minimum(j, n_past - 1), 0))
    new = pl.BlockSpec((tn, w), lambda b, j: (b, 0))
    qmap = lambda b, j: (b, 0)
    return pl.pallas_call(
        functools.partial(_diff_past_kernel, heads=heads, tq=t_len, tkv=tkv, p_len=p_len, s_len=p_len + t_len,
                          out_scale=out_scale),
        grid=(batch, n_past + 1),
        in_specs=[pl.BlockSpec(memory_space=pltpu.SMEM), pl.BlockSpec((t_len, w), qmap), cache, cache, new, new,
                  pl.BlockSpec((1, hd), lambda b, j: (0, 0))],
        out_specs=pl.BlockSpec((t_len, w), qmap),
        out_shape=jax.ShapeDtypeStruct((batch * t_len, w), BF16),
        scratch_shapes=[pltpu.VMEM((heads, 2 * t_len, hd), BF16)] + _flash_scratch_wide(heads, 2 * t_len, hd),
        compiler_params=_cparams("parallel", "arbitrary"), name="diff_attention_past",
    )(lam.reshape(1, 1), q, kc, vc, kn, vn, g_sub.reshape(1, hd))


def _mla_rope_q(q_ref, cos_ref, sa_ref, sb_ref, heads, scale, store):
    cos, sa, sb = cos_ref[...], sa_ref[...], sb_ref[...]
    nope_w = heads * MLA_NOPE
    for c in range(heads * MLA_ROPE // LANES):
        pe = _rope_chunk(q_ref[:, nope_w + c * LANES:nope_w + (c + 1) * LANES], cos, sa, sb) * scale
        store(2 * c, pe[:, :MLA_ROPE].astype(BF16))
        store(2 * c + 1, pe[:, MLA_ROPE:].astype(BF16))


MLA_KEY_W = 2 * MLA_NOPE


def _mla_kernel(q_ref, cos_ref, sa_ref, sb_ref, k_ref, v_ref, o_ref, qc_ref, m_ref, acc_ref, *,
                heads, tq, tkv, s_len, scale):
    i, j = pl.program_id(1), pl.program_id(2)
    k_of = lambda h: k_ref[:, h * MLA_KEY_W:(h + 1) * MLA_KEY_W]
    v_of = lambda h: v_ref[:, h * MLA_NOPE:(h + 1) * MLA_NOPE]
    full = _full_kv_blocks(i, tq, tkv, 0, s_len)

    @pl.when(j == 0)
    def _():
        _flash_init_wide(m_ref, acc_ref)
        cos, sa, sb = cos_ref[...], sa_ref[...], sb_ref[...]
        nope_w = heads * MLA_NOPE
        lane = lax.broadcasted_iota(jnp.int32, (tq, LANES), 1)
        for h in range(heads):
            qc_ref[h, :, 0:MLA_NOPE] = (q_ref[:, h * MLA_NOPE:(h + 1) * MLA_NOPE] * scale).astype(BF16)
        for c in range(heads * MLA_ROPE // LANES):
            pe = _rope_chunk(q_ref[:, nope_w + c * LANES:nope_w + (c + 1) * LANES], cos, sa, sb) * scale
            qc_ref[2 * c, :, MLA_NOPE:MLA_KEY_W] = jnp.where(lane < MLA_ROPE, pe, 0.0).astype(BF16)
            qc_ref[2 * c + 1, :, MLA_NOPE:MLA_KEY_W] = jnp.where(
                lane < MLA_ROPE, pltpu.roll(pe, MLA_ROPE, 1), 0.0).astype(BF16)

    @pl.when(j < full)
    def _():
        _heads_sweep(qc_ref, k_of, v_of, None, 1, m_ref, acc_ref, heads)

    @pl.when((j >= full) & (j <= _last_kv_block(i, tq, tkv, 0, s_len)))
    def _():
        _heads_sweep(qc_ref, k_of, v_of, _mask_bias(i * tq, tq, j * tkv, tkv, s_len), 1, m_ref, acc_ref, heads)

    @pl.when(j == pl.num_programs(2) - 1)
    def _():
        for h in range(heads):
            o_ref[:, h * MLA_NOPE:(h + 1) * MLA_NOPE] = _flash_result_wide(acc_ref.at[h]).astype(o_ref.dtype)


def mla_attention(q, tabs, kv_up, batch, heads, t_len, tq, tkv):
    kw, vw = heads * MLA_KEY_W, heads * MLA_NOPE
    nq, nkv = t_len // tq, t_len // tkv
    last = functools.partial(_last_kv_block, tq=tq, tkv=tkv, q0=0, s_len=t_len)
    kvrow = lambda b, i, j: b * nkv + jnp.minimum(j, last(i))
    tab = pl.BlockSpec((tq, LANES), lambda b, i, j: (i, 0))
    return pl.pallas_call(
        functools.partial(_mla_kernel, heads=heads, tq=tq, tkv=tkv, s_len=t_len,
                          scale=(MLA_NOPE + MLA_ROPE) ** -0.5),
        grid=(batch, nq, nkv),
        in_specs=[pl.BlockSpec((tq, q.shape[1]), lambda b, i, j: (b * nq + i, 0)),
                  tab, tab, tab,
                  pl.BlockSpec((tkv, kw), lambda b, i, j: (kvrow(b, i, j), 0)),
                  pl.BlockSpec((tkv, vw), lambda b, i, j: (kvrow(b, i, j), kw // vw))],
        out_specs=pl.BlockSpec((tq, vw), lambda b, i, j: (b * nq + i, 0)),
        out_shape=jax.ShapeDtypeStruct((batch * t_len, vw), BF16),
        scratch_shapes=[pltpu.VMEM((heads, tq, MLA_KEY_W), BF16)] + _flash_scratch_wide(heads, tq, MLA_NOPE),
        compiler_params=_cparams("parallel", "parallel", "arbitrary"), name="mla_attention",
    )(q, *tabs, kv_up, kv_up)


def _mla_past_kernel(q_ref, cos_ref, sa_ref, sb_ref, wuk_ref, ckvc_ref, kpec_ref, ckvn_ref, kpen_ref, wuv_ref,
                     o_ref, qlat_ref, qpe_ref, m_ref, l_ref, acc_ref, *, heads, tq, tkv, p_len, s_len, scale):
    j = pl.program_id(1)
    n_past = pl.num_programs(1) - 1

    @pl.when(j == 0)
    def _():
        _flash_init(m_ref, l_ref, acc_ref)
        for h in range(heads):
            qn = q_ref[:, h * MLA_NOPE:(h + 1) * MLA_NOPE].astype(BF16)
            qlat = jnp.dot(qn, wuk_ref[h], preferred_element_type=F32) * scale
            qlat_ref[h * tq:(h + 1) * tq, :] = qlat.astype(BF16)

        def store(h, pe):
            qpe_ref[h * tq:(h + 1) * tq, :] = pe
        _mla_rope_q(q_ref, cos_ref, sa_ref, sb_ref, heads, scale, store)

    def sweep(ckv, kpe, k_start):
        s = (lax.dot_general(qlat_ref[...], ckv, _NT, preferred_element_type=F32)
             + lax.dot_general(qpe_ref[...], kpe, _NT, preferred_element_type=F32))
        s = _add_bias(s, _mask_bias(p_len, tq, k_start, ckv.shape[0], s_len), heads)
        _flash_update(s, ckv, m_ref, l_ref, acc_ref)

    @pl.when(j < n_past)
    def _():
        sweep(ckvc_ref[...].astype(BF16), kpec_ref[...].astype(BF16), j * tkv)

    @pl.when(j == n_past)
    def _():
        sweep(ckvn_ref[...], kpen_ref[...], p_len)
        v_w = wuv_ref.shape[2]
        o_lat = _flash_result(l_ref, acc_ref).astype(BF16)
        for h in range(heads):
            o_h = jnp.dot(o_lat[h * tq:(h + 1) * tq], wuv_ref[h], preferred_element_type=F32)
            o_ref[:, h * v_w:(h + 1) * v_w] = o_h.astype(o_ref.dtype)


def mla_attention_past(q, tabs, wuk_t, wuv, l, ckv_c, kpe_c, ckv_n, kpe_n, batch, t_len):
    _, heads, _, rank = wuk_t.shape
    v_w = wuv.shape[3]
    p_len = ckv_c.shape[2]
    tn = ckv_n.shape[0] // batch
    tkv = _past_tile(p_len, 1024)
    n_past = p_len // tkv
    rows = heads * t_len
    tab = pl.BlockSpec((t_len, LANES), lambda b, j: (0, 0))
    layer4 = lambda a: pl.BlockSpec((None,) + a.shape[1:], lambda b, j: (l, 0, 0, 0))
    cache = lambda w: pl.BlockSpec((None, None, tkv, w), lambda b, j: (l, b, jnp.minimum(j, n_past - 1), 0))
    new = lambda w: pl.BlockSpec((tn, w), lambda b, j: (b, 0))
    return pl.pallas_call(
        functools.partial(_mla_past_kernel, heads=heads, tq=t_len, tkv=tkv, p_len=p_len, s_len=p_len + t_len,
                          scale=(MLA_NOPE + MLA_ROPE) ** -0.5),
        grid=(batch, n_past + 1),
        in_specs=[pl.BlockSpec((t_len, q.shape[1]), lambda b, j: (b, 0)), tab, tab, tab, layer4(wuk_t),
                  cache(rank), cache(MLA_ROPE), new(rank), new(MLA_ROPE), layer4(wuv)],
        out_specs=pl.BlockSpec((t_len, heads * v_w), lambda b, j: (b, 0)),
        out_shape=jax.ShapeDtypeStruct((batch * t_len, heads * v_w), BF16),
        scratch_shapes=[pltpu.VMEM((rows, rank), BF16), pltpu.VMEM((rows, MLA_ROPE), BF16)]
                       + _flash_scratch(None, rows, rank),
        compiler_params=_cparams("parallel", "arbitrary"), name="mla_attention_past",
    )(q, *tabs, wuk_t, ckv_c, kpe_c, ckv_n, kpe_n, wuv)


def _cross_block_kernel(y_ref, x_ref, g_mix_ref, g_pre_ref, wq_ref, k_ref, v_ref, wo_ref, g_post_ref, g_next_ref,
                        xo_ref, ho_ref, *, heads, hd):
    x1 = x_ref[...] + _rms(y_ref[...], g_mix_ref[...])
    q = jnp.dot(_rms(x1, g_pre_ref[...]).astype(BF16), wq_ref[...], preferred_element_type=F32).astype(BF16)
    outs = []
    for h in range(heads):
        sl = slice(h * hd, (h + 1) * hd)
        s = lax.dot_general(q[:, sl], k_ref[:, sl], _NT, preferred_element_type=F32) * (hd ** -0.5)
        p = jnp.exp(s - jnp.max(s, axis=-1, keepdims=True))
        o = jnp.dot(p.astype(BF16), v_ref[:, sl], preferred_element_type=F32)
        outs.append((o / jnp.sum(p, axis=-1, keepdims=True)).astype(BF16))
    y2 = jnp.dot(jnp.concatenate(outs, axis=1), wo_ref[...], preferred_element_type=F32)
    x2 = x1 + _rms(y2, g_post_ref[...])
    xo_ref[...] = x2
    ho_ref[...] = _rms(x2, g_next_ref[...]).astype(ho_ref.dtype)


def cross_block(y_mix, x, g_mix_post, g_pre, wq, mk, mv, wo, l, g_post, g_next, batch, t_len, mem_len, heads, tm):
    m, d = x.shape
    w = wq.shape[2]
    nq = t_len // tm
    row = pl.BlockSpec((tm, d), lambda b, i: (b * nq + i, 0))
    vec = pl.BlockSpec((1, d), lambda b, i: (0, 0))
    kv = pl.BlockSpec((mem_len, w), lambda b, i: (b, 0))
    once = pl.Buffered(1)
    return pl.pallas_call(
        functools.partial(_cross_block_kernel, heads=heads, hd=w // heads),
        grid=(batch, nq),
        in_specs=[row, row, vec, vec,
                  pl.BlockSpec((None, d, w), lambda b, i: (l, 0, 0), pipeline_mode=once), kv, kv,
                  pl.BlockSpec((None, w, d), lambda b, i: (l, 0, 0), pipeline_mode=once), vec, vec],
        out_specs=[row, row],
        out_shape=[jax.ShapeDtypeStruct((m, d), F32), jax.ShapeDtypeStruct((m, d), BF16)],
        compiler_params=_cparams("parallel", "parallel"), name="cross_block",
    )(y_mix, x, g_mix_post.reshape(1, d), g_pre.reshape(1, d), wq, mk, mv, wo,
      g_post.reshape(1, d), g_next.reshape(1, d))


def _rope_tables(pos):
    half = DIFF_D // 2
    inv = ROPE_THETA ** (-jnp.arange(half, dtype=F32) / half)
    ang = pos.astype(F32)[:, None] * inv[None, :]
    cos, sin, zero = jnp.cos(ang), jnp.sin(ang), jnp.zeros_like(ang)
    reps = LANES // DIFF_D
    return (jnp.tile(cos, (1, 2 * reps)), jnp.tile(jnp.concatenate([-sin, zero], 1), (1, reps)),
            jnp.tile(jnp.concatenate([zero, sin], 1), (1, reps)))


def _round_up(a, b):
    return -(-a // b) * b


def _kv_up_weight(w_uk, w_uv):
    depth, rank, heads, nope = w_uk.shape
    k_rows = jnp.concatenate([w_uk, jnp.zeros_like(w_uk)], axis=3).reshape(depth, rank, heads * MLA_KEY_W)
    slot = jnp.concatenate([jnp.zeros((MLA_ROPE, nope), F32), jnp.eye(MLA_ROPE, dtype=F32),
                            jnp.zeros((MLA_ROPE, MLA_KEY_W - nope - MLA_ROPE), F32)], axis=1)
    rope_rows = jnp.broadcast_to(jnp.tile(slot, (1, heads)), (depth, MLA_ROPE, heads * MLA_KEY_W))
    pad = LANES - MLA_ROPE
    w_k = jnp.concatenate([k_rows, rope_rows, jnp.zeros((depth, pad, heads * MLA_KEY_W), F32)], axis=1)
    w_v = jnp.concatenate([w_uv.reshape(depth, rank, -1), jnp.zeros((depth, LANES, heads * w_uv.shape[3]), F32)],
                          axis=1)
    return jnp.concatenate([w_k, w_v], axis=2).astype(BF16)


def _pad_rows(a, batch, rows):
    t_len = a.shape[0] // batch
    a = jnp.pad(a.reshape(batch, t_len, a.shape[1]), ((0, 0), (0, rows - t_len), (0, 0)))
    return a.reshape(batch * rows, a.shape[2])


def _layer(wts, l, x, h, batch, t_len, pos0, tabs, mem_k, mem_v, past, g_next, tiles, state):
    tm, tt, tq_diff, tkv_diff, tq_mla, tkv_mla, tq_x = tiles
    lam_init = 0.8 - 0.6 * math.exp(-0.3 * l)
    pool_w, diff_w, q_rank, kv_rank = wts["dims"]
    heads_d = diff_w // (2 * DIFF_D)
    heads_m = wts["w_uk_t"].shape[1]

    z = matmul(h, wts["w_in"], l, F32, tm, wts["tn_in"])
    (qd, kd16, vd16, cqn, ckv16, kpe16, kc16), state = split_projection(
        z, tabs, wts["g_mla_q"][l], wts["g_mla_kv"][l], wts["dims"], t_len, tt, l, wts["depth"], state)

    keep = POOL_HALO - 1
    if past is None:
        prev = jnp.zeros((batch, POOL_HALO, pool_w), F32)
    else:
        prev = jnp.concatenate([jnp.zeros((batch, 1, pool_w), F32), past["pool"]], axis=1)
    z3 = z.reshape(batch, t_len, z.shape[1])
    if t_len >= keep:
        new_pool = z3[:, t_len - keep:, :pool_w]
    else:
        new_pool = jnp.concatenate([prev[:, 1 + t_len:], z3[:, :, :pool_w]], axis=1)
    y_pool = pool_mix(z, prev, wts["w_pool"], l, wts["pool_scale"][l], batch, t_len, pos0, tt)

    lq = wts["diff_lambda"][l]
    lam = jnp.exp(jnp.sum(lq[0] * lq[1])) - jnp.exp(jnp.sum(lq[2] * lq[3])) + lam_init
    q = matmul(cqn, wts["w_uq"], l, F32, tm, 1024)
    if past is None:
        o_diff = diff_attention(qd, kd16, vd16, lam, wts["g_diff_sub"][l], batch, heads_d, t_len,
                                1.0 - lam_init, tq_diff, tkv_diff)
        kv_up = matmul(kc16, wts["w_kv_up"], l, BF16, tm, heads_m * MLA_NOPE)
        o_mla = mla_attention(q, tabs, kv_up, batch, heads_m, t_len, tq_mla, tkv_mla)
    else:
        tn = _round_up(t_len, LANES)
        o_diff = diff_attention_past(qd, past["diff_k"], past["diff_v"], _pad_rows(kd16, batch, tn),
                                     _pad_rows(vd16, batch, tn), l, lam, wts["g_diff_sub"][l], batch, heads_d,
                                     t_len, 1.0 - lam_init)
        o_mla = mla_attention_past(q, tabs, wts["w_uk_t"], wts["w_uv"], l, past["ckv"], past["kpe"],
                                   _pad_rows(ckv16, batch, tn), _pad_rows(kpe16, batch, tn), batch, t_len)

    y = matmul_cat([y_pool, o_diff, o_mla], wts["w_out"], l, F32, tm, 1024)
    x, h = cross_block(y, x, wts["g_mix_post"][l], wts["g_x_pre"][l], wts["w_mem_q"], mem_k, mem_v,
                       wts["w_mem_o"], l, wts["g_x_post"][l], wts["g_ff_pre"][l], batch, t_len,
                       mem_k.shape[0] // batch, wts["mem_heads"], tq_x)

    a = matmul_swiglu(h, wts["w_gate"], wts["w_up"], l, 2 * tm, 256)
    y = matmul(a, wts["w_down"], l, F32, 512, 512)
    x, h = post_res(y, x, wts["g_ff_post"][l], g_next, tt)
    return x, h, state, new_pool


def kernel(x_prompt, x_sample, cache_diff_k, cache_diff_v, cache_mla_ckv, cache_mla_kpe, cache_pool, cache_mem_k, cache_mem_v, mem_prompt, g_mix_pre, w_in, w_pool, pool_scale, diff_lambda, g_diff_sub, g_mla_q, w_mla_uq, w_mla_uk, w_mla_uv, g_mla_kv, w_out, g_mix_post, g_mem, w_mem_k, w_mem_v, w_mem_q, w_mem_o, g_x_pre, g_x_post, g_ff_pre, w_gate, w_up, w_down, g_ff_post):
    depth = w_in.shape[0]
    bp, tp, d = x_prompt.shape
    bs, ts, _ = x_sample.shape
    past_len = cache_mla_ckv.shape[2]
    pool_w = cache_pool.shape[3]
    heads_d, diff_w = cache_diff_k.shape[3], cache_diff_k.shape[3] * cache_diff_k.shape[4]
    q_rank, kv_rank = g_mla_q.shape[1], g_mla_kv.shape[1]
    mla_heads = w_mla_uk.shape[2]
    mem_len, mem_heads, mem_hd = cache_mem_k.shape[2:]
    mem_w = mem_heads * mem_hd

    in_w = w_in.shape[2]
    nz = _round_up(in_w, 5 * LANES)
    uq = w_mla_uq.reshape(depth, q_rank, mla_heads, MLA_NOPE + MLA_ROPE)
    wts = {
        "depth": depth, "dims": (pool_w, diff_w, q_rank, kv_rank), "tn_in": nz // 5, "mem_heads": mem_heads,
        "w_in": jnp.pad(w_in, ((0, 0), (0, 0), (0, nz - in_w))).astype(BF16),
        "w_pool": w_pool.astype(BF16), "pool_scale": pool_scale, "diff_lambda": diff_lambda,
        "g_diff_sub": g_diff_sub, "g_mla_q": g_mla_q, "g_mla_kv": g_mla_kv,
        "w_uq": jnp.concatenate([uq[..., :MLA_NOPE].reshape(depth, q_rank, -1),
                                 uq[..., MLA_NOPE:].reshape(depth, q_rank, -1)], axis=2).astype(BF16),
        "w_uk_t": jnp.transpose(w_mla_uk, (0, 2, 3, 1)).astype(BF16),
        "w_uv": jnp.transpose(w_mla_uv, (0, 2, 1, 3)).astype(BF16),
        "w_kv_up": _kv_up_weight(w_mla_uk, w_mla_uv),
        "w_out": w_out.astype(BF16), "g_mix_post": g_mix_post,
        "w_mem_q": w_mem_q.astype(BF16), "w_mem_o": w_mem_o.astype(BF16),
        "g_x_pre": g_x_pre, "g_x_post": g_x_post, "g_ff_pre": g_ff_pre,
        "w_gate": w_gate, "w_up": w_up, "w_down": w_down.astype(BF16),
        "g_ff_post": g_ff_post,
    }
    w_mem_kv = jnp.concatenate([w_mem_k, w_mem_v], axis=2).astype(BF16)

    tabs_p = _rope_tables(jnp.arange(tp))
    tabs_s = _rope_tables(past_len + jnp.arange(ts))
    tiles_p = (1024, 256, 512, 512, 512, 512, 256)
    tiles_s = (bs * ts, ts, ts, 0, ts, 0, ts)

    xp = x_prompt.reshape(bp * tp, d)
    xs = x_sample.reshape(bs * ts, d)
    hp = rmsnorm_bf16(xp, g_mix_pre[0], tiles_p[1])
    hs = rmsnorm_bf16(xs, g_mix_pre[0], tiles_s[1])
    mem_n = mem_prompt.reshape(bp * mem_len, d)
    mem_k_s = cache_mem_k.reshape(depth, bs * mem_len, mem_w).astype(BF16)
    mem_v_s = cache_mem_v.reshape(depth, bs * mem_len, mem_w).astype(BF16)
    past = {"diff_k": jnp.transpose(cache_diff_k, (0, 1, 3, 2, 4)).astype(BF16),
            "diff_v": jnp.transpose(cache_diff_v, (0, 1, 3, 2, 4)).astype(BF16),
            "ckv": cache_mla_ckv, "kpe": cache_mla_kpe}

    state_p = state_s = None
    pools_p, pools_s, mem_ks, mem_vs = [], [], [], []
    for l in range(depth):
        g_next = g_mix_pre[l + 1] if l + 1 < depth else None
        m = rmsnorm_bf16(mem_n, g_mem[l], 256)
        mkv = matmul(m, w_mem_kv, l, F32, 1024, 1024)
        mk, mv = mkv[:, :mem_w], mkv[:, mem_w:]
        mem_ks.append(mk.reshape(bp, mem_len, mem_heads, mem_hd))
        mem_vs.append(mv.reshape(bp, mem_len, mem_heads, mem_hd))
        xp, hp, state_p, pool_p = _layer(wts, l, xp, hp, bp, tp, 0, tabs_p, mk.astype(BF16), mv.astype(BF16),
                                         None, g_next, tiles_p, state_p)
        pools_p.append(pool_p)
        xs, hs, state_s, pool_s = _layer(wts, l, xs, hs, bs, ts, past_len, tabs_s, mem_k_s[l], mem_v_s[l],
                                         dict(past, pool=cache_pool[l]), g_next, tiles_s, state_s)
        pools_s.append(pool_s)

    hd = 2 * DIFF_D
    shape_p = lambda a, tail: a.reshape((depth, bp, tp) + tail)
    shape_s = lambda a, tail: a.reshape((depth, bs, ts) + tail)
    return (xp.reshape(bp, tp, d), xs.reshape(bs, ts, d),
            shape_p(state_p[0], (heads_d, hd)), shape_p(state_p[1], (heads_d, hd)),
            shape_p(state_p[2], (kv_rank,)), shape_p(state_p[3], (MLA_ROPE,)),
            jnp.stack(pools_p), jnp.stack(mem_ks), jnp.stack(mem_vs),
            shape_s(state_s[0], (heads_d, hd)), shape_s(state_s[1], (heads_d, hd)),
            shape_s(state_s[2], (kv_rank,)), shape_s(state_s[3], (MLA_ROPE,)),
            jnp.stack(pools_s))
```

```python
import functools
import math

import jax
import jax.numpy as jnp
from jax import lax
from jax.experimental import pallas as pl
from jax.experimental.pallas import tpu as pltpu

F32 = jnp.float32
BF16 = jnp.bfloat16

EPS = 1e-6
CHUNK = 64
ROPE_THETA = 10000.0
POOL_WINDOWS = (2, 4, 8, 16)
POOL_HALO = 16
DIFF_D = 64
MLA_NOPE = 128
MLA_ROPE = 64
LANES = 128
MIB = 1024 * 1024
VMEM_LIMIT = 56 * MIB

_NT = (((1,), (1,)), ((), ()))


def _cparams(*sem):
    return pltpu.CompilerParams(dimension_semantics=sem, vmem_limit_bytes=VMEM_LIMIT)


def _rms(x, g):
    return x * lax.rsqrt(jnp.mean(x * x, axis=-1, keepdims=True) + EPS) * g


def _rmsnorm_kernel(x_ref, g_ref, o_ref):
    o_ref[...] = _rms(x_ref[...], g_ref[...]).astype(o_ref.dtype)


def rmsnorm_bf16(x, g, tm):
    m, d = x.shape
    return pl.pallas_call(
        _rmsnorm_kernel,
        grid=(m // tm,),
        in_specs=[pl.BlockSpec((tm, d), lambda i: (i, 0)), pl.BlockSpec((1, d), lambda i: (0, 0))],
        out_specs=pl.BlockSpec((tm, d), lambda i: (i, 0)),
        out_shape=jax.ShapeDtypeStruct((m, d), BF16),
        compiler_params=_cparams("parallel"),
        name="rmsnorm",
    )(x, g.reshape(1, d))


def _post_res_kernel(y_ref, x_ref, gp_ref, gn_ref, xo_ref, ho_ref):
    x = x_ref[...] + _rms(y_ref[...], gp_ref[...])
    xo_ref[...] = x
    ho_ref[...] = _rms(x, gn_ref[...]).astype(ho_ref.dtype)


def _post_res_last_kernel(y_ref, x_ref, gp_ref, xo_ref):
    xo_ref[...] = x_ref[...] + _rms(y_ref[...], gp_ref[...])


def post_res(y, x, g_post, g_next, tm):
    m, d = x.shape
    row = pl.BlockSpec((tm, d), lambda i: (i, 0))
    vec = pl.BlockSpec((1, d), lambda i: (0, 0))
    if g_next is None:
        return pl.pallas_call(
            _post_res_last_kernel, grid=(m // tm,), in_specs=[row, row, vec], out_specs=row,
            out_shape=jax.ShapeDtypeStruct((m, d), F32), compiler_params=_cparams("parallel"),
            name="post_res_last",
        )(y, x, g_post.reshape(1, d)), None
    return pl.pallas_call(
        _post_res_kernel, grid=(m // tm,), in_specs=[row, row, vec, vec], out_specs=[row, row],
        out_shape=[jax.ShapeDtypeStruct((m, d), F32), jax.ShapeDtypeStruct((m, d), BF16)],
        compiler_params=_cparams("parallel"), name="post_res",
    )(y, x, g_post.reshape(1, d), g_next.reshape(1, d))


def _mm_kernel(x_ref, w_ref, o_ref):
    o_ref[...] = jnp.dot(x_ref[...], w_ref[...], preferred_element_type=F32).astype(o_ref.dtype)


def matmul(x, w, l, out_dtype, tm, tn):
    m, k = x.shape
    n = w.shape[2]
    tm, tn = min(tm, m), min(tn, n)
    return pl.pallas_call(
        _mm_kernel, grid=(n // tn, m // tm),
        in_specs=[pl.BlockSpec((tm, k), lambda j, i: (i, 0)), pl.BlockSpec((None, k, tn), lambda j, i: (l, 0, j))],
        out_specs=pl.BlockSpec((tm, tn), lambda j, i: (i, j)),
        out_shape=jax.ShapeDtypeStruct((m, n), out_dtype),
        compiler_params=_cparams("parallel", "parallel"), name="matmul",
    )(x, w)


def _mm_cat_kernel(*refs):
    n_in = (len(refs) - 1) // 2
    acc = jnp.dot(refs[0][...], refs[n_in][...], preferred_element_type=F32)
    for a in range(1, n_in):
        acc += jnp.dot(refs[a][...], refs[n_in + a][...], preferred_element_type=F32)
    refs[-1][...] = acc.astype(refs[-1].dtype)


def matmul_cat(xs, w, l, out_dtype, tm, tn):
    m = xs[0].shape[0]
    n = w.shape[2]
    tm, tn = min(tm, m), min(tn, n)
    x_specs, w_specs, off = [], [], 0
    for x in xs:
        k = x.shape[1]
        assert off % k == 0, "each input's row band of w must start on a multiple of its own width"
        x_specs.append(pl.BlockSpec((tm, k), lambda j, i: (i, 0)))
        w_specs.append(pl.BlockSpec((None, k, tn), functools.partial(lambda j, i, r: (l, r, j), r=off // k)))
        off += k
    return pl.pallas_call(
        _mm_cat_kernel, grid=(n // tn, m // tm),
        in_specs=x_specs + w_specs,
        out_specs=pl.BlockSpec((tm, tn), lambda j, i: (i, j)),
        out_shape=jax.ShapeDtypeStruct((m, n), out_dtype),
        compiler_params=_cparams("parallel", "parallel"), name="matmul_cat",
    )(*xs, *([w] * len(xs)))


def _swiglu_kernel(x_ref, wg_ref, wu_ref, o_ref):
    x = x_ref[...]
    g = jnp.dot(x, wg_ref[...].astype(BF16), preferred_element_type=F32)
    u = jnp.dot(x, wu_ref[...].astype(BF16), preferred_element_type=F32)
    o_ref[...] = (g * jax.nn.sigmoid(g) * u).astype(o_ref.dtype)


def matmul_swiglu(x, wg, wu, l, tm, tn):
    m, k = x.shape
    n = wg.shape[2]
    tm = min(tm, m)
    wspec = pl.BlockSpec((None, k, tn), lambda i, j: (l, 0, j))
    return pl.pallas_call(
        _swiglu_kernel, grid=(m // tm, n // tn),
        in_specs=[pl.BlockSpec((tm, k), lambda i, j: (i, 0), pipeline_mode=pl.Buffered(1)), wspec, wspec],
        out_specs=pl.BlockSpec((tm, tn), lambda i, j: (i, j)),
        out_shape=jax.ShapeDtypeStruct((m, n), BF16),
        compiler_params=_cparams("parallel", "parallel"), name="matmul_swiglu",
    )(x, wg, wu)


def _q_proj_kernel(x_ref, w_ref, cos_ref, sa_ref, sb_ref, o_ref, *, scale):
    acc = jnp.dot(x_ref[...], w_ref[...], preferred_element_type=F32)
    cos, sa, sb = cos_ref[...], sa_ref[...], sb_ref[...]
    for s in range(acc.shape[1] // MLA_KEY_W):
        lo, mid, hi = s * MLA_KEY_W, s * MLA_KEY_W + MLA_NOPE, (s + 1) * MLA_KEY_W
        o_ref[:, lo:mid] = (acc[:, lo:mid] * scale).astype(o_ref.dtype)
        o_ref[:, mid:hi] = (_rope_chunk(acc[:, mid:hi], cos, sa, sb) * scale).astype(o_ref.dtype)


def mla_query_projection(cqn, w_slots, tabs, l, tm, tn):
    m, k = cqn.shape
    n = w_slots.shape[2]
    t_tab = tabs[0].shape[0]
    tm, tn = min(tm, t_tab), min(tn, n)
    nt = t_tab // tm
    tab = pl.BlockSpec((tm, LANES), lambda j, i: (i % nt, 0))
    return pl.pallas_call(
        functools.partial(_q_proj_kernel, scale=(MLA_NOPE + MLA_ROPE) ** -0.5),
        grid=(n // tn, m // tm),
        in_specs=[pl.BlockSpec((tm, k), lambda j, i: (i, 0)), pl.BlockSpec((None, k, tn), lambda j, i: (l, 0, j)),
                  tab, tab, tab],
        out_specs=pl.BlockSpec((tm, tn), lambda j, i: (i, j)),
        out_shape=jax.ShapeDtypeStruct((m, n), BF16),
        compiler_params=_cparams("parallel", "parallel"), name="mla_query_projection",
    )(cqn, w_slots, *tabs)


def _rope_chunk(x, cos, sa, sb):
    return x * cos + pltpu.roll(x, LANES - 32, 1) * sa + pltpu.roll(x, 32, 1) * sb


def _rope_cols(x, cos, sa, sb):
    return jnp.concatenate([_rope_chunk(x[:, c * LANES:(c + 1) * LANES], cos, sa, sb)
                            for c in range(x.shape[1] // LANES)], axis=1)


_IN_GROUPS = 5
_IN_INPUTS = 6


def _in_proj_kernel(*refs, q_scale):
    x_ref, w_ref, cos_ref, sa_ref, sb_ref, gq_ref = refs[:_IN_INPUTS]
    u_ref, qd_ref, kd32_ref, kd16_ref, vd32_ref, vd16_ref, cq_ref = refs[-7:]
    j = pl.program_id(1)
    acc = jnp.dot(x_ref[...], w_ref[...], preferred_element_type=F32)
    rope = lambda a: _rope_cols(a, cos_ref[...], sa_ref[...], sb_ref[...])

    @pl.when(j == 0)
    def _():
        u_ref[...] = acc

    @pl.when(j == 1)
    def _():
        qd_ref[...] = (rope(acc) * q_scale).astype(BF16)

    @pl.when(j == 2)
    def _():
        k = rope(acc)
        kd32_ref[...] = k
        kd16_ref[...] = k.astype(BF16)

    @pl.when(j == 3)
    def _():
        vd32_ref[...] = acc
        vd16_ref[...] = acc.astype(BF16)

    @pl.when(j == 4)
    def _():
        cq_ref[...] = _rms(acc, gq_ref[...]).astype(BF16)


_LAT_INPUTS = 6


def _in_proj_latent_kernel(*refs):
    x_ref, w_ref, cos_ref, sa_ref, sb_ref, gkv_ref = refs[:_LAT_INPUTS]
    ckv32_ref, ckv16_ref, kpe32_ref, kpe16_ref, kc16_ref = refs[-5:]
    rank = ckv32_ref.shape[1]
    acc = jnp.dot(x_ref[...], w_ref[...], preferred_element_type=F32)
    ckv = _rms(acc[:, 0:rank], gkv_ref[...])
    ckv32_ref[...] = ckv
    ckv16_ref[...] = ckv.astype(BF16)
    kpe_chunk = _rope_chunk(acc[:, rank:rank + LANES], cos_ref[...], sa_ref[...], sb_ref[...])
    kpe = kpe_chunk[:, :MLA_ROPE]
    kpe32_ref[...] = kpe
    kpe16_ref[...] = kpe.astype(BF16)
    kc16_ref[:, 0:rank] = ckv.astype(BF16)
    kc16_ref[:, rank:rank + LANES] = kpe_chunk.astype(BF16)


def input_projection(h, w_main, w_lat, tabs, g_q, g_kv, dims, tm, tm_lat, l, depth, state):
    m, d = h.shape
    pool_w, diff_w, q_rank, kv_rank = dims
    gw = diff_w
    assert pool_w == gw and q_rank == gw and w_main.shape[2] == _IN_GROUPS * gw
    t_tab = tabs[0].shape[0]
    tm, tm_lat = min(tm, t_tab), min(tm_lat, t_tab)
    state = (None, None) if state is None else ((state[0], state[1]), (state[2], state[3]))
    sds = lambda w, dt: jax.ShapeDtypeStruct((m, w), dt)
    stk = lambda w: jax.ShapeDtypeStruct((depth, m, w), F32)
    alias = pl.BlockSpec(memory_space=pl.ANY)

    nt = t_tab // tm
    row = lambda w: pl.BlockSpec((tm, w), lambda i, j: (i, 0))
    lay = lambda w: pl.BlockSpec((None, tm, w), lambda i, j: (l, i, 0))
    tab = pl.BlockSpec((tm, LANES), lambda i, j: (i % nt, 0))
    prev = () if state[0] is None else state[0]
    u, qd, kd32, kd16, vd32, vd16, cqn = pl.pallas_call(
        functools.partial(_in_proj_kernel, q_scale=DIFF_D ** -0.5),
        grid=(m // tm, _IN_GROUPS),
        in_specs=[pl.BlockSpec((tm, d), lambda i, j: (i, 0)),
                  pl.BlockSpec((None, d, gw), lambda i, j: (l, 0, j)),
                  tab, tab, tab, pl.BlockSpec((1, gw), lambda i, j: (0, 0))] + [alias] * len(prev),
        out_specs=[row(gw), row(gw), lay(gw), row(gw), lay(gw), row(gw), row(gw)],
        out_shape=[sds(gw, F32), sds(gw, BF16), stk(gw), sds(gw, BF16), stk(gw), sds(gw, BF16), sds(gw, BF16)],
        input_output_aliases={_IN_INPUTS + a: o for a, o in enumerate((2, 4)[:len(prev)])},
        compiler_params=_cparams("parallel", "arbitrary"), name="in_proj",
    )(h, w_main, *tabs, g_q.reshape(1, gw), *prev)

    nt = t_tab // tm_lat
    lw = w_lat.shape[2]
    row = lambda w: pl.BlockSpec((tm_lat, w), lambda i: (i, 0))
    lay = lambda w: pl.BlockSpec((None, tm_lat, w), lambda i: (l, i, 0))
    tab = pl.BlockSpec((tm_lat, LANES), lambda i: (i % nt, 0))
    prev = () if state[1] is None else state[1]
    ckv32, ckv16, kpe32, kpe16, kc16 = pl.pallas_call(
        _in_proj_latent_kernel,
        grid=(m // tm_lat,),
        in_specs=[pl.BlockSpec((tm_lat, d), lambda i: (i, 0)),
                  pl.BlockSpec((None, d, lw), lambda i: (l, 0, 0)),
                  tab, tab, tab, pl.BlockSpec((1, kv_rank), lambda i: (0, 0))] + [alias] * len(prev),
        out_specs=[lay(kv_rank), row(kv_rank), lay(MLA_ROPE), row(MLA_ROPE), row(lw)],
        out_shape=[stk(kv_rank), sds(kv_rank, BF16), stk(MLA_ROPE), sds(MLA_ROPE, BF16), sds(lw, BF16)],
        input_output_aliases={_LAT_INPUTS + a: o for a, o in enumerate((0, 2)[:len(prev)])},
        compiler_params=_cparams("parallel"), name="in_proj_latent",
    )(h, w_lat, *tabs, g_kv.reshape(1, kv_rank), *prev)
    return (u, qd, kd16, vd16, cqn, ckv16, kpe16, kc16), (kd32, vd32, ckv32, kpe32)


def _pool_kernel(z_ref, halo_ref, prev_ref, wp_ref, sc_ref, o_ref, ext_ref, *, tt, pos0):
    i = pl.program_id(1)
    ext_ref[0:POOL_HALO, :] = jnp.where(i == 0, prev_ref[0], halo_ref[...])
    ext_ref[POOL_HALO:POOL_HALO + tt, :] = z_ref[...]
    pos = pos0 + i * tt + lax.broadcasted_iota(jnp.int32, (tt, 1), 0)
    group = wp_ref.shape[1]
    for g, w in enumerate(POOL_WINDOWS):
        cs = slice(g * group, (g + 1) * group)
        s = ext_ref[POOL_HALO:POOL_HALO + tt, cs]
        u = s
        for j in range(1, w):
            s = s + ext_ref[POOL_HALO - j:POOL_HALO - j + tt, cs]
        cnt = jnp.minimum(w, pos + 1).astype(F32)
        d = s / cnt - u
        y = jnp.dot(d.astype(BF16), wp_ref[g], preferred_element_type=F32) * sc_ref[:, cs]
        o_ref[:, cs] = y.astype(o_ref.dtype)


def pool_mix(z, prev, w_pool, l, scale, batch, t_len, pos0, tt):
    m = z.shape[0]
    c = prev.shape[2]
    nt = t_len // tt
    hb = tt // POOL_HALO
    return pl.pallas_call(
        functools.partial(_pool_kernel, tt=tt, pos0=pos0),
        grid=(batch, nt),
        in_specs=[pl.BlockSpec((tt, c), lambda b, i: (b * nt + i, 0)),
                  pl.BlockSpec((POOL_HALO, c), lambda b, i: (jnp.maximum((b * nt + i) * hb - 1, 0), 0)),
                  pl.BlockSpec((1, POOL_HALO, c), lambda b, i: (b, 0, 0)),
                  pl.BlockSpec((None,) + w_pool.shape[1:], lambda b, i: (l, 0, 0, 0)),
                  pl.BlockSpec((1, c), lambda b, i: (0, 0))],
        out_specs=pl.BlockSpec((tt, c), lambda b, i: (b * nt + i, 0)),
        out_shape=jax.ShapeDtypeStruct((m, c), BF16),
        scratch_shapes=[pltpu.VMEM((POOL_HALO + tt, c), F32)],
        compiler_params=_cparams("parallel", "parallel"), name="pool_mix",
    )(z, z, prev, w_pool, scale.reshape(1, c))


def _last_kv_block(i, tq, tkv, q0, s_len):
    last_q = q0 + (i + 1) * tq - 1
    visible = jnp.minimum((last_q // CHUNK + 1) * CHUNK, s_len)
    return (visible - 1) // tkv


def _mask_bias(q_start, tq, k_start, tkv, s_len):
    qpos = q_start + lax.broadcasted_iota(jnp.int32, (tq, 1), 0)
    kpos = k_start + lax.broadcasted_iota(jnp.int32, (1, tkv), 1)
    shift = CHUNK.bit_length() - 1
    ok = (jnp.right_shift(kpos, shift) <= jnp.right_shift(qpos, shift)) & (kpos < s_len)
    return jnp.where(ok, 0.0, -jnp.inf).astype(F32)


def _add_bias(s, bias, groups):
    rows, tkv = s.shape
    return (s.reshape(groups, rows // groups, tkv) + bias[None]).reshape(rows, tkv)


def _lane_tile(x, width):
    return x if width == LANES else jnp.concatenate([x] * (width // LANES), axis=1)


def _flash_update(s, v, m_ref, l_ref, acc_ref):
    tkv = s.shape[1]
    m_prev = m_ref[...]
    m_new = jnp.maximum(m_prev, jnp.max(s, axis=-1, keepdims=True))
    alpha = jnp.exp(m_prev - m_new)
    p = jnp.exp(s - _lane_tile(m_new, tkv))
    psum = p[:, :LANES]
    for c in range(1, tkv // LANES):
        psum = psum + p[:, c * LANES:(c + 1) * LANES]
    l_ref[...] = alpha * l_ref[...] + psum
    acc_ref[...] = (_lane_tile(alpha, acc_ref.shape[1]) * acc_ref[...]
                    + jnp.dot(p.astype(BF16), v, preferred_element_type=F32))
    m_ref[...] = m_new


def _flash_init(m_ref, l_ref, acc_ref):
    m_ref[...] = jnp.full(m_ref.shape, -jnp.inf, F32)
    l_ref[...] = jnp.zeros(l_ref.shape, F32)
    acc_ref[...] = jnp.zeros(acc_ref.shape, F32)


def _flash_result(l_ref, acc_ref):
    return acc_ref[...] / jnp.sum(l_ref[...], axis=-1, keepdims=True)


def _flash_scratch(groups, rows, dv):
    lead = () if groups is None else (groups,)
    return [pltpu.VMEM(lead + (rows, LANES), F32), pltpu.VMEM(lead + (rows, LANES), F32),
            pltpu.VMEM(lead + (rows, dv), F32)]


def _full_kv_blocks(i, tq, tkv, q0, s_len):
    first_q = q0 + i * tq
    return jnp.minimum((first_q // CHUNK + 1) * CHUNK, s_len) // tkv


def _with_ones(v):
    return jnp.concatenate([v, jnp.ones((v.shape[0], LANES), v.dtype)], axis=1)


def _flash_update_wide(s, v_ones, m_ref, acc_ref):
    tkv = s.shape[1]
    m_prev = m_ref[...]
    m_new = jnp.maximum(m_prev, jnp.max(s, axis=-1, keepdims=True))
    alpha = jnp.exp(m_prev - m_new)
    p = jnp.exp((s - _lane_tile(m_new, tkv)).astype(BF16))
    acc_ref[...] = (_lane_tile(alpha, acc_ref.shape[1]) * acc_ref[...]
                    + jnp.dot(p, v_ones, preferred_element_type=F32))
    m_ref[...] = m_new


def _flash_init_wide(m_ref, acc_ref):
    m_ref[...] = jnp.full(m_ref.shape, -jnp.inf, F32)
    acc_ref[...] = jnp.zeros(acc_ref.shape, F32)


def _flash_result_wide(acc_ref):
    dv = acc_ref.shape[-1] - LANES
    assert dv == LANES
    return acc_ref[:, 0:dv] / acc_ref[:, dv:dv + LANES]


def _flash_scratch_wide(groups, rows, dv):
    return [pltpu.VMEM((groups, rows, LANES), F32), pltpu.VMEM((groups, rows, dv + LANES), F32)]


def _past_tile(p_len, cap):
    tile = max(t for t in range(LANES, cap + 1, LANES) if p_len % t == 0)
    return tile


def _diff_split_q(q_ref, qs_ref, heads, tq):
    hd = 2 * DIFF_D
    lane = lax.broadcasted_iota(jnp.int32, (tq, hd), 1)
    for h in range(heads):
        q = q_ref[:, h * hd:(h + 1) * hd]
        qs_ref[h, 0:tq, :] = jnp.where(lane < DIFF_D, q, jnp.zeros_like(q))
        qs_ref[h, tq:2 * tq, :] = jnp.where(lane >= DIFF_D, q, jnp.zeros_like(q))


def _heads_sweep(q_of, k_of, v_of, bias, groups, m_ref, acc_ref, heads):
    for h in range(heads):
        s = lax.dot_general(q_of(h), k_of(h), _NT, preferred_element_type=F32)
        if bias is not None:
            s = s + bias if groups == 1 else _add_bias(s, bias, groups)
        _flash_update_wide(s, _with_ones(v_of(h)), m_ref.at[h], acc_ref.at[h])


def _diff_finish(lam_ref, g_ref, o_ref, acc_ref, heads, tq, out_scale):
    hd = 2 * DIFF_D
    lam = lam_ref[0, 0]
    for h in range(heads):
        a = _flash_result_wide(acc_ref.at[h])
        o = a[0:tq] - lam * a[tq:2 * tq]
        o_ref[:, h * hd:(h + 1) * hd] = (_rms(o, g_ref[...]) * out_scale).astype(o_ref.dtype)


def _diff_kernel(lam_ref, q_ref, k_ref, v_ref, g_ref, o_ref, qs_ref, m_ref, acc_ref, *,
                 heads, tq, tkv, s_len, out_scale):
    i, j = pl.program_id(1), pl.program_id(2)
    hd = 2 * DIFF_D
    k_of = lambda h: k_ref[:, h * hd:(h + 1) * hd]
    v_of = lambda h: v_ref[:, h * hd:(h + 1) * hd]
    full = _full_kv_blocks(i, tq, tkv, 0, s_len)

    @pl.when(j == 0)
    def _():
        _flash_init_wide(m_ref, acc_ref)
        _diff_split_q(q_ref, qs_ref, heads, tq)

    @pl.when(j < full)
    def _():
        _heads_sweep(lambda h: qs_ref[h], k_of, v_of, None, 2, m_ref, acc_ref, heads)

    @pl.when((j >= full) & (j <= _last_kv_block(i, tq, tkv, 0, s_len)))
    def _():
        _heads_sweep(lambda h: qs_ref[h], k_of, v_of, _mask_bias(i * tq, tq, j * tkv, tkv, s_len), 2,
                     m_ref, acc_ref, heads)

    @pl.when(j == pl.num_programs(2) - 1)
    def _():
        _diff_finish(lam_ref, g_ref, o_ref, acc_ref, heads, tq, out_scale)


def diff_attention(q, k, v, lam, g_sub, batch, heads, t_len, out_scale, tq, tkv):
    hd = 2 * DIFF_D
    w = heads * hd
    nq, nkv = t_len // tq, t_len // tkv
    last = functools.partial(_last_kv_block, tq=tq, tkv=tkv, q0=0, s_len=t_len)
    kvmap = lambda b, i, j: (b * nkv + jnp.minimum(j, last(i)), 0)
    qmap = lambda b, i, j: (b * nq + i, 0)
    return pl.pallas_call(
        functools.partial(_diff_kernel, heads=heads, tq=tq, tkv=tkv, s_len=t_len, out_scale=out_scale),
        grid=(batch, nq, nkv),
        in_specs=[pl.BlockSpec(memory_space=pltpu.SMEM),
                  pl.BlockSpec((tq, w), qmap),
                  pl.BlockSpec((tkv, w), kvmap),
                  pl.BlockSpec((tkv, w), kvmap),
                  pl.BlockSpec((1, hd), lambda b, i, j: (0, 0))],
        out_specs=pl.BlockSpec((tq, w), qmap),
        out_shape=jax.ShapeDtypeStruct((batch * t_len, w), BF16),
        scratch_shapes=[pltpu.VMEM((heads, 2 * tq, hd), BF16)] + _flash_scratch_wide(heads, 2 * tq, hd),
        compiler_params=_cparams("parallel", "parallel", "arbitrary"), name="diff_attention",
    )(lam.reshape(1, 1), q, k, v, g_sub.reshape(1, hd))


def _diff_past_kernel(lam_ref, q_ref, kc_ref, vc_ref, kn_ref, vn_ref, g_ref, o_ref,
                      qs_ref, m_ref, acc_ref, *, heads, tq, tkv, p_len, s_len, out_scale):
    j = pl.program_id(1)
    n_past = pl.num_programs(1) - 1
    hd = 2 * DIFF_D

    @pl.when(j == 0)
    def _():
        _flash_init_wide(m_ref, acc_ref)
        _diff_split_q(q_ref, qs_ref, heads, tq)

    @pl.when(j < n_past)
    def _():
        _heads_sweep(lambda h: qs_ref[h], lambda h: kc_ref[h], lambda h: vc_ref[h],
                     _mask_bias(p_len, tq, j * tkv, tkv, s_len), 2, m_ref, acc_ref, heads)

    @pl.when(j == n_past)
    def _():
        _heads_sweep(lambda h: qs_ref[h], lambda h: kn_ref[:, h * hd:(h + 1) * hd],
                     lambda h: vn_ref[:, h * hd:(h + 1) * hd],
                     _mask_bias(p_len, tq, p_len, kn_ref.shape[0], s_len), 2, m_ref, acc_ref, heads)
        _diff_finish(lam_ref, g_ref, o_ref, acc_ref, heads, tq, out_scale)


def diff_attention_past(q, kc, vc, kn, vn, l, lam, g_sub, batch, heads, t_len, out_scale):
    hd = 2 * DIFF_D
    w = heads * hd
    p_len = kc.shape[3]
    tn = kn.shape[0] // batch
    tkv = _past_tile(p_len, 1024)
    n_past = p_len // tkv
    cache = pl.BlockSpec((None, None, heads, tkv, hd), lambda b, j: (l, b, 0, jnp.minimum(j, n_past - 1), 0))
    new = pl.BlockSpec((tn, w), lambda b, j: (b, 0))
    qmap = lambda b, j: (b, 0)
    return pl.pallas_call(
        functools.partial(_diff_past_kernel, heads=heads, tq=t_len, tkv=tkv, p_len=p_len, s_len=p_len + t_len,
                          out_scale=out_scale),
        grid=(batch, n_past + 1),
        in_specs=[pl.BlockSpec(memory_space=pltpu.SMEM), pl.BlockSpec((t_len, w), qmap), cache, cache, new, new,
                  pl.BlockSpec((1, hd), lambda b, j: (0, 0))],
        out_specs=pl.BlockSpec((t_len, w), qmap),
        out_shape=jax.ShapeDtypeStruct((batch * t_len, w), BF16),
        scratch_shapes=[pltpu.VMEM((heads, 2 * t_len, hd), BF16)] + _flash_scratch_wide(heads, 2 * t_len, hd),
        compiler_params=_cparams("parallel", "arbitrary"), name="diff_attention_past",
    )(lam.reshape(1, 1), q, kc, vc, kn, vn, g_sub.reshape(1, hd))


MLA_KEY_W = 2 * MLA_NOPE


def _mla_kernel(q_ref, k_ref, v_ref, o_ref, m_ref, acc_ref, *, heads, tq, tkv, s_len):
    i, j = pl.program_id(1), pl.program_id(2)
    q_of = lambda h: q_ref[:, h * MLA_KEY_W:(h + 1) * MLA_KEY_W]
    k_of = lambda h: k_ref[:, h * MLA_KEY_W:(h + 1) * MLA_KEY_W]
    v_of = lambda h: v_ref[:, h * MLA_NOPE:(h + 1) * MLA_NOPE]
    full = _full_kv_blocks(i, tq, tkv, 0, s_len)

    @pl.when(j == 0)
    def _():
        _flash_init_wide(m_ref, acc_ref)

    @pl.when(j < full)
    def _():
        _heads_sweep(q_of, k_of, v_of, None, 1, m_ref, acc_ref, heads)

    @pl.when((j >= full) & (j <= _last_kv_block(i, tq, tkv, 0, s_len)))
    def _():
        _heads_sweep(q_of, k_of, v_of, _mask_bias(i * tq, tq, j * tkv, tkv, s_len), 1, m_ref, acc_ref, heads)

    @pl.when(j == pl.num_programs(2) - 1)
    def _():
        for h in range(heads):
            o_ref[:, h * MLA_NOPE:(h + 1) * MLA_NOPE] = _flash_result_wide(acc_ref.at[h]).astype(o_ref.dtype)


def mla_attention(q, kv_up, batch, heads, t_len, tq, tkv):
    kw, vw = heads * MLA_KEY_W, heads * MLA_NOPE
    nq, nkv = t_len // tq, t_len // tkv
    last = functools.partial(_last_kv_block, tq=tq, tkv=tkv, q0=0, s_len=t_len)
    kvrow = lambda b, i, j: b * nkv + jnp.minimum(j, last(i))
    return pl.pallas_call(
        functools.partial(_mla_kernel, heads=heads, tq=tq, tkv=tkv, s_len=t_len),
        grid=(batch, nq, nkv),
        in_specs=[pl.BlockSpec((tq, kw), lambda b, i, j: (b * nq + i, 0)),
                  pl.BlockSpec((tkv, kw), lambda b, i, j: (kvrow(b, i, j), 0)),
                  pl.BlockSpec((tkv, vw), lambda b, i, j: (kvrow(b, i, j), kw // vw))],
        out_specs=pl.BlockSpec((tq, vw), lambda b, i, j: (b * nq + i, 0)),
        out_shape=jax.ShapeDtypeStruct((batch * t_len, vw), BF16),
        scratch_shapes=_flash_scratch_wide(heads, tq, MLA_NOPE),
        compiler_params=_cparams("parallel", "parallel", "arbitrary"), name="mla_attention",
    )(q, kv_up, kv_up)


def _mla_past_kernel(q_ref, wuk_ref, ckvc_ref, kpec_ref, ckvn_ref, kpen_ref, wuv_ref,
                     o_ref, qlat_ref, qpe_ref, m_ref, l_ref, acc_ref, *, heads, tq, tkv, p_len, s_len):
    j = pl.program_id(1)
    n_past = pl.num_programs(1) - 1

    @pl.when(j == 0)
    def _():
        _flash_init(m_ref, l_ref, acc_ref)
        for h in range(heads):
            lo = h * MLA_KEY_W
            qlat = jnp.dot(q_ref[:, lo:lo + MLA_NOPE], wuk_ref[h], preferred_element_type=F32)
            qlat_ref[h * tq:(h + 1) * tq, :] = qlat.astype(BF16)
            qpe_ref[h * tq:(h + 1) * tq, :] = q_ref[:, lo + MLA_NOPE:lo + MLA_NOPE + MLA_ROPE]

    def sweep(ckv, kpe, k_start):
        s = (lax.dot_general(qlat_ref[...], ckv, _NT, preferred_element_type=F32)
             + lax.dot_general(qpe_ref[...], kpe, _NT, preferred_element_type=F32))
        s = _add_bias(s, _mask_bias(p_len, tq, k_start, ckv.shape[0], s_len), heads)
        _flash_update(s, ckv, m_ref, l_ref, acc_ref)

    @pl.when(j < n_past)
    def _():
        sweep(ckvc_ref[...].astype(BF16), kpec_ref[...].astype(BF16), j * tkv)

    @pl.when(j == n_past)
    def _():
        sweep(ckvn_ref[...], kpen_ref[...], p_len)
        v_w = wuv_ref.shape[2]
        o_lat = _flash_result(l_ref, acc_ref).astype(BF16)
        for h in range(heads):
            o_h = jnp.dot(o_lat[h * tq:(h + 1) * tq], wuv_ref[h], preferred_element_type=F32)
            o_ref[:, h * v_w:(h + 1) * v_w] = o_h.astype(o_ref.dtype)


def mla_attention_past(q, wuk_t, wuv, l, ckv_c, kpe_c, ckv_n, kpe_n, batch, t_len):
    _, heads, _, rank = wuk_t.shape
    v_w = wuv.shape[3]
    p_len = ckv_c.shape[2]
    tn = ckv_n.shape[0] // batch
    tkv = _past_tile(p_len, 1024)
    n_past = p_len // tkv
    rows = heads * t_len
    layer4 = lambda a: pl.BlockSpec((None,) + a.shape[1:], lambda b, j: (l, 0, 0, 0))
    cache = lambda w: pl.BlockSpec((None, None, tkv, w), lambda b, j: (l, b, jnp.minimum(j, n_past - 1), 0))
    new = lambda w: pl.BlockSpec((tn, w), lambda b, j: (b, 0))
    return pl.pallas_call(
        functools.partial(_mla_past_kernel, heads=heads, tq=t_len, tkv=tkv, p_len=p_len, s_len=p_len + t_len),
        grid=(batch, n_past + 1),
        in_specs=[pl.BlockSpec((t_len, q.shape[1]), lambda b, j: (b, 0)), layer4(wuk_t),
                  cache(rank), cache(MLA_ROPE), new(rank), new(MLA_ROPE), layer4(wuv)],
        out_specs=pl.BlockSpec((t_len, heads * v_w), lambda b, j: (b, 0)),
        out_shape=jax.ShapeDtypeStruct((batch * t_len, heads * v_w), BF16),
        scratch_shapes=[pltpu.VMEM((rows, rank), BF16), pltpu.VMEM((rows, MLA_ROPE), BF16)]
                       + _flash_scratch(None, rows, rank),
        compiler_params=_cparams("parallel", "arbitrary"), name="mla_attention_past",
    )(q, wuk_t, ckv_c, kpe_c, ckv_n, kpe_n, wuv)


def _cross_block_kernel(y_ref, x_ref, g_mix_ref, g_pre_ref, wq_ref, k_ref, v_ref, wo_ref, g_post_ref, g_next_ref,
                        xo_ref, ho_ref, *, heads, hd):
    x1 = x_ref[...] + _rms(y_ref[...], g_mix_ref[...])
    q = jnp.dot(_rms(x1, g_pre_ref[...]).astype(BF16), wq_ref[...], preferred_element_type=F32).astype(BF16)
    outs = []
    for h in range(heads):
        sl = slice(h * hd, (h + 1) * hd)
        s = lax.dot_general(q[:, sl], k_ref[:, sl], _NT, preferred_element_type=F32) * (hd ** -0.5)
        p = jnp.exp(s - jnp.max(s, axis=-1, keepdims=True))
        o = jnp.dot(p.astype(BF16), v_ref[:, sl], preferred_element_type=F32)
        outs.append((o / jnp.sum(p, axis=-1, keepdims=True)).astype(BF16))
    y2 = jnp.dot(jnp.concatenate(outs, axis=1), wo_ref[...], preferred_element_type=F32)
    x2 = x1 + _rms(y2, g_post_ref[...])
    xo_ref[...] = x2
    ho_ref[...] = _rms(x2, g_next_ref[...]).astype(ho_ref.dtype)


def cross_block(y_mix, x, g_mix_post, g_pre, wq, mk, mv, wo, l, g_post, g_next, batch, t_len, mem_len, heads, tm):
    m, d = x.shape
    w = wq.shape[2]
    nq = t_len // tm
    row = pl.BlockSpec((tm, d), lambda b, i: (b * nq + i, 0))
    vec = pl.BlockSpec((1, d), lambda b, i: (0, 0))
    kv = pl.BlockSpec((mem_len, w), lambda b, i: (b, 0))
    once = pl.Buffered(1)
    return pl.pallas_call(
        functools.partial(_cross_block_kernel, heads=heads, hd=w // heads),
        grid=(batch, nq),
        in_specs=[row, row, vec, vec,
                  pl.BlockSpec((None, d, w), lambda b, i: (l, 0, 0), pipeline_mode=once), kv, kv,
                  pl.BlockSpec((None, w, d), lambda b, i: (l, 0, 0), pipeline_mode=once), vec, vec],
        out_specs=[row, row],
        out_shape=[jax.ShapeDtypeStruct((m, d), F32), jax.ShapeDtypeStruct((m, d), BF16)],
        compiler_params=_cparams("parallel", "parallel"), name="cross_block",
    )(y_mix, x, g_mix_post.reshape(1, d), g_pre.reshape(1, d), wq, mk, mv, wo,
      g_post.reshape(1, d), g_next.reshape(1, d))


def _rope_tables(pos):
    half = DIFF_D // 2
    inv = ROPE_THETA ** (-jnp.arange(half, dtype=F32) / half)
    ang = pos.astype(F32)[:, None] * inv[None, :]
    cos, sin, zero = jnp.cos(ang), jnp.sin(ang), jnp.zeros_like(ang)
    reps = LANES // DIFF_D
    return (jnp.tile(cos, (1, 2 * reps)), jnp.tile(jnp.concatenate([-sin, zero], 1), (1, reps)),
            jnp.tile(jnp.concatenate([zero, sin], 1), (1, reps)))


def _round_up(a, b):
    return -(-a // b) * b


def _kv_up_weight(w_uk, w_uv):
    depth, rank, heads, nope = w_uk.shape
    k_rows = jnp.concatenate([w_uk, jnp.zeros_like(w_uk)], axis=3).reshape(depth, rank, heads * MLA_KEY_W)
    slot = jnp.concatenate([jnp.zeros((MLA_ROPE, nope), F32), jnp.eye(MLA_ROPE, dtype=F32),
                            jnp.zeros((MLA_ROPE, MLA_KEY_W - nope - MLA_ROPE), F32)], axis=1)
    rope_rows = jnp.broadcast_to(jnp.tile(slot, (1, heads)), (depth, MLA_ROPE, heads * MLA_KEY_W))
    pad = LANES - MLA_ROPE
    w_k = jnp.concatenate([k_rows, rope_rows, jnp.zeros((depth, pad, heads * MLA_KEY_W), F32)], axis=1)
    w_v = jnp.concatenate([w_uv.reshape(depth, rank, -1), jnp.zeros((depth, LANES, heads * w_uv.shape[3]), F32)],
                          axis=1)
    return jnp.concatenate([w_k, w_v], axis=2).astype(BF16)


def _pad_rows(a, batch, rows):
    t_len = a.shape[0] // batch
    a = jnp.pad(a.reshape(batch, t_len, a.shape[1]), ((0, 0), (0, rows - t_len), (0, 0)))
    return a.reshape(batch * rows, a.shape[2])


def _layer(wts, l, x, h, batch, t_len, pos0, tabs, mem_k, mem_v, past, g_next, tiles, state):
    tm, tt, tq_diff, tkv_diff, tq_mla, tkv_mla, tq_x = tiles
    lam_init = 0.8 - 0.6 * math.exp(-0.3 * l)
    pool_w, diff_w, q_rank, kv_rank = wts["dims"]
    heads_d = diff_w // (2 * DIFF_D)
    heads_m = wts["w_uk_t"].shape[1]

    (u, qd, kd16, vd16, cqn, ckv16, kpe16, kc16), state = input_projection(
        h, wts["w_in"], wts["w_in_lat"], tabs, wts["g_mla_q"][l], wts["g_mla_kv"][l], wts["dims"],
        min(tm, 512), tm, l, wts["depth"], state)

    keep = POOL_HALO - 1
    if past is None:
        prev = jnp.zeros((batch, POOL_HALO, pool_w), F32)
    else:
        prev = jnp.concatenate([jnp.zeros((batch, 1, pool_w), F32), past["pool"]], axis=1)
    u3 = u.reshape(batch, t_len, pool_w)
    if t_len >= keep:
        new_pool = u3[:, t_len - keep:]
    else:
        new_pool = jnp.concatenate([prev[:, 1 + t_len:], u3], axis=1)
    y_pool = pool_mix(u, prev, wts["w_pool"], l, wts["pool_scale"][l], batch, t_len, pos0, tt)

    lq = wts["diff_lambda"][l]
    lam = jnp.exp(jnp.sum(lq[0] * lq[1])) - jnp.exp(jnp.sum(lq[2] * lq[3])) + lam_init
    q = mla_query_projection(cqn, wts["w_uq"], tabs, l, tm, 1024)
    if past is None:
        o_diff = diff_attention(qd, kd16, vd16, lam, wts["g_diff_sub"][l], batch, heads_d, t_len,
                                1.0 - lam_init, tq_diff, tkv_diff)
        kv_up = matmul(kc16, wts["w_kv_up"], l, BF16, tm, heads_m * MLA_NOPE)
        o_mla = mla_attention(q, kv_up, batch, heads_m, t_len, tq_mla, tkv_mla)
    else:
        tn = _round_up(t_len, LANES)
        o_diff = diff_attention_past(qd, past["diff_k"], past["diff_v"], _pad_rows(kd16, batch, tn),
                                     _pad_rows(vd16, batch, tn), l, lam, wts["g_diff_sub"][l], batch, heads_d,
                                     t_len, 1.0 - lam_init)
        o_mla = mla_attention_past(q, wts["w_uk_t"], wts["w_uv"], l, past["ckv"], past["kpe"],
                                   _pad_rows(ckv16, batch, tn), _pad_rows(kpe16, batch, tn), batch, t_len)

    y = matmul_cat([y_pool, o_diff, o_mla], wts["w_out"], l, F32, tm, 1024)
    x, h = cross_block(y, x, wts["g_mix_post"][l], wts["g_x_pre"][l], wts["w_mem_q"], mem_k, mem_v,
                       wts["w_mem_o"], l, wts["g_x_post"][l], wts["g_ff_pre"][l], batch, t_len,
                       mem_k.shape[0] // batch, wts["mem_heads"], tq_x)

    a = matmul_swiglu(h, wts["w_gate"], wts["w_up"], l, 2 * tm, 256)
    y = matmul(a, wts["w_down"], l, F32, 512, 512)
    x, h = post_res(y, x, wts["g_ff_post"][l], g_next, tt)
    return x, h, state, new_pool


def kernel(x_prompt, x_sample, cache_diff_k, cache_diff_v, cache_mla_ckv, cache_mla_kpe, cache_pool, cache_mem_k, cache_mem_v, mem_prompt, g_mix_pre, w_in, w_pool, pool_scale, diff_lambda, g_diff_sub, g_mla_q, w_mla_uq, w_mla_uk, w_mla_uv, g_mla_kv, w_out, g_mix_post, g_mem, w_mem_k, w_mem_v, w_mem_q, w_mem_o, g_x_pre, g_x_post, g_ff_pre, w_gate, w_up, w_down, g_ff_post):
    depth = w_in.shape[0]
    bp, tp, d = x_prompt.shape
    bs, ts, _ = x_sample.shape
    past_len = cache_mla_ckv.shape[2]
    pool_w = cache_pool.shape[3]
    heads_d, diff_w = cache_diff_k.shape[3], cache_diff_k.shape[3] * cache_diff_k.shape[4]
    q_rank, kv_rank = g_mla_q.shape[1], g_mla_kv.shape[1]
    mla_heads = w_mla_uk.shape[2]
    mem_len, mem_heads, mem_hd = cache_mem_k.shape[2:]
    mem_w = mem_heads * mem_hd

    main_w = pool_w + 3 * diff_w + q_rank
    lat_pad = main_w + kv_rank + LANES - w_in.shape[2]
    uq = w_mla_uq.reshape(depth, q_rank, mla_heads, MLA_NOPE + MLA_ROPE)
    wts = {
        "depth": depth, "dims": (pool_w, diff_w, q_rank, kv_rank), "mem_heads": mem_heads,
        "w_in": w_in[:, :, :main_w].astype(BF16),
        "w_in_lat": jnp.pad(w_in[:, :, main_w:], ((0, 0), (0, 0), (0, lat_pad))).astype(BF16),
        "w_pool": w_pool.astype(BF16), "pool_scale": pool_scale, "diff_lambda": diff_lambda,
        "g_diff_sub": g_diff_sub, "g_mla_q": g_mla_q, "g_mla_kv": g_mla_kv,
        "w_uq": jnp.pad(uq, ((0, 0), (0, 0), (0, 0), (0, MLA_KEY_W - MLA_NOPE - MLA_ROPE))).reshape(
            depth, q_rank, mla_heads * MLA_KEY_W).astype(BF16),
        "w_uk_t": jnp.transpose(w_mla_uk, (0, 2, 3, 1)).astype(BF16),
        "w_uv": jnp.transpose(w_mla_uv, (0, 2, 1, 3)).astype(BF16),
        "w_kv_up": _kv_up_weight(w_mla_uk, w_mla_uv),
        "w_out": w_out.astype(BF16), "g_mix_post": g_mix_post,
        "w_mem_q": w_mem_q.astype(BF16), "w_mem_o": w_mem_o.astype(BF16),
        "g_x_pre": g_x_pre, "g_x_post": g_x_post, "g_ff_pre": g_ff_pre,
        "w_gate": w_gate, "w_up": w_up, "w_down": w_down.astype(BF16),
        "g_ff_post": g_ff_post,
    }
    w_mem_kv = jnp.concatenate([w_mem_k, w_mem_v], axis=2).astype(BF16)

    tabs_p = _rope_tables(jnp.arange(tp))
    tabs_s = tuple(jnp.tile(t, (bs, 1)) for t in _rope_tables(past_len + jnp.arange(ts)))
    tiles_p = (1024, 256, 512, 512, 512, 512, 256)
    tiles_s = (bs * ts, ts, ts, 0, ts, 0, ts)

    xp = x_prompt.reshape(bp * tp, d)
    xs = x_sample.reshape(bs * ts, d)
    hp = rmsnorm_bf16(xp, g_mix_pre[0], tiles_p[1])
    hs = rmsnorm_bf16(xs, g_mix_pre[0], tiles_s[1])
    mem_n = mem_prompt.reshape(bp * mem_len, d)
    mem_k_s = cache_mem_k.reshape(depth, bs * mem_len, mem_w).astype(BF16)
    mem_v_s = cache_mem_v.reshape(depth, bs * mem_len, mem_w).astype(BF16)
    past = {"diff_k": jnp.transpose(cache_diff_k, (0, 1, 3, 2, 4)).astype(BF16),
            "diff_v": jnp.transpose(cache_diff_v, (0, 1, 3, 2, 4)).astype(BF16),
            "ckv": cache_mla_ckv, "kpe": cache_mla_kpe}

    state_p = state_s = None
    pools_p, pools_s, mem_ks, mem_vs = [], [], [], []
    for l in range(depth):
        g_next = g_mix_pre[l + 1] if l + 1 < depth else None
        m = rmsnorm_bf16(mem_n, g_mem[l], 256)
        mkv = matmul(m, w_mem_kv, l, F32, 1024, 1024)
        mk, mv = mkv[:, :mem_w], mkv[:, mem_w:]
        mem_ks.append(mk.reshape(bp, mem_len, mem_heads, mem_hd))
        mem_vs.append(mv.reshape(bp, mem_len, mem_heads, mem_hd))
        xp, hp, state_p, pool_p = _layer(wts, l, xp, hp, bp, tp, 0, tabs_p, mk.astype(BF16), mv.astype(BF16),
                                         None, g_next, tiles_p, state_p)
        pools_p.append(pool_p)
        xs, hs, state_s, pool_s = _layer(wts, l, xs, hs, bs, ts, past_len, tabs_s, mem_k_s[l], mem_v_s[l],
                                         dict(past, pool=cache_pool[l]), g_next, tiles_s, state_s)
        pools_s.append(pool_s)

    hd = 2 * DIFF_D
    shape_p = lambda a, tail: a.reshape((depth, bp, tp) + tail)
    shape_s = lambda a, tail: a.reshape((depth, bs, ts) + tail)
    return (xp.reshape(bp, tp, d), xs.reshape(bs, ts, d),
            shape_p(state_p[0], (heads_d, hd)), shape_p(state_p[1], (heads_d, hd)),
            shape_p(state_p[2], (kv_rank,)), shape_p(state_p[3], (MLA_ROPE,)),
            jnp.stack(pools_p), jnp.stack(mem_ks), jnp.stack(mem_vs),
            shape_s(state_s[0], (heads_d, hd)), shape_s(state_s[1], (heads_d, hd)),
            shape_s(state_s[2], (kv_rank,)), shape_s(state_s[3], (MLA_ROPE,)),
            jnp.stack(pools_s))
```

```python
import functools
import math

import jax
import jax.numpy as jnp
from jax import lax
from jax.experimental import pallas as pl
from jax.experimental.pallas import tpu as pltpu

F32 = jnp.float32
BF16 = jnp.bfloat16

EPS = 1e-6
CHUNK = 64
ROPE_THETA = 10000.0
POOL_WINDOWS = (2, 4, 8, 16)
POOL_HALO = 16
DIFF_D = 64
MLA_NOPE = 128
MLA_ROPE = 64
LANES = 128
MIB = 1024 * 1024
VMEM_LIMIT = 56 * MIB

_NT = (((1,), (1,)), ((), ()))


def _cparams(*sem):
    return pltpu.CompilerParams(dimension_semantics=sem, vmem_limit_bytes=VMEM_LIMIT)


def _rms(x, g):
    return x * lax.rsqrt(jnp.mean(x * x, axis=-1, keepdims=True) + EPS) * g


def _rmsnorm_kernel(x_ref, g_ref, o_ref):
    o_ref[...] = _rms(x_ref[...], g_ref[...]).astype(o_ref.dtype)


def rmsnorm_bf16(x, g, tm):
    m, d = x.shape
    return pl.pallas_call(
        _rmsnorm_kernel,
        grid=(m // tm,),
        in_specs=[pl.BlockSpec((tm, d), lambda i: (i, 0)), pl.BlockSpec((1, d), lambda i: (0, 0))],
        out_specs=pl.BlockSpec((tm, d), lambda i: (i, 0)),
        out_shape=jax.ShapeDtypeStruct((m, d), BF16),
        compiler_params=_cparams("parallel"),
        name="rmsnorm",
    )(x, g.reshape(1, d))


def _post_res_kernel(y_ref, x_ref, gp_ref, gn_ref, xo_ref, ho_ref):
    x = x_ref[...] + _rms(y_ref[...], gp_ref[...])
    xo_ref[...] = x
    ho_ref[...] = _rms(x, gn_ref[...]).astype(ho_ref.dtype)


def _post_res_last_kernel(y_ref, x_ref, gp_ref, xo_ref):
    xo_ref[...] = x_ref[...] + _rms(y_ref[...], gp_ref[...])


def post_res(y, x, g_post, g_next, tm):
    m, d = x.shape
    row = pl.BlockSpec((tm, d), lambda i: (i, 0))
    vec = pl.BlockSpec((1, d), lambda i: (0, 0))
    if g_next is None:
        return pl.pallas_call(
            _post_res_last_kernel, grid=(m // tm,), in_specs=[row, row, vec], out_specs=row,
            out_shape=jax.ShapeDtypeStruct((m, d), F32), compiler_params=_cparams("parallel"),
            name="post_res_last",
        )(y, x, g_post.reshape(1, d)), None
    return pl.pallas_call(
        _post_res_kernel, grid=(m // tm,), in_specs=[row, row, vec, vec], out_specs=[row, row],
        out_shape=[jax.ShapeDtypeStruct((m, d), F32), jax.ShapeDtypeStruct((m, d), BF16)],
        compiler_params=_cparams("parallel"), name="post_res",
    )(y, x, g_post.reshape(1, d), g_next.reshape(1, d))


def _mm_kernel(x_ref, w_ref, o_ref):
    o_ref[...] = jnp.dot(x_ref[...], w_ref[...], preferred_element_type=F32).astype(o_ref.dtype)


def matmul(x, w, l, out_dtype, tm, tn):
    m, k = x.shape
    n = w.shape[2]
    tm, tn = min(tm, m), min(tn, n)
    return pl.pallas_call(
        _mm_kernel, grid=(n // tn, m // tm),
        in_specs=[pl.BlockSpec((tm, k), lambda j, i: (i, 0)), pl.BlockSpec((None, k, tn), lambda j, i: (l, 0, j))],
        out_specs=pl.BlockSpec((tm, tn), lambda j, i: (i, j)),
        out_shape=jax.ShapeDtypeStruct((m, n), out_dtype),
        compiler_params=_cparams("parallel", "parallel"), name="matmul",
    )(x, w)


def _mm_cat_kernel(*refs):
    n_in = (len(refs) - 1) // 2
    acc = jnp.dot(refs[0][...], refs[n_in][...], preferred_element_type=F32)
    for a in range(1, n_in):
        acc += jnp.dot(refs[a][...], refs[n_in + a][...], preferred_element_type=F32)
    refs[-1][...] = acc.astype(refs[-1].dtype)


def matmul_cat(xs, w, l, out_dtype, tm, tn):
    m = xs[0].shape[0]
    n = w.shape[2]
    tm, tn = min(tm, m), min(tn, n)
    x_specs, w_specs, off = [], [], 0
    for x in xs:
        k = x.shape[1]
        assert off % k == 0, "each input's row band of w must start on a multiple of its own width"
        x_specs.append(pl.BlockSpec((tm, k), lambda j, i: (i, 0)))
        w_specs.append(pl.BlockSpec((None, k, tn), functools.partial(lambda j, i, r: (l, r, j), r=off // k)))
        off += k
    return pl.pallas_call(
        _mm_cat_kernel, grid=(n // tn, m // tm),
        in_specs=x_specs + w_specs,
        out_specs=pl.BlockSpec((tm, tn), lambda j, i: (i, j)),
        out_shape=jax.ShapeDtypeStruct((m, n), out_dtype),
        compiler_params=_cparams("parallel", "parallel"), name="matmul_cat",
    )(*xs, *([w] * len(xs)))


def _swiglu_kernel(x_ref, wg_ref, wu_ref, o_ref):
    x = x_ref[...]
    g = jnp.dot(x, wg_ref[...].astype(BF16), preferred_element_type=F32)
    u = jnp.dot(x, wu_ref[...].astype(BF16), preferred_element_type=F32)
    o_ref[...] = (g * jax.nn.sigmoid(g) * u).astype(o_ref.dtype)


def matmul_swiglu(x, wg, wu, l, tm, tn):
    m, k = x.shape
    n = wg.shape[2]
    tm = min(tm, m)
    wspec = pl.BlockSpec((None, k, tn), lambda i, j: (l, 0, j))
    return pl.pallas_call(
        _swiglu_kernel, grid=(m // tm, n // tn),
        in_specs=[pl.BlockSpec((tm, k), lambda i, j: (i, 0), pipeline_mode=pl.Buffered(1)), wspec, wspec],
        out_specs=pl.BlockSpec((tm, tn), lambda i, j: (i, j)),
        out_shape=jax.ShapeDtypeStruct((m, n), BF16),
        compiler_params=_cparams("parallel", "parallel"), name="matmul_swiglu",
    )(x, wg, wu)


def _swiglu_pair_kernel(x_ref, wg_ref, wu_ref, xs_ref, o_ref, os_ref):
    wg, wu = wg_ref[...].astype(BF16), wu_ref[...].astype(BF16)

    def gated(x):
        g = jnp.dot(x, wg, preferred_element_type=F32)
        u = jnp.dot(x, wu, preferred_element_type=F32)
        return (g * jax.nn.sigmoid(g) * u).astype(BF16)

    o_ref[...] = gated(x_ref[...])

    @pl.when(pl.program_id(0) == 0)
    def _():
        os_ref[...] = gated(xs_ref[...])


def matmul_swiglu_pair(x, xs, wg, wu, l, tm, tn):
    m, k = x.shape
    ms = xs.shape[0]
    n = wg.shape[2]
    tm = min(tm, m)
    nj = n // tn
    wspec = pl.BlockSpec((None, k, tn), lambda i, j: (l, 0, j))
    once = pl.Buffered(1)
    return pl.pallas_call(
        _swiglu_pair_kernel, grid=(m // tm, nj),
        in_specs=[pl.BlockSpec((tm, k), lambda i, j: (i, 0), pipeline_mode=once), wspec, wspec,
                  pl.BlockSpec((ms, k), lambda i, j: (0, 0), pipeline_mode=once)],
        out_specs=[pl.BlockSpec((tm, tn), lambda i, j: (i, j)),
                   pl.BlockSpec((ms, tn), lambda i, j: (0, jnp.where(i == 0, j, nj - 1)))],
        out_shape=[jax.ShapeDtypeStruct((m, n), BF16), jax.ShapeDtypeStruct((ms, n), BF16)],
        compiler_params=_cparams("arbitrary", "arbitrary"), name="matmul_swiglu_pair",
    )(x, wg, wu, xs)


def _mm_pair_kernel(x_ref, w_ref, xs_ref, o_ref, os_ref):
    o_ref[...] = jnp.dot(x_ref[...], w_ref[...], preferred_element_type=F32).astype(o_ref.dtype)

    @pl.when(pl.program_id(1) == 0)
    def _():
        os_ref[...] = jnp.dot(xs_ref[...], w_ref[...], preferred_element_type=F32).astype(os_ref.dtype)


def matmul_pair(x, xs, w, l, out_dtype, tm, tn):
    m, k = x.shape
    ms = xs.shape[0]
    n = w.shape[2]
    tm, tn = min(tm, m), min(tn, n)
    return pl.pallas_call(
        _mm_pair_kernel, grid=(n // tn, m // tm),
        in_specs=[pl.BlockSpec((tm, k), lambda j, i: (i, 0)), pl.BlockSpec((None, k, tn), lambda j, i: (l, 0, j)),
                  pl.BlockSpec((ms, k), lambda j, i: (0, 0), pipeline_mode=pl.Buffered(1))],
        out_specs=[pl.BlockSpec((tm, tn), lambda j, i: (i, j)), pl.BlockSpec((ms, tn), lambda j, i: (0, j))],
        out_shape=[jax.ShapeDtypeStruct((m, n), out_dtype), jax.ShapeDtypeStruct((ms, n), out_dtype)],
        compiler_params=_cparams("arbitrary", "arbitrary"), name="matmul_pair",
    )(x, w, xs)


def _q_proj_kernel(x_ref, w_ref, cos_ref, sa_ref, sb_ref, o_ref, *, scale):
    acc = jnp.dot(x_ref[...], w_ref[...], preferred_element_type=F32)
    cos, sa, sb = cos_ref[...], sa_ref[...], sb_ref[...]
    for s in range(acc.shape[1] // MLA_KEY_W):
        lo, mid, hi = s * MLA_KEY_W, s * MLA_KEY_W + MLA_NOPE, (s + 1) * MLA_KEY_W
        o_ref[:, lo:mid] = (acc[:, lo:mid] * scale).astype(o_ref.dtype)
        o_ref[:, mid:hi] = (_rope_chunk(acc[:, mid:hi], cos, sa, sb) * scale).astype(o_ref.dtype)


def mla_query_projection(cqn, w_slots, tabs, l, tm, tn):
    m, k = cqn.shape
    n = w_slots.shape[2]
    t_tab = tabs[0].shape[0]
    tm, tn = min(tm, t_tab), min(tn, n)
    nt = t_tab // tm
    tab = pl.BlockSpec((tm, LANES), lambda j, i: (i % nt, 0))
    return pl.pallas_call(
        functools.partial(_q_proj_kernel, scale=(MLA_NOPE + MLA_ROPE) ** -0.5),
        grid=(n // tn, m // tm),
        in_specs=[pl.BlockSpec((tm, k), lambda j, i: (i, 0)), pl.BlockSpec((None, k, tn), lambda j, i: (l, 0, j)),
                  tab, tab, tab],
        out_specs=pl.BlockSpec((tm, tn), lambda j, i: (i, j)),
        out_shape=jax.ShapeDtypeStruct((m, n), BF16),
        compiler_params=_cparams("parallel", "parallel"), name="mla_query_projection",
    )(cqn, w_slots, *tabs)


def _rope_chunk(x, cos, sa, sb):
    return x * cos + pltpu.roll(x, LANES - 32, 1) * sa + pltpu.roll(x, 32, 1) * sb


def _rope_cols(x, cos, sa, sb):
    return jnp.concatenate([_rope_chunk(x[:, c * LANES:(c + 1) * LANES], cos, sa, sb)
                            for c in range(x.shape[1] // LANES)], axis=1)


_IN_GROUPS = 5
_IN_INPUTS = 6


def _in_proj_kernel(*refs, q_scale):
    x_ref, w_ref, cos_ref, sa_ref, sb_ref, gq_ref = refs[:_IN_INPUTS]
    u_ref, qd_ref, kd32_ref, kd16_ref, vd32_ref, vd16_ref, cq_ref = refs[-7:]
    j = pl.program_id(1)
    acc = jnp.dot(x_ref[...], w_ref[...], preferred_element_type=F32)
    rope = lambda a: _rope_cols(a, cos_ref[...], sa_ref[...], sb_ref[...])

    @pl.when(j == 0)
    def _():
        u_ref[...] = acc

    @pl.when(j == 1)
    def _():
        qd_ref[...] = (rope(acc) * q_scale).astype(BF16)

    @pl.when(j == 2)
    def _():
        k = rope(acc)
        kd32_ref[...] = k
        kd16_ref[...] = k.astype(BF16)

    @pl.when(j == 3)
    def _():
        vd32_ref[...] = acc
        vd16_ref[...] = acc.astype(BF16)

    @pl.when(j == 4)
    def _():
        cq_ref[...] = _rms(acc, gq_ref[...]).astype(BF16)


_LAT_INPUTS = 6


def _in_proj_latent_kernel(*refs):
    x_ref, w_ref, cos_ref, sa_ref, sb_ref, gkv_ref = refs[:_LAT_INPUTS]
    ckv32_ref, ckv16_ref, kpe32_ref, kpe16_ref, kc16_ref = refs[-5:]
    rank = ckv32_ref.shape[1]
    acc = jnp.dot(x_ref[...], w_ref[...], preferred_element_type=F32)
    ckv = _rms(acc[:, 0:rank], gkv_ref[...])
    ckv32_ref[...] = ckv
    ckv16_ref[...] = ckv.astype(BF16)
    lane = lax.broadcasted_iota(jnp.int32, (acc.shape[0], LANES), 1)
    kpe_raw = jnp.where(lane < MLA_ROPE, acc[:, rank:rank + LANES], 0.0)
    kpe_chunk = _rope_chunk(kpe_raw, cos_ref[...], sa_ref[...], sb_ref[...])
    kpe = kpe_chunk[:, :MLA_ROPE]
    kpe32_ref[...] = kpe
    kpe16_ref[...] = kpe.astype(BF16)
    kc16_ref[:, 0:rank] = ckv.astype(BF16)
    kc16_ref[:, rank:rank + LANES] = kpe_chunk.astype(BF16)


def input_projection(h, w_in, tabs, g_q, g_kv, dims, tm, tm_lat, l, depth, state):
    m, d = h.shape
    pool_w, diff_w, q_rank, kv_rank = dims
    gw = diff_w
    lw = kv_rank + LANES
    assert pool_w == gw and q_rank == gw and w_in.shape[2] == _IN_GROUPS * gw + kv_rank + MLA_ROPE
    assert (_IN_GROUPS * gw) % lw == 0, "the latent columns must start on a multiple of their block width"
    t_tab = tabs[0].shape[0]
    tm, tm_lat = min(tm, t_tab), min(tm_lat, t_tab)
    state = (None, None) if state is None else ((state[0], state[1]), (state[2], state[3]))
    sds = lambda w, dt: jax.ShapeDtypeStruct((m, w), dt)
    stk = lambda w: jax.ShapeDtypeStruct((depth, m, w), F32)
    alias = pl.BlockSpec(memory_space=pl.ANY)

    nt = t_tab // tm
    row = lambda w: pl.BlockSpec((tm, w), lambda i, j: (i, 0))
    lay = lambda w: pl.BlockSpec((None, tm, w), lambda i, j: (l, i, 0))
    tab = pl.BlockSpec((tm, LANES), lambda i, j: (i % nt, 0))
    prev = () if state[0] is None else state[0]
    u, qd, kd32, kd16, vd32, vd16, cqn = pl.pallas_call(
        functools.partial(_in_proj_kernel, q_scale=DIFF_D ** -0.5),
        grid=(m // tm, _IN_GROUPS),
        in_specs=[pl.BlockSpec((tm, d), lambda i, j: (i, 0)),
                  pl.BlockSpec((None, d, gw), lambda i, j: (l, 0, j)),
                  tab, tab, tab, pl.BlockSpec((1, gw), lambda i, j: (0, 0))] + [alias] * len(prev),
        out_specs=[row(gw), row(gw), lay(gw), row(gw), lay(gw), row(gw), row(gw)],
        out_shape=[sds(gw, F32), sds(gw, BF16), stk(gw), sds(gw, BF16), stk(gw), sds(gw, BF16), sds(gw, BF16)],
        input_output_aliases={_IN_INPUTS + a: o for a, o in enumerate((2, 4)[:len(prev)])},
        compiler_params=_cparams("parallel", "arbitrary"), name="in_proj",
    )(h, w_in, *tabs, g_q.reshape(1, gw), *prev)

    nt = t_tab // tm_lat
    row = lambda w: pl.BlockSpec((tm_lat, w), lambda i: (i, 0))
    lay = lambda w: pl.BlockSpec((None, tm_lat, w), lambda i: (l, i, 0))
    tab = pl.BlockSpec((tm_lat, LANES), lambda i: (i % nt, 0))
    prev = () if state[1] is None else state[1]
    ckv32, ckv16, kpe32, kpe16, kc16 = pl.pallas_call(
        _in_proj_latent_kernel,
        grid=(m // tm_lat,),
        in_specs=[pl.BlockSpec((tm_lat, d), lambda i: (i, 0)),
                  pl.BlockSpec((None, d, lw), lambda i: (l, 0, _IN_GROUPS * gw // lw)),
                  tab, tab, tab, pl.BlockSpec((1, kv_rank), lambda i: (0, 0))] + [alias] * len(prev),
        out_specs=[lay(kv_rank), row(kv_rank), lay(MLA_ROPE), row(MLA_ROPE), row(lw)],
        out_shape=[stk(kv_rank), sds(kv_rank, BF16), stk(MLA_ROPE), sds(MLA_ROPE, BF16), sds(lw, BF16)],
        input_output_aliases={_LAT_INPUTS + a: o for a, o in enumerate((0, 2)[:len(prev)])},
        compiler_params=_cparams("parallel"), name="in_proj_latent",
    )(h, w_in, *tabs, g_kv.reshape(1, kv_rank), *prev)
    return (u, qd, kd16, vd16, cqn, ckv16, kpe16, kc16), (kd32, vd32, ckv32, kpe32)


def _pool_kernel(z_ref, halo_ref, prev_ref, wp_ref, sc_ref, o_ref, ext_ref, *, tt, pos0):
    i = pl.program_id(1)
    ext_ref[0:POOL_HALO, :] = jnp.where(i == 0, prev_ref[0], halo_ref[...])
    ext_ref[POOL_HALO:POOL_HALO + tt, :] = z_ref[...]
    pos = pos0 + i * tt + lax.broadcasted_iota(jnp.int32, (tt, 1), 0)
    group = wp_ref.shape[1]
    for g, w in enumerate(POOL_WINDOWS):
        cs = slice(g * group, (g + 1) * group)
        s = ext_ref[POOL_HALO:POOL_HALO + tt, cs]
        u = s
        for j in range(1, w):
            s = s + ext_ref[POOL_HALO - j:POOL_HALO - j + tt, cs]
        cnt = jnp.minimum(w, pos + 1).astype(F32)
        d = s / cnt - u
        y = jnp.dot(d.astype(BF16), wp_ref[g], preferred_element_type=F32) * sc_ref[:, cs]
        o_ref[:, cs] = y.astype(o_ref.dtype)


def pool_mix(z, prev, w_pool, l, scale, batch, t_len, pos0, tt):
    m = z.shape[0]
    c = prev.shape[2]
    nt = t_len // tt
    hb = tt // POOL_HALO
    return pl.pallas_call(
        functools.partial(_pool_kernel, tt=tt, pos0=pos0),
        grid=(batch, nt),
        in_specs=[pl.BlockSpec((tt, c), lambda b, i: (b * nt + i, 0)),
                  pl.BlockSpec((POOL_HALO, c), lambda b, i: (jnp.maximum((b * nt + i) * hb - 1, 0), 0)),
                  pl.BlockSpec((1, POOL_HALO, c), lambda b, i: (b, 0, 0)),
                  pl.BlockSpec((None,) + w_pool.shape[1:], lambda b, i: (l, 0, 0, 0)),
                  pl.BlockSpec((1, c), lambda b, i: (0, 0))],
        out_specs=pl.BlockSpec((tt, c), lambda b, i: (b * nt + i, 0)),
        out_shape=jax.ShapeDtypeStruct((m, c), BF16),
        scratch_shapes=[pltpu.VMEM((POOL_HALO + tt, c), F32)],
        compiler_params=_cparams("parallel", "parallel"), name="pool_mix",
    )(z, z, prev, w_pool, scale.reshape(1, c))


def _last_kv_block(i, tq, tkv, q0, s_len):
    last_q = q0 + (i + 1) * tq - 1
    visible = jnp.minimum((last_q // CHUNK + 1) * CHUNK, s_len)
    return (visible - 1) // tkv


def _mask_bias(q_start, tq, k_start, tkv, s_len):
    qpos = q_start + lax.broadcasted_iota(jnp.int32, (tq, 1), 0)
    kpos = k_start + lax.broadcasted_iota(jnp.int32, (1, tkv), 1)
    shift = CHUNK.bit_length() - 1
    ok = (jnp.right_shift(kpos, shift) <= jnp.right_shift(qpos, shift)) & (kpos < s_len)
    return jnp.where(ok, 0.0, -jnp.inf).astype(F32)


def _add_bias(s, bias, groups):
    rows, tkv = s.shape
    return (s.reshape(groups, rows // groups, tkv) + bias[None]).reshape(rows, tkv)


def _lane_tile(x, width):
    return x if width == LANES else jnp.concatenate([x] * (width // LANES), axis=1)


def _flash_update(s, v, m_ref, l_ref, acc_ref):
    tkv = s.shape[1]
    m_prev = m_ref[...]
    m_new = jnp.maximum(m_prev, jnp.max(s, axis=-1, keepdims=True))
    alpha = jnp.exp(m_prev - m_new)
    p = jnp.exp(s - _lane_tile(m_new, tkv))
    psum = p[:, :LANES]
    for c in range(1, tkv // LANES):
        psum = psum + p[:, c * LANES:(c + 1) * LANES]
    l_ref[...] = alpha * l_ref[...] + psum
    acc_ref[...] = (_lane_tile(alpha, acc_ref.shape[1]) * acc_ref[...]
                    + jnp.dot(p.astype(BF16), v, preferred_element_type=F32))
    m_ref[...] = m_new


def _flash_init(m_ref, l_ref, acc_ref):
    m_ref[...] = jnp.full(m_ref.shape, -jnp.inf, F32)
    l_ref[...] = jnp.zeros(l_ref.shape, F32)
    acc_ref[...] = jnp.zeros(acc_ref.shape, F32)


def _flash_result(l_ref, acc_ref):
    return acc_ref[...] / jnp.sum(l_ref[...], axis=-1, keepdims=True)


def _flash_scratch(groups, rows, dv):
    lead = () if groups is None else (groups,)
    return [pltpu.VMEM(lead + (rows, LANES), F32), pltpu.VMEM(lead + (rows, LANES), F32),
            pltpu.VMEM(lead + (rows, dv), F32)]


def _full_kv_blocks(i, tq, tkv, q0, s_len):
    first_q = q0 + i * tq
    return jnp.minimum((first_q // CHUNK + 1) * CHUNK, s_len) // tkv


def _with_ones(v):
    return jnp.concatenate([v, jnp.ones((v.shape[0], LANES), v.dtype)], axis=1)


def _flash_update_wide(s, v_ones, m_ref, acc_ref):
    tkv = s.shape[1]
    m_prev = m_ref[...]
    m_new = jnp.maximum(m_prev, jnp.max(s, axis=-1, keepdims=True))
    alpha = jnp.exp(m_prev - m_new)
    p = jnp.exp((s - _lane_tile(m_new, tkv)).astype(BF16))
    acc_ref[...] = (_lane_tile(alpha, acc_ref.shape[1]) * acc_ref[...]
                    + jnp.dot(p, v_ones, preferred_element_type=F32))
    m_ref[...] = m_new


def _flash_init_wide(m_ref, acc_ref):
    m_ref[...] = jnp.full(m_ref.shape, -jnp.inf, F32)
    acc_ref[...] = jnp.zeros(acc_ref.shape, F32)


def _flash_result_wide(acc_ref):
    dv = acc_ref.shape[-1] - LANES
    assert dv == LANES
    return acc_ref[:, 0:dv] / acc_ref[:, dv:dv + LANES]


def _flash_scratch_wide(groups, rows, dv):
    return [pltpu.VMEM((groups, rows, LANES), F32), pltpu.VMEM((groups, rows, dv + LANES), F32)]


def _past_tile(p_len, cap):
    tile = max(t for t in range(LANES, cap + 1, LANES) if p_len % t == 0)
    return tile


def _diff_split_q(q_ref, qs_ref, heads, tq):
    hd = 2 * DIFF_D
    lane = lax.broadcasted_iota(jnp.int32, (tq, hd), 1)
    for h in range(heads):
        q = q_ref[:, h * hd:(h + 1) * hd]
        qs_ref[h, 0:tq, :] = jnp.where(lane < DIFF_D, q, jnp.zeros_like(q))
        qs_ref[h, tq:2 * tq, :] = jnp.where(lane >= DIFF_D, q, jnp.zeros_like(q))


def _heads_sweep(q_of, k_of, v_of, bias, groups, m_ref, acc_ref, heads):
    for h in range(heads):
        s = lax.dot_general(q_of(h), k_of(h), _NT, preferred_element_type=F32)
        if bias is not None:
            s = s + bias if groups == 1 else _add_bias(s, bias, groups)
        _flash_update_wide(s, _with_ones(v_of(h)), m_ref.at[h], acc_ref.at[h])


def _diff_finish(lam_ref, g_ref, o_ref, acc_ref, heads, tq, out_scale):
    hd = 2 * DIFF_D
    lam = lam_ref[0, 0]
    for h in range(heads):
        a = _flash_result_wide(acc_ref.at[h])
        o = a[0:tq] - lam * a[tq:2 * tq]
        o_ref[:, h * hd:(h + 1) * hd] = (_rms(o, g_ref[...]) * out_scale).astype(o_ref.dtype)


def _diff_kernel(lam_ref, q_ref, k_ref, v_ref, g_ref, o_ref, qs_ref, m_ref, acc_ref, *,
                 heads, tq, tkv, s_len, out_scale):
    i, j = pl.program_id(1), pl.program_id(2)
    hd = 2 * DIFF_D
    k_of = lambda h: k_ref[:, h * hd:(h + 1) * hd]
    v_of = lambda h: v_ref[:, h * hd:(h + 1) * hd]
    full = _full_kv_blocks(i, tq, tkv, 0, s_len)

    @pl.when(j == 0)
    def _():
        _flash_init_wide(m_ref, acc_ref)
        _diff_split_q(q_ref, qs_ref, heads, tq)

    @pl.when(j < full)
    def _():
        _heads_sweep(lambda h: qs_ref[h], k_of, v_of, None, 2, m_ref, acc_ref, heads)

    @pl.when((j >= full) & (j <= _last_kv_block(i, tq, tkv, 0, s_len)))
    def _():
        _heads_sweep(lambda h: qs_ref[h], k_of, v_of, _mask_bias(i * tq, tq, j * tkv, tkv, s_len), 2,
                     m_ref, acc_ref, heads)

    @pl.when(j == pl.num_programs(2) - 1)
    def _():
        _diff_finish(lam_ref, g_ref, o_ref, acc_ref, heads, tq, out_scale)


def diff_attention(q, k, v, lam, g_sub, batch, heads, t_len, out_scale, tq, tkv):
    hd = 2 * DIFF_D
    w = heads * hd
    nq, nkv = t_len // tq, t_len // tkv
    last = functools.partial(_last_kv_block, tq=tq, tkv=tkv, q0=0, s_len=t_len)
    kvmap = lambda b, i, j: (b * nkv + jnp.minimum(j, last(i)), 0)
    qmap = lambda b, i, j: (b * nq + i, 0)
    return pl.pallas_call(
        functools.partial(_diff_kernel, heads=heads, tq=tq, tkv=tkv, s_len=t_len, out_scale=out_scale),
        grid=(batch, nq, nkv),
        in_specs=[pl.BlockSpec(memory_space=pltpu.SMEM),
                  pl.BlockSpec((tq, w), qmap),
                  pl.BlockSpec((tkv, w), kvmap),
                  pl.BlockSpec((tkv, w), kvmap),
                  pl.BlockSpec((1, hd), lambda b, i, j: (0, 0))],
        out_specs=pl.BlockSpec((tq, w), qmap),
        out_shape=jax.ShapeDtypeStruct((batch * t_len, w), BF16),
        scratch_shapes=[pltpu.VMEM((heads, 2 * tq, hd), BF16)] + _flash_scratch_wide(heads, 2 * tq, hd),
        compiler_params=_cparams("parallel", "parallel", "arbitrary"), name="diff_attention",
    )(lam.reshape(1, 1), q, k, v, g_sub.reshape(1, hd))


def _diff_past_kernel(lam_ref, q_ref, kc_ref, vc_ref, kn_ref, vn_ref, g_ref, o_ref,
                      qs_ref, m_ref, acc_ref, *, heads, tq, tkv, p_len, s_len, out_scale):
    j = pl.program_id(1)
    n_past = pl.num_programs(1) - 1
    hd = 2 * DIFF_D

    @pl.when(j == 0)
    def _():
        _flash_init_wide(m_ref, acc_ref)
        _diff_split_q(q_ref, qs_ref, heads, tq)

    @pl.when(j < n_past)
    def _():
        _heads_sweep(lambda h: qs_ref[h], lambda h: kc_ref[h], lambda h: vc_ref[h],
                     _mask_bias(p_len, tq, j * tkv, tkv, s_len), 2, m_ref, acc_ref, heads)

    @pl.when(j == n_past)
    def _():
        _heads_sweep(lambda h: qs_ref[h], lambda h: kn_ref[:, h * hd:(h + 1) * hd],
                     lambda h: vn_ref[:, h * hd:(h + 1) * hd],
                     _mask_bias(p_len, tq, p_len, kn_ref.shape[0], s_len), 2, m_ref, acc_ref, heads)
        _diff_finish(lam_ref, g_ref, o_ref, acc_ref, heads, tq, out_scale)


def diff_attention_past(q, kc, vc, kn, vn, l, lam, g_sub, batch, heads, t_len, out_scale):
    hd = 2 * DIFF_D
    w = heads * hd
    p_len = kc.shape[3]
    tn = kn.shape[0] // batch
    tkv = _past_tile(p_len, 1024)
    n_past = p_len // tkv
    cache = pl.BlockSpec((None, None, heads, tkv, hd), lambda b, j: (l, b, 0, jnp.minimum(j, n_past - 1), 0))
    new = pl.BlockSpec((tn, w), lambda b, j: (b, 0))
    qmap = lambda b, j: (b, 0)
    return pl.pallas_call(
        functools.partial(_diff_past_kernel, heads=heads, tq=t_len, tkv=tkv, p_len=p_len, s_len=p_len + t_len,
                          out_scale=out_scale),
        grid=(batch, n_past + 1),
        in_specs=[pl.BlockSpec(memory_space=pltpu.SMEM), pl.BlockSpec((t_len, w), qmap), cache, cache, new, new,
                  pl.BlockSpec((1, hd), lambda b, j: (0, 0))],
        out_specs=pl.BlockSpec((t_len, w), qmap),
        out_shape=jax.ShapeDtypeStruct((batch * t_len, w), BF16),
        scratch_shapes=[pltpu.VMEM((heads, 2 * t_len, hd), BF16)] + _flash_scratch_wide(heads, 2 * t_len, hd),
        compiler_params=_cparams("parallel", "arbitrary"), name="diff_attention_past",
    )(lam.reshape(1, 1), q, kc, vc, kn, vn, g_sub.reshape(1, hd))


MLA_KEY_W = 2 * MLA_NOPE


def _mla_kernel(q_ref, k_ref, v_ref, o_ref, m_ref, acc_ref, *, heads, tq, tkv, s_len):
    i, j = pl.program_id(1), pl.program_id(2)
    q_of = lambda h: q_ref[:, h * MLA_KEY_W:(h + 1) * MLA_KEY_W]
    k_of = lambda h: k_ref[:, h * MLA_KEY_W:(h + 1) * MLA_KEY_W]
    v_of = lambda h: v_ref[:, h * MLA_NOPE:(h + 1) * MLA_NOPE]
    full = _full_kv_blocks(i, tq, tkv, 0, s_len)

    @pl.when(j == 0)
    def _():
        _flash_init_wide(m_ref, acc_ref)

    @pl.when(j < full)
    def _():
        _heads_sweep(q_of, k_of, v_of, None, 1, m_ref, acc_ref, heads)

    @pl.when((j >= full) & (j <= _last_kv_block(i, tq, tkv, 0, s_len)))
    def _():
        _heads_sweep(q_of, k_of, v_of, _mask_bias(i * tq, tq, j * tkv, tkv, s_len), 1, m_ref, acc_ref, heads)

    @pl.when(j == pl.num_programs(2) - 1)
    def _():
        for h in range(heads):
            o_ref[:, h * MLA_NOPE:(h + 1) * MLA_NOPE] = _flash_result_wide(acc_ref.at[h]).astype(o_ref.dtype)


def mla_attention(q, kv_up, batch, heads, t_len, tq, tkv):
    kw, vw = heads * MLA_KEY_W, heads * MLA_NOPE
    nq, nkv = t_len // tq, t_len // tkv
    last = functools.partial(_last_kv_block, tq=tq, tkv=tkv, q0=0, s_len=t_len)
    kvrow = lambda b, i, j: b * nkv + jnp.minimum(j, last(i))
    return pl.pallas_call(
        functools.partial(_mla_kernel, heads=heads, tq=tq, tkv=tkv, s_len=t_len),
        grid=(batch, nq, nkv),
        in_specs=[pl.BlockSpec((tq, kw), lambda b, i, j: (b * nq + i, 0)),
                  pl.BlockSpec((tkv, kw), lambda b, i, j: (kvrow(b, i, j), 0)),
                  pl.BlockSpec((tkv, vw), lambda b, i, j: (kvrow(b, i, j), kw // vw))],
        out_specs=pl.BlockSpec((tq, vw), lambda b, i, j: (b * nq + i, 0)),
        out_shape=jax.ShapeDtypeStruct((batch * t_len, vw), BF16),
        scratch_shapes=_flash_scratch_wide(heads, tq, MLA_NOPE),
        compiler_params=_cparams("parallel", "parallel", "arbitrary"), name="mla_attention",
    )(q, kv_up, kv_up)


def _mla_past_kernel(q_ref, wuk_ref, ckvc_ref, kpec_ref, ckvn_ref, kpen_ref, wuv_ref,
                     o_ref, qlat_ref, qpe_ref, m_ref, l_ref, acc_ref, *, heads, tq, tkv, p_len, s_len):
    j = pl.program_id(1)
    n_past = pl.num_programs(1) - 1

    @pl.when(j == 0)
    def _():
        _flash_init(m_ref, l_ref, acc_ref)
        for h in range(heads):
            lo = h * MLA_KEY_W
            qlat = jnp.dot(q_ref[:, lo:lo + MLA_NOPE], wuk_ref[h], preferred_element_type=F32)
            qlat_ref[h * tq:(h + 1) * tq, :] = qlat.astype(BF16)
            qpe_ref[h * tq:(h + 1) * tq, :] = q_ref[:, lo + MLA_NOPE:lo + MLA_NOPE + MLA_ROPE]

    def sweep(ckv, kpe, k_start):
        s = (lax.dot_general(qlat_ref[...], ckv, _NT, preferred_element_type=F32)
             + lax.dot_general(qpe_ref[...], kpe, _NT, preferred_element_type=F32))
        s = _add_bias(s, _mask_bias(p_len, tq, k_start, ckv.shape[0], s_len), heads)
        _flash_update(s, ckv, m_ref, l_ref, acc_ref)

    @pl.when(j < n_past)
    def _():
        sweep(ckvc_ref[...].astype(BF16), kpec_ref[...].astype(BF16), j * tkv)

    @pl.when(j == n_past)
    def _():
        sweep(ckvn_ref[...], kpen_ref[...], p_len)
        v_w = wuv_ref.shape[2]
        o_lat = _flash_result(l_ref, acc_ref).astype(BF16)
        for h in range(heads):
            o_h = jnp.dot(o_lat[h * tq:(h + 1) * tq], wuv_ref[h], preferred_element_type=F32)
            o_ref[:, h * v_w:(h + 1) * v_w] = o_h.astype(o_ref.dtype)


def mla_attention_past(q, wuk_t, wuv, l, ckv_c, kpe_c, ckv_n, kpe_n, batch, t_len):
    _, heads, _, rank = wuk_t.shape
    v_w = wuv.shape[3]
    p_len = ckv_c.shape[2]
    tn = ckv_n.shape[0] // batch
    tkv = _past_tile(p_len, 1024)
    n_past = p_len // tkv
    rows = heads * t_len
    layer4 = lambda a: pl.BlockSpec((None,) + a.shape[1:], lambda b, j: (l, 0, 0, 0))
    cache = lambda w: pl.BlockSpec((None, None, tkv, w), lambda b, j: (l, b, jnp.minimum(j, n_past - 1), 0))
    new = lambda w: pl.BlockSpec((tn, w), lambda b, j: (b, 0))
    return pl.pallas_call(
        functools.partial(_mla_past_kernel, heads=heads, tq=t_len, tkv=tkv, p_len=p_len, s_len=p_len + t_len),
        grid=(batch, n_past + 1),
        in_specs=[pl.BlockSpec((t_len, q.shape[1]), lambda b, j: (b, 0)), layer4(wuk_t),
                  cache(rank), cache(MLA_ROPE), new(rank), new(MLA_ROPE), layer4(wuv)],
        out_specs=pl.BlockSpec((t_len, heads * v_w), lambda b, j: (b, 0)),
        out_shape=jax.ShapeDtypeStruct((batch * t_len, heads * v_w), BF16),
        scratch_shapes=[pltpu.VMEM((rows, rank), BF16), pltpu.VMEM((rows, MLA_ROPE), BF16)]
                       + _flash_scratch(None, rows, rank),
        compiler_params=_cparams("parallel", "arbitrary"), name="mla_attention_past",
    )(q, wuk_t, ckv_c, kpe_c, ckv_n, kpe_n, wuv)


def _cross_block_kernel(y_ref, x_ref, g_mix_ref, g_pre_ref, wq_ref, k_ref, v_ref, wo_ref, g_post_ref, g_next_ref,
                        xo_ref, ho_ref, *, heads, hd):
    x1 = x_ref[...] + _rms(y_ref[...], g_mix_ref[...])
    q = jnp.dot(_rms(x1, g_pre_ref[...]).astype(BF16), wq_ref[...], preferred_element_type=F32).astype(BF16)
    outs = []
    for h in range(heads):
        sl = slice(h * hd, (h + 1) * hd)
        s = lax.dot_general(q[:, sl], k_ref[:, sl], _NT, preferred_element_type=F32) * (hd ** -0.5)
        p = jnp.exp(s - jnp.max(s, axis=-1, keepdims=True))
        o = jnp.dot(p.astype(BF16), v_ref[:, sl], preferred_element_type=F32)
        outs.append((o / jnp.sum(p, axis=-1, keepdims=True)).astype(BF16))
    y2 = jnp.dot(jnp.concatenate(outs, axis=1), wo_ref[...], preferred_element_type=F32)
    x2 = x1 + _rms(y2, g_post_ref[...])
    xo_ref[...] = x2
    ho_ref[...] = _rms(x2, g_next_ref[...]).astype(ho_ref.dtype)


def cross_block(y_mix, x, g_mix_post, g_pre, wq, mk, mv, wo, l, g_post, g_next, batch, t_len, mem_len, heads, tm):
    m, d = x.shape
    w = wq.shape[2]
    nq = t_len // tm
    row = pl.BlockSpec((tm, d), lambda b, i: (b * nq + i, 0))
    vec = pl.BlockSpec((1, d), lambda b, i: (0, 0))
    kv = pl.BlockSpec((mem_len, w), lambda b, i: (b, 0))
    once = pl.Buffered(1)
    return pl.pallas_call(
        functools.partial(_cross_block_kernel, heads=heads, hd=w // heads),
        grid=(batch, nq),
        in_specs=[row, row, vec, vec,
                  pl.BlockSpec((None, d, w), lambda b, i: (l, 0, 0), pipeline_mode=once), kv, kv,
                  pl.BlockSpec((None, w, d), lambda b, i: (l, 0, 0), pipeline_mode=once), vec, vec],
        out_specs=[row, row],
        out_shape=[jax.ShapeDtypeStruct((m, d), F32), jax.ShapeDtypeStruct((m, d), BF16)],
        compiler_params=_cparams("parallel", "parallel"), name="cross_block",
    )(y_mix, x, g_mix_post.reshape(1, d), g_pre.reshape(1, d), wq, mk, mv, wo,
      g_post.reshape(1, d), g_next.reshape(1, d))


def _rope_tables(pos):
    half = DIFF_D // 2
    inv = ROPE_THETA ** (-jnp.arange(half, dtype=F32) / half)
    ang = pos.astype(F32)[:, None] * inv[None, :]
    cos, sin, zero = jnp.cos(ang), jnp.sin(ang), jnp.zeros_like(ang)
    reps = LANES // DIFF_D
    return (jnp.tile(cos, (1, 2 * reps)), jnp.tile(jnp.concatenate([-sin, zero], 1), (1, reps)),
            jnp.tile(jnp.concatenate([zero, sin], 1), (1, reps)))


def _round_up(a, b):
    return -(-a // b) * b


def _kv_up_weight(w_uk, w_uv):
    depth, rank, heads, nope = w_uk.shape
    k_rows = jnp.concatenate([w_uk, jnp.zeros_like(w_uk)], axis=3).reshape(depth, rank, heads * MLA_KEY_W)
    slot = jnp.concatenate([jnp.zeros((MLA_ROPE, nope), F32), jnp.eye(MLA_ROPE, dtype=F32),
                            jnp.zeros((MLA_ROPE, MLA_KEY_W - nope - MLA_ROPE), F32)], axis=1)
    rope_rows = jnp.broadcast_to(jnp.tile(slot, (1, heads)), (depth, MLA_ROPE, heads * MLA_KEY_W))
    pad = LANES - MLA_ROPE
    w_k = jnp.concatenate([k_rows, rope_rows, jnp.zeros((depth, pad, heads * MLA_KEY_W), F32)], axis=1)
    w_v = jnp.concatenate([w_uv.reshape(depth, rank, -1), jnp.zeros((depth, LANES, heads * w_uv.shape[3]), F32)],
                          axis=1)
    return jnp.concatenate([w_k, w_v], axis=2).astype(BF16)


def _pad_rows(a, batch, rows):
    t_len = a.shape[0] // batch
    a = jnp.pad(a.reshape(batch, t_len, a.shape[1]), ((0, 0), (0, rows - t_len), (0, 0)))
    return a.reshape(batch * rows, a.shape[2])


def _mixer_and_memory(wts, l, x, h, batch, t_len, pos0, tabs, mem_k, mem_v, past, tiles, state):
    tm, tt, tq_diff, tkv_diff, tq_mla, tkv_mla, tq_x = tiles
    lam_init = 0.8 - 0.6 * math.exp(-0.3 * l)
    pool_w, diff_w, q_rank, kv_rank = wts["dims"]
    heads_d = diff_w // (2 * DIFF_D)
    heads_m = wts["w_uk_t"].shape[1]

    (u, qd, kd16, vd16, cqn, ckv16, kpe16, kc16), state = input_projection(
        h, wts["w_in"], tabs, wts["g_mla_q"][l], wts["g_mla_kv"][l], wts["dims"],
        min(tm, 512), tm, l, wts["depth"], state)

    keep = POOL_HALO - 1
    if past is None:
        prev = jnp.zeros((batch, POOL_HALO, pool_w), F32)
    else:
        prev = jnp.concatenate([jnp.zeros((batch, 1, pool_w), F32), past["pool"]], axis=1)
    u3 = u.reshape(batch, t_len, pool_w)
    if t_len >= keep:
        new_pool = u3[:, t_len - keep:]
    else:
        new_pool = jnp.concatenate([prev[:, 1 + t_len:], u3], axis=1)
    y_pool = pool_mix(u, prev, wts["w_pool"], l, wts["pool_scale"][l], batch, t_len, pos0, tt)

    lq = wts["diff_lambda"][l]
    lam = jnp.exp(jnp.sum(lq[0] * lq[1])) - jnp.exp(jnp.sum(lq[2] * lq[3])) + lam_init
    q = mla_query_projection(cqn, wts["w_uq"], tabs, l, tm, 1024)
    if past is None:
        o_diff = diff_attention(qd, kd16, vd16, lam, wts["g_diff_sub"][l], batch, heads_d, t_len,
                                1.0 - lam_init, tq_diff, tkv_diff)
        kv_up = matmul(kc16, wts["w_kv_up"], l, BF16, tm, heads_m * MLA_NOPE)
        o_mla = mla_attention(q, kv_up, batch, heads_m, t_len, tq_mla, tkv_mla)
    else:
        tn = _round_up(t_len, LANES)
        o_diff = diff_attention_past(qd, past["diff_k"], past["diff_v"], _pad_rows(kd16, batch, tn),
                                     _pad_rows(vd16, batch, tn), l, lam, wts["g_diff_sub"][l], batch, heads_d,
                                     t_len, 1.0 - lam_init)
        o_mla = mla_attention_past(q, wts["w_uk_t"], wts["w_uv"], l, past["ckv"], past["kpe"],
                                   _pad_rows(ckv16, batch, tn), _pad_rows(kpe16, batch, tn), batch, t_len)

    y = matmul_cat([y_pool, o_diff, o_mla], wts["w_out"], l, F32, tm, 1024)
    x, h = cross_block(y, x, wts["g_mix_post"][l], wts["g_x_pre"][l], wts["w_mem_q"], mem_k, mem_v,
                       wts["w_mem_o"], l, wts["g_x_post"][l], wts["g_ff_pre"][l], batch, t_len,
                       mem_k.shape[0] // batch, wts["mem_heads"], tq_x)
    return x, h, state, new_pool


def _ffn_pair(wts, l, x, h, xs, hs, g_next, tiles, tiles_s):
    a, a_s = matmul_swiglu_pair(h, hs, wts["w_gate"], wts["w_up"], l, 2 * tiles[0], 256)
    y, y_s = matmul_pair(a, a_s, wts["w_down"], l, F32, 512, 512)
    x, h = post_res(y, x, wts["g_ff_post"][l], g_next, tiles[1])
    xs, hs = post_res(y_s, xs, wts["g_ff_post"][l], g_next, tiles_s[1])
    return x, h, xs, hs


def kernel(x_prompt, x_sample, cache_diff_k, cache_diff_v, cache_mla_ckv, cache_mla_kpe, cache_pool, cache_mem_k, cache_mem_v, mem_prompt, g_mix_pre, w_in, w_pool, pool_scale, diff_lambda, g_diff_sub, g_mla_q, w_mla_uq, w_mla_uk, w_mla_uv, g_mla_kv, w_out, g_mix_post, g_mem, w_mem_k, w_mem_v, w_mem_q, w_mem_o, g_x_pre, g_x_post, g_ff_pre, w_gate, w_up, w_down, g_ff_post):
    depth = w_in.shape[0]
    bp, tp, d = x_prompt.shape
    bs, ts, _ = x_sample.shape
    past_len = cache_mla_ckv.shape[2]
    pool_w = cache_pool.shape[3]
    heads_d, diff_w = cache_diff_k.shape[3], cache_diff_k.shape[3] * cache_diff_k.shape[4]
    q_rank, kv_rank = g_mla_q.shape[1], g_mla_kv.shape[1]
    mla_heads = w_mla_uk.shape[2]
    mem_len, mem_heads, mem_hd = cache_mem_k.shape[2:]
    mem_w = mem_heads * mem_hd

    uq = w_mla_uq.reshape(depth, q_rank, mla_heads, MLA_NOPE + MLA_ROPE)
    wts = {
        "depth": depth, "dims": (pool_w, diff_w, q_rank, kv_rank), "mem_heads": mem_heads,
        "w_in": w_in.astype(BF16),
        "w_pool": w_pool.astype(BF16), "pool_scale": pool_scale, "diff_lambda": diff_lambda,
        "g_diff_sub": g_diff_sub, "g_mla_q": g_mla_q, "g_mla_kv": g_mla_kv,
        "w_uq": jnp.pad(uq, ((0, 0), (0, 0), (0, 0), (0, MLA_KEY_W - MLA_NOPE - MLA_ROPE))).reshape(
            depth, q_rank, mla_heads * MLA_KEY_W).astype(BF16),
        "w_uk_t": jnp.transpose(w_mla_uk, (0, 2, 3, 1)).astype(BF16),
        "w_uv": jnp.transpose(w_mla_uv, (0, 2, 1, 3)).astype(BF16),
        "w_kv_up": _kv_up_weight(w_mla_uk, w_mla_uv),
        "w_out": w_out.astype(BF16), "g_mix_post": g_mix_post,
        "w_mem_q": w_mem_q.astype(BF16), "w_mem_o": w_mem_o.astype(BF16),
        "g_x_pre": g_x_pre, "g_x_post": g_x_post, "g_ff_pre": g_ff_pre,
        "w_gate": w_gate, "w_up": w_up, "w_down": w_down.astype(BF16),
        "g_ff_post": g_ff_post,
    }
    w_mem_kv = jnp.concatenate([w_mem_k, w_mem_v], axis=2).astype(BF16)

    tabs_p = _rope_tables(jnp.arange(tp))
    tabs_s = tuple(jnp.tile(t, (bs, 1)) for t in _rope_tables(past_len + jnp.arange(ts)))
    tiles_p = (1024, 256, 512, 512, 512, 512, 256)
    tiles_s = (bs * ts, ts, ts, 0, ts, 0, ts)

    xp = x_prompt.reshape(bp * tp, d)
    xs = x_sample.reshape(bs * ts, d)
    hp = rmsnorm_bf16(xp, g_mix_pre[0], tiles_p[1])
    hs = rmsnorm_bf16(xs, g_mix_pre[0], tiles_s[1])
    mem_n = mem_prompt.reshape(bp * mem_len, d)
    mem_k_s = cache_mem_k.reshape(depth, bs * mem_len, mem_w).astype(BF16)
    mem_v_s = cache_mem_v.reshape(depth, bs * mem_len, mem_w).astype(BF16)
    past = {"diff_k": jnp.transpose(cache_diff_k, (0, 1, 3, 2, 4)).astype(BF16),
            "diff_v": jnp.transpose(cache_diff_v, (0, 1, 3, 2, 4)).astype(BF16),
            "ckv": cache_mla_ckv, "kpe": cache_mla_kpe}

    state_p = state_s = None
    pools_p, pools_s, mem_ks, mem_vs = [], [], [], []
    for l in range(depth):
        g_next = g_mix_pre[l + 1] if l + 1 < depth else None
        m = rmsnorm_bf16(mem_n, g_mem[l], 256)
        mkv = matmul(m, w_mem_kv, l, F32, 1024, 1024)
        mk, mv = mkv[:, :mem_w], mkv[:, mem_w:]
        mem_ks.append(mk.reshape(bp, mem_len, mem_heads, mem_hd))
        mem_vs.append(mv.reshape(bp, mem_len, mem_heads, mem_hd))
        xp, hp, state_p, pool_p = _mixer_and_memory(wts, l, xp, hp, bp, tp, 0, tabs_p, mk.astype(BF16),
                                                    mv.astype(BF16), None, tiles_p, state_p)
        pools_p.append(pool_p)
        xs, hs, state_s, pool_s = _mixer_and_memory(wts, l, xs, hs, bs, ts, past_len, tabs_s, mem_k_s[l],
                                                    mem_v_s[l], dict(past, pool=cache_pool[l]), tiles_s, state_s)
        pools_s.append(pool_s)
        xp, hp, xs, hs = _ffn_pair(wts, l, xp, hp, xs, hs, g_next, tiles_p, tiles_s)

    hd = 2 * DIFF_D
    shape_p = lambda a, tail: a.reshape((depth, bp, tp) + tail)
    shape_s = lambda a, tail: a.reshape((depth, bs, ts) + tail)
    return (xp.reshape(bp, tp, d), xs.reshape(bs, ts, d),
            shape_p(state_p[0], (heads_d, hd)), shape_p(state_p[1], (heads_d, hd)),
            shape_p(state_p[2], (kv_rank,)), shape_p(state_p[3], (MLA_ROPE,)),
            jnp.stack(pools_p), jnp.stack(mem_ks), jnp.stack(mem_vs),
            shape_s(state_s[0], (heads_d, hd)), shape_s(state_s[1], (heads_d, hd)),
            shape_s(state_s[2], (kv_rank,)), shape_s(state_s[3], (MLA_ROPE,)),
            jnp.stack(pools_s))
```

```python
import functools
import math

import jax
import jax.numpy as jnp
from jax import lax
from jax.experimental import pallas as pl
from jax.experimental.pallas import tpu as pltpu

F32 = jnp.float32
BF16 = jnp.bfloat16

EPS = 1e-6
CHUNK = 64
ROPE_THETA = 10000.0
POOL_WINDOWS = (2, 4, 8, 16)
POOL_HALO = 16
DIFF_D = 64
MLA_NOPE = 128
MLA_ROPE = 64
LANES = 128
MIB = 1024 * 1024
VMEM_LIMIT = 56 * MIB

_NT = (((1,), (1,)), ((), ()))


def _cparams(*sem):
    return pltpu.CompilerParams(dimension_semantics=sem, vmem_limit_bytes=VMEM_LIMIT)


def _rms(x, g):
    return x * lax.rsqrt(jnp.mean(x * x, axis=-1, keepdims=True) + EPS) * g


def _rmsnorm_kernel(x_ref, g_ref, o_ref):
    o_ref[...] = _rms(x_ref[...], g_ref[...]).astype(o_ref.dtype)


def rmsnorm_bf16(x, g, tm):
    m, d = x.shape
    return pl.pallas_call(
        _rmsnorm_kernel,
        grid=(m // tm,),
        in_specs=[pl.BlockSpec((tm, d), lambda i: (i, 0)), pl.BlockSpec((1, d), lambda i: (0, 0))],
        out_specs=pl.BlockSpec((tm, d), lambda i: (i, 0)),
        out_shape=jax.ShapeDtypeStruct((m, d), BF16),
        compiler_params=_cparams("parallel"),
        name="rmsnorm",
    )(x, g.reshape(1, d))


def _post_res_kernel(y_ref, x_ref, gp_ref, gn_ref, xo_ref, ho_ref):
    x = x_ref[...] + _rms(y_ref[...], gp_ref[...])
    xo_ref[...] = x
    ho_ref[...] = _rms(x, gn_ref[...]).astype(ho_ref.dtype)


def _post_res_last_kernel(y_ref, x_ref, gp_ref, xo_ref):
    xo_ref[...] = x_ref[...] + _rms(y_ref[...], gp_ref[...])


def post_res(y, x, g_post, g_next, tm):
    m, d = x.shape
    row = pl.BlockSpec((tm, d), lambda i: (i, 0))
    vec = pl.BlockSpec((1, d), lambda i: (0, 0))
    if g_next is None:
        return pl.pallas_call(
            _post_res_last_kernel, grid=(m // tm,), in_specs=[row, row, vec], out_specs=row,
            out_shape=jax.ShapeDtypeStruct((m, d), F32), compiler_params=_cparams("parallel"),
            name="post_res_last",
        )(y, x, g_post.reshape(1, d)), None
    return pl.pallas_call(
        _post_res_kernel, grid=(m // tm,), in_specs=[row, row, vec, vec], out_specs=[row, row],
        out_shape=[jax.ShapeDtypeStruct((m, d), F32), jax.ShapeDtypeStruct((m, d), BF16)],
        compiler_params=_cparams("parallel"), name="post_res",
    )(y, x, g_post.reshape(1, d), g_next.reshape(1, d))


def _mm_kernel(x_ref, w_ref, o_ref):
    o_ref[...] = jnp.dot(x_ref[...], w_ref[...], preferred_element_type=F32).astype(o_ref.dtype)


def matmul(x, w, l, out_dtype, tm, tn):
    m, k = x.shape
    n = w.shape[2]
    tm, tn = min(tm, m), min(tn, n)
    return pl.pallas_call(
        _mm_kernel, grid=(n // tn, m // tm),
        in_specs=[pl.BlockSpec((tm, k), lambda j, i: (i, 0)), pl.BlockSpec((None, k, tn), lambda j, i: (l, 0, j))],
        out_specs=pl.BlockSpec((tm, tn), lambda j, i: (i, j)),
        out_shape=jax.ShapeDtypeStruct((m, n), out_dtype),
        compiler_params=_cparams("parallel", "parallel"), name="matmul",
    )(x, w)


def _mm_cat_kernel(*refs):
    n_in = (len(refs) - 1) // 2
    acc = jnp.dot(refs[0][...], refs[n_in][...], preferred_element_type=F32)
    for a in range(1, n_in):
        acc += jnp.dot(refs[a][...], refs[n_in + a][...], preferred_element_type=F32)
    refs[-1][...] = acc.astype(refs[-1].dtype)


def matmul_cat(xs, w, l, out_dtype, tm, tn):
    m = xs[0].shape[0]
    n = w.shape[2]
    tm, tn = min(tm, m), min(tn, n)
    x_specs, w_specs, off = [], [], 0
    for x in xs:
        k = x.shape[1]
        assert off % k == 0, "each input's row band of w must start on a multiple of its own width"
        x_specs.append(pl.BlockSpec((tm, k), lambda j, i: (i, 0)))
        w_specs.append(pl.BlockSpec((None, k, tn), functools.partial(lambda j, i, r: (l, r, j), r=off // k)))
        off += k
    return pl.pallas_call(
        _mm_cat_kernel, grid=(n // tn, m // tm),
        in_specs=x_specs + w_specs,
        out_specs=pl.BlockSpec((tm, tn), lambda j, i: (i, j)),
        out_shape=jax.ShapeDtypeStruct((m, n), out_dtype),
        compiler_params=_cparams("parallel", "parallel"), name="matmul_cat",
    )(*xs, *([w] * len(xs)))


def _swiglu_kernel(x_ref, wg_ref, wu_ref, o_ref):
    x = x_ref[...]
    g = jnp.dot(x, wg_ref[...].astype(BF16), preferred_element_type=F32)
    u = jnp.dot(x, wu_ref[...].astype(BF16), preferred_element_type=F32)
    o_ref[...] = (g * jax.nn.sigmoid(g) * u).astype(o_ref.dtype)


def matmul_swiglu(x, wg, wu, l, tm, tn):
    m, k = x.shape
    n = wg.shape[2]
    tm = min(tm, m)
    wspec = pl.BlockSpec((None, k, tn), lambda i, j: (l, 0, j))
    return pl.pallas_call(
        _swiglu_kernel, grid=(m // tm, n // tn),
        in_specs=[pl.BlockSpec((tm, k), lambda i, j: (i, 0), pipeline_mode=pl.Buffered(1)), wspec, wspec],
        out_specs=pl.BlockSpec((tm, tn), lambda i, j: (i, j)),
        out_shape=jax.ShapeDtypeStruct((m, n), BF16),
        compiler_params=_cparams("parallel", "parallel"), name="matmul_swiglu",
    )(x, wg, wu)


def _swiglu_pair_kernel(x_ref, wg_ref, wu_ref, xs_ref, o_ref, os_ref):
    wg, wu = wg_ref[...].astype(BF16), wu_ref[...].astype(BF16)

    def gated(x):
        g = jnp.dot(x, wg, preferred_element_type=F32)
        u = jnp.dot(x, wu, preferred_element_type=F32)
        return (g * jax.nn.sigmoid(g) * u).astype(BF16)

    o_ref[...] = gated(x_ref[...])

    @pl.when(pl.program_id(0) == 0)
    def _():
        os_ref[...] = gated(xs_ref[...])


def matmul_swiglu_pair(x, xs, wg, wu, l, tm, tn):
    m, k = x.shape
    ms = xs.shape[0]
    n = wg.shape[2]
    tm = min(tm, m)
    nj = n // tn
    wspec = pl.BlockSpec((None, k, tn), lambda i, j: (l, 0, j))
    once = pl.Buffered(1)
    return pl.pallas_call(
        _swiglu_pair_kernel, grid=(m // tm, nj),
        in_specs=[pl.BlockSpec((tm, k), lambda i, j: (i, 0), pipeline_mode=once), wspec, wspec,
                  pl.BlockSpec((ms, k), lambda i, j: (0, 0), pipeline_mode=once)],
        out_specs=[pl.BlockSpec((tm, tn), lambda i, j: (i, j)),
                   pl.BlockSpec((ms, tn), lambda i, j: (0, jnp.where(i == 0, j, nj - 1)))],
        out_shape=[jax.ShapeDtypeStruct((m, n), BF16), jax.ShapeDtypeStruct((ms, n), BF16)],
        compiler_params=_cparams("arbitrary", "arbitrary"), name="matmul_swiglu_pair",
    )(x, wg, wu, xs)


def _mm_pair_kernel(x_ref, w_ref, xs_ref, o_ref, os_ref):
    o_ref[...] = jnp.dot(x_ref[...], w_ref[...], preferred_element_type=F32).astype(o_ref.dtype)

    @pl.when(pl.program_id(1) == 0)
    def _():
        os_ref[...] = jnp.dot(xs_ref[...], w_ref[...], preferred_element_type=F32).astype(os_ref.dtype)


def matmul_pair(x, xs, w, l, out_dtype, tm, tn):
    m, k = x.shape
    ms = xs.shape[0]
    n = w.shape[2]
    tm, tn = min(tm, m), min(tn, n)
    return pl.pallas_call(
        _mm_pair_kernel, grid=(n // tn, m // tm),
        in_specs=[pl.BlockSpec((tm, k), lambda j, i: (i, 0)), pl.BlockSpec((None, k, tn), lambda j, i: (l, 0, j)),
                  pl.BlockSpec((ms, k), lambda j, i: (0, 0), pipeline_mode=pl.Buffered(1))],
        out_specs=[pl.BlockSpec((tm, tn), lambda j, i: (i, j)), pl.BlockSpec((ms, tn), lambda j, i: (0, j))],
        out_shape=[jax.ShapeDtypeStruct((m, n), out_dtype), jax.ShapeDtypeStruct((ms, n), out_dtype)],
        compiler_params=_cparams("arbitrary", "arbitrary"), name="matmul_pair",
    )(x, w, xs)


def _q_proj_kernel(x_ref, w_ref, cos_ref, sa_ref, sb_ref, o_ref, *, scale):
    acc = jnp.dot(x_ref[...], w_ref[...], preferred_element_type=F32)
    cos, sa, sb = cos_ref[...], sa_ref[...], sb_ref[...]
    for s in range(acc.shape[1] // MLA_KEY_W):
        lo, mid, hi = s * MLA_KEY_W, s * MLA_KEY_W + MLA_NOPE, (s + 1) * MLA_KEY_W
        o_ref[:, lo:mid] = (acc[:, lo:mid] * scale).astype(o_ref.dtype)
        o_ref[:, mid:hi] = (_rope_chunk(acc[:, mid:hi], cos, sa, sb) * scale).astype(o_ref.dtype)


def mla_query_projection(cqn, w_slots, tabs, l, tm, tn):
    m, k = cqn.shape
    n = w_slots.shape[2]
    t_tab = tabs[0].shape[0]
    tm, tn = min(tm, t_tab), min(tn, n)
    nt = t_tab // tm
    tab = pl.BlockSpec((tm, LANES), lambda j, i: (i % nt, 0))
    return pl.pallas_call(
        functools.partial(_q_proj_kernel, scale=(MLA_NOPE + MLA_ROPE) ** -0.5),
        grid=(n // tn, m // tm),
        in_specs=[pl.BlockSpec((tm, k), lambda j, i: (i, 0)), pl.BlockSpec((None, k, tn), lambda j, i: (l, 0, j)),
                  tab, tab, tab],
        out_specs=pl.BlockSpec((tm, tn), lambda j, i: (i, j)),
        out_shape=jax.ShapeDtypeStruct((m, n), BF16),
        compiler_params=_cparams("parallel", "parallel"), name="mla_query_projection",
    )(cqn, w_slots, *tabs)


def _rope_chunk(x, cos, sa, sb):
    return x * cos + pltpu.roll(x, LANES - 32, 1) * sa + pltpu.roll(x, 32, 1) * sb


def _rope_cols(x, cos, sa, sb):
    return jnp.concatenate([_rope_chunk(x[:, c * LANES:(c + 1) * LANES], cos, sa, sb)
                            for c in range(x.shape[1] // LANES)], axis=1)


_IN_GROUPS = 5
_IN_INPUTS = 6


def _in_proj_kernel(*refs, q_scale):
    x_ref, w_ref, cos_ref, sa_ref, sb_ref, gq_ref = refs[:_IN_INPUTS]
    u_ref, qd_ref, kd32_ref, kd16_ref, vd32_ref, vd16_ref, cq_ref, acc_a, acc_b = refs[-9:]
    accs = (acc_a, acc_b)
    j = pl.program_id(1)
    rope = lambda a: _rope_cols(a, cos_ref[...], sa_ref[...], sb_ref[...])

    def finish(g, acc):
        if g == 0:
            u_ref[...] = acc
        elif g == 1:
            qd_ref[...] = (rope(acc) * q_scale).astype(BF16)
        elif g == 2:
            k = rope(acc)
            kd32_ref[...] = k
            kd16_ref[...] = k.astype(BF16)
        elif g == 3:
            vd32_ref[...] = acc
            vd16_ref[...] = acc.astype(BF16)
        else:
            cq_ref[...] = _rms(acc, gq_ref[...]).astype(BF16)

    for g in range(_IN_GROUPS + 1):
        @pl.when(j == g)
        def _(g=g):
            if g < _IN_GROUPS:
                accs[g % 2][...] = jnp.dot(x_ref[...], w_ref[...], preferred_element_type=F32)
            if g > 0:
                finish(g - 1, accs[(g - 1) % 2][...])


_LAT_INPUTS = 6


def _in_proj_latent_kernel(*refs):
    x_ref, w_ref, cos_ref, sa_ref, sb_ref, gkv_ref = refs[:_LAT_INPUTS]
    ckv32_ref, ckv16_ref, kpe32_ref, kpe16_ref, kc16_ref = refs[-5:]
    rank = ckv32_ref.shape[1]
    acc = jnp.dot(x_ref[...], w_ref[...], preferred_element_type=F32)
    ckv = _rms(acc[:, 0:rank], gkv_ref[...])
    ckv32_ref[...] = ckv
    ckv16_ref[...] = ckv.astype(BF16)
    lane = lax.broadcasted_iota(jnp.int32, (acc.shape[0], LANES), 1)
    kpe_raw = jnp.where(lane < MLA_ROPE, acc[:, rank:rank + LANES], 0.0)
    kpe_chunk = _rope_chunk(kpe_raw, cos_ref[...], sa_ref[...], sb_ref[...])
    kpe = kpe_chunk[:, :MLA_ROPE]
    kpe32_ref[...] = kpe
    kpe16_ref[...] = kpe.astype(BF16)
    kc16_ref[:, 0:rank] = ckv.astype(BF16)
    kc16_ref[:, rank:rank + LANES] = kpe_chunk.astype(BF16)


def input_projection(h, w_in, tabs, g_q, g_kv, dims, tm, tm_lat, l, depth, state):
    m, d = h.shape
    pool_w, diff_w, q_rank, kv_rank = dims
    gw = diff_w
    lw = kv_rank + LANES
    assert pool_w == gw and q_rank == gw and w_in.shape[2] == _IN_GROUPS * gw + kv_rank + MLA_ROPE
    assert (_IN_GROUPS * gw) % lw == 0, "the latent columns must start on a multiple of their block width"
    t_tab = tabs[0].shape[0]
    tm, tm_lat = min(tm, t_tab), min(tm_lat, t_tab)
    state = (None, None) if state is None else ((state[0], state[1]), (state[2], state[3]))
    sds = lambda w, dt: jax.ShapeDtypeStruct((m, w), dt)
    stk = lambda w: jax.ShapeDtypeStruct((depth, m, w), F32)
    alias = pl.BlockSpec(memory_space=pl.ANY)

    nt = t_tab // tm
    row = lambda w: pl.BlockSpec((tm, w), lambda i, j: (i, 0))
    lay = lambda w: pl.BlockSpec((None, tm, w), lambda i, j: (l, i, 0))
    tab = pl.BlockSpec((tm, LANES), lambda i, j: (i % nt, 0))
    prev = () if state[0] is None else state[0]
    u, qd, kd32, kd16, vd32, vd16, cqn = pl.pallas_call(
        functools.partial(_in_proj_kernel, q_scale=DIFF_D ** -0.5),
        grid=(m // tm, _IN_GROUPS + 1),
        in_specs=[pl.BlockSpec((tm, d), lambda i, j: (i, 0)),
                  pl.BlockSpec((None, d, gw), lambda i, j: (l, 0, jnp.minimum(j, _IN_GROUPS - 1))),
                  tab, tab, tab, pl.BlockSpec((1, gw), lambda i, j: (0, 0))] + [alias] * len(prev),
        out_specs=[row(gw), row(gw), lay(gw), row(gw), lay(gw), row(gw), row(gw)],
        out_shape=[sds(gw, F32), sds(gw, BF16), stk(gw), sds(gw, BF16), stk(gw), sds(gw, BF16), sds(gw, BF16)],
        scratch_shapes=[pltpu.VMEM((tm, gw), F32), pltpu.VMEM((tm, gw), F32)],
        input_output_aliases={_IN_INPUTS + a: o for a, o in enumerate((2, 4)[:len(prev)])},
        compiler_params=_cparams("parallel", "arbitrary"), name="in_proj",
    )(h, w_in, *tabs, g_q.reshape(1, gw), *prev)

    nt = t_tab // tm_lat
    row = lambda w: pl.BlockSpec((tm_lat, w), lambda i: (i, 0))
    lay = lambda w: pl.BlockSpec((None, tm_lat, w), lambda i: (l, i, 0))
    tab = pl.BlockSpec((tm_lat, LANES), lambda i: (i % nt, 0))
    prev = () if state[1] is None else state[1]
    ckv32, ckv16, kpe32, kpe16, kc16 = pl.pallas_call(
        _in_proj_latent_kernel,
        grid=(m // tm_lat,),
        in_specs=[pl.BlockSpec((tm_lat, d), lambda i: (i, 0)),
                  pl.BlockSpec((None, d, lw), lambda i: (l, 0, _IN_GROUPS * gw // lw)),
                  tab, tab, tab, pl.BlockSpec((1, kv_rank), lambda i: (0, 0))] + [alias] * len(prev),
        out_specs=[lay(kv_rank), row(kv_rank), lay(MLA_ROPE), row(MLA_ROPE), row(lw)],
        out_shape=[stk(kv_rank), sds(kv_rank, BF16), stk(MLA_ROPE), sds(MLA_ROPE, BF16), sds(lw, BF16)],
        input_output_aliases={_LAT_INPUTS + a: o for a, o in enumerate((0, 2)[:len(prev)])},
        compiler_params=_cparams("parallel"), name="in_proj_latent",
    )(h, w_in, *tabs, g_kv.reshape(1, kv_rank), *prev)
    return (u, qd, kd16, vd16, cqn, ckv16, kpe16, kc16), (kd32, vd32, ckv32, kpe32)


def _pool_kernel(z_ref, halo_ref, prev_ref, wp_ref, sc_ref, o_ref, ext_ref, *, tt, pos0):
    i = pl.program_id(1)
    ext_ref[0:POOL_HALO, :] = jnp.where(i == 0, prev_ref[0], halo_ref[...])
    ext_ref[POOL_HALO:POOL_HALO + tt, :] = z_ref[...]
    pos = pos0 + i * tt + lax.broadcasted_iota(jnp.int32, (tt, 1), 0)
    group = wp_ref.shape[1]
    for g, w in enumerate(POOL_WINDOWS):
        cs = slice(g * group, (g + 1) * group)
        s = ext_ref[POOL_HALO:POOL_HALO + tt, cs]
        u = s
        for j in range(1, w):
            s = s + ext_ref[POOL_HALO - j:POOL_HALO - j + tt, cs]
        cnt = jnp.minimum(w, pos + 1).astype(F32)
        d = s / cnt - u
        y = jnp.dot(d.astype(BF16), wp_ref[g], preferred_element_type=F32) * sc_ref[:, cs]
        o_ref[:, cs] = y.astype(o_ref.dtype)


def pool_mix(z, prev, w_pool, l, scale, batch, t_len, pos0, tt):
    m = z.shape[0]
    c = prev.shape[2]
    nt = t_len // tt
    hb = tt // POOL_HALO
    return pl.pallas_call(
        functools.partial(_pool_kernel, tt=tt, pos0=pos0),
        grid=(batch, nt),
        in_specs=[pl.BlockSpec((tt, c), lambda b, i: (b * nt + i, 0)),
                  pl.BlockSpec((POOL_HALO, c), lambda b, i: (jnp.maximum((b * nt + i) * hb - 1, 0), 0)),
                  pl.BlockSpec((1, POOL_HALO, c), lambda b, i: (b, 0, 0)),
                  pl.BlockSpec((None,) + w_pool.shape[1:], lambda b, i: (l, 0, 0, 0)),
                  pl.BlockSpec((1, c), lambda b, i: (0, 0))],
        out_specs=pl.BlockSpec((tt, c), lambda b, i: (b * nt + i, 0)),
        out_shape=jax.ShapeDtypeStruct((m, c), BF16),
        scratch_shapes=[pltpu.VMEM((POOL_HALO + tt, c), F32)],
        compiler_params=_cparams("parallel", "parallel"), name="pool_mix",
    )(z, z, prev, w_pool, scale.reshape(1, c))


def _last_kv_block(i, tq, tkv, q0, s_len):
    last_q = q0 + (i + 1) * tq - 1
    visible = jnp.minimum((last_q // CHUNK + 1) * CHUNK, s_len)
    return (visible - 1) // tkv


def _mask_bias(q_start, tq, k_start, tkv, s_len):
    qpos = q_start + lax.broadcasted_iota(jnp.int32, (tq, 1), 0)
    kpos = k_start + lax.broadcasted_iota(jnp.int32, (1, tkv), 1)
    shift = CHUNK.bit_length() - 1
    ok = (jnp.right_shift(kpos, shift) <= jnp.right_shift(qpos, shift)) & (kpos < s_len)
    return jnp.where(ok, 0.0, -jnp.inf).astype(F32)


def _add_bias(s, bias, groups):
    rows, tkv = s.shape
    return (s.reshape(groups, rows // groups, tkv) + bias[None]).reshape(rows, tkv)


def _lane_tile(x, width):
    return x if width == LANES else jnp.concatenate([x] * (width // LANES), axis=1)


def _flash_update(s, v, m_ref, l_ref, acc_ref):
    tkv = s.shape[1]
    m_prev = m_ref[...]
    m_new = jnp.maximum(m_prev, jnp.max(s, axis=-1, keepdims=True))
    alpha = jnp.exp(m_prev - m_new)
    p = jnp.exp(s - _lane_tile(m_new, tkv))
    psum = p[:, :LANES]
    for c in range(1, tkv // LANES):
        psum = psum + p[:, c * LANES:(c + 1) * LANES]
    l_ref[...] = alpha * l_ref[...] + psum
    acc_ref[...] = (_lane_tile(alpha, acc_ref.shape[1]) * acc_ref[...]
                    + jnp.dot(p.astype(BF16), v, preferred_element_type=F32))
    m_ref[...] = m_new


def _flash_init(m_ref, l_ref, acc_ref):
    m_ref[...] = jnp.full(m_ref.shape, -jnp.inf, F32)
    l_ref[...] = jnp.zeros(l_ref.shape, F32)
    acc_ref[...] = jnp.zeros(acc_ref.shape, F32)


def _flash_result(l_ref, acc_ref):
    return acc_ref[...] / jnp.sum(l_ref[...], axis=-1, keepdims=True)


def _flash_scratch(groups, rows, dv):
    lead = () if groups is None else (groups,)
    return [pltpu.VMEM(lead + (rows, LANES), F32), pltpu.VMEM(lead + (rows, LANES), F32),
            pltpu.VMEM(lead + (rows, dv), F32)]


def _full_kv_blocks(i, tq, tkv, q0, s_len):
    first_q = q0 + i * tq
    return jnp.minimum((first_q // CHUNK + 1) * CHUNK, s_len) // tkv


def _with_ones(v):
    return jnp.concatenate([v, jnp.ones((v.shape[0], LANES), v.dtype)], axis=1)


def _flash_update_wide(s, v_ones, m_ref, acc_ref):
    tkv = s.shape[1]
    m_prev = m_ref[...]
    m_new = jnp.maximum(m_prev, jnp.max(s, axis=-1, keepdims=True))
    alpha = jnp.exp(m_prev - m_new)
    p = jnp.exp((s - _lane_tile(m_new, tkv)).astype(BF16))
    acc_ref[...] = (_lane_tile(alpha, acc_ref.shape[1]) * acc_ref[...]
                    + jnp.dot(p, v_ones, preferred_element_type=F32))
    m_ref[...] = m_new


def _flash_init_wide(m_ref, acc_ref):
    m_ref[...] = jnp.full(m_ref.shape, -jnp.inf, F32)
    acc_ref[...] = jnp.zeros(acc_ref.shape, F32)


def _flash_result_wide(acc_ref):
    dv = acc_ref.shape[-1] - LANES
    assert dv == LANES
    return acc_ref[:, 0:dv] / acc_ref[:, dv:dv + LANES]


def _flash_scratch_wide(groups, rows, dv):
    return [pltpu.VMEM((groups, rows, LANES), F32), pltpu.VMEM((groups, rows, dv + LANES), F32)]


def _past_tile(p_len, cap):
    tile = max(t for t in range(LANES, cap + 1, LANES) if p_len % t == 0)
    return tile


def _diff_split_q(q_ref, qs_ref, heads, tq):
    hd = 2 * DIFF_D
    lane = lax.broadcasted_iota(jnp.int32, (tq, hd), 1)
    for h in range(heads):
        q = q_ref[:, h * hd:(h + 1) * hd]
        qs_ref[h, 0:tq, :] = jnp.where(lane < DIFF_D, q, jnp.zeros_like(q))
        qs_ref[h, tq:2 * tq, :] = jnp.where(lane >= DIFF_D, q, jnp.zeros_like(q))


def _heads_sweep(q_of, k_of, v_of, bias, groups, m_ref, acc_ref, heads):
    for h in range(heads):
        s = lax.dot_general(q_of(h), k_of(h), _NT, preferred_element_type=F32)
        if bias is not None:
            s = s + bias if groups == 1 else _add_bias(s, bias, groups)
        _flash_update_wide(s, _with_ones(v_of(h)), m_ref.at[h], acc_ref.at[h])


def _diff_finish(lam_ref, g_ref, o_ref, acc_ref, heads, tq, out_scale):
    hd = 2 * DIFF_D
    lam = lam_ref[0, 0]
    for h in range(heads):
        a = _flash_result_wide(acc_ref.at[h])
        o = a[0:tq] - lam * a[tq:2 * tq]
        o_ref[:, h * hd:(h + 1) * hd] = (_rms(o, g_ref[...]) * out_scale).astype(o_ref.dtype)


def _diff_kernel(lam_ref, q_ref, k_ref, v_ref, g_ref, o_ref, qs_ref, m_ref, acc_ref, *,
                 heads, tq, tkv, s_len, out_scale):
    i, j = pl.program_id(1), pl.program_id(2)
    hd = 2 * DIFF_D
    k_of = lambda h: k_ref[:, h * hd:(h + 1) * hd]
    v_of = lambda h: v_ref[:, h * hd:(h + 1) * hd]
    full = _full_kv_blocks(i, tq, tkv, 0, s_len)

    @pl.when(j == 0)
    def _():
        _flash_init_wide(m_ref, acc_ref)
        _diff_split_q(q_ref, qs_ref, heads, tq)

    @pl.when(j < full)
    def _():
        _heads_sweep(lambda h: qs_ref[h], k_of, v_of, None, 2, m_ref, acc_ref, heads)

    @pl.when((j >= full) & (j <= _last_kv_block(i, tq, tkv, 0, s_len)))
    def _():
        _heads_sweep(lambda h: qs_ref[h], k_of, v_of, _mask_bias(i * tq, tq, j * tkv, tkv, s_len), 2,
                     m_ref, acc_ref, heads)

    @pl.when(j == pl.num_programs(2) - 1)
    def _():
        _diff_finish(lam_ref, g_ref, o_ref, acc_ref, heads, tq, out_scale)


def diff_attention(q, k, v, lam, g_sub, batch, heads, t_len, out_scale, tq, tkv):
    hd = 2 * DIFF_D
    w = heads * hd
    nq, nkv = t_len // tq, t_len // tkv
    last = functools.partial(_last_kv_block, tq=tq, tkv=tkv, q0=0, s_len=t_len)
    kvmap = lambda b, i, j: (b * nkv + jnp.minimum(j, last(i)), 0)
    qmap = lambda b, i, j: (b * nq + i, 0)
    return pl.pallas_call(
        functools.partial(_diff_kernel, heads=heads, tq=tq, tkv=tkv, s_len=t_len, out_scale=out_scale),
        grid=(batch, nq, nkv),
        in_specs=[pl.BlockSpec(memory_space=pltpu.SMEM),
                  pl.BlockSpec((tq, w), qmap),
                  pl.BlockSpec((tkv, w), kvmap),
                  pl.BlockSpec((tkv, w), kvmap),
                  pl.BlockSpec((1, hd), lambda b, i, j: (0, 0))],
        out_specs=pl.BlockSpec((tq, w), qmap),
        out_shape=jax.ShapeDtypeStruct((batch * t_len, w), BF16),
        scratch_shapes=[pltpu.VMEM((heads, 2 * tq, hd), BF16)] + _flash_scratch_wide(heads, 2 * tq, hd),
        compiler_params=_cparams("parallel", "parallel", "arbitrary"), name="diff_attention",
    )(lam.reshape(1, 1), q, k, v, g_sub.reshape(1, hd))


def _diff_past_kernel(lam_ref, q_ref, kc_ref, vc_ref, kn_ref, vn_ref, g_ref, o_ref,
                      qs_ref, m_ref, acc_ref, *, heads, tq, tkv, p_len, s_len, out_scale):
    j = pl.program_id(1)
    n_past = pl.num_programs(1) - 1
    hd = 2 * DIFF_D

    @pl.when(j == 0)
    def _():
        _flash_init_wide(m_ref, acc_ref)
        _diff_split_q(q_ref, qs_ref, heads, tq)

    @pl.when(j < n_past)
    def _():
        _heads_sweep(lambda h: qs_ref[h], lambda h: kc_ref[h], lambda h: vc_ref[h],
                     _mask_bias(p_len, tq, j * tkv, tkv, s_len), 2, m_ref, acc_ref, heads)

    @pl.when(j == n_past)
    def _():
        _heads_sweep(lambda h: qs_ref[h], lambda h: kn_ref[:, h * hd:(h + 1) * hd],
                     lambda h: vn_ref[:, h * hd:(h + 1) * hd],
                     _mask_bias(p_len, tq, p_len, kn_ref.shape[0], s_len), 2, m_ref, acc_ref, heads)
        _diff_finish(lam_ref, g_ref, o_ref, acc_ref, heads, tq, out_scale)


def diff_attention_past(q, kc, vc, kn, vn, l, lam, g_sub, batch, heads, t_len, out_scale):
    hd = 2 * DIFF_D
    w = heads * hd
    p_len = kc.shape[3]
    tn = kn.shape[0] // batch
    tkv = _past_tile(p_len, 1024)
    n_past = p_len // tkv
    cache = pl.BlockSpec((None, None, heads, tkv, hd), lambda b, j: (l, b, 0, jnp.minimum(j, n_past - 1), 0))
    new = pl.BlockSpec((tn, w), lambda b, j: (b, 0))
    qmap = lambda b, j: (b, 0)
    return pl.pallas_call(
        functools.partial(_diff_past_kernel, heads=heads, tq=t_len, tkv=tkv, p_len=p_len, s_len=p_len + t_len,
                          out_scale=out_scale),
        grid=(batch, n_past + 1),
        in_specs=[pl.BlockSpec(memory_space=pltpu.SMEM), pl.BlockSpec((t_len, w), qmap), cache, cache, new, new,
                  pl.BlockSpec((1, hd), lambda b, j: (0, 0))],
        out_specs=pl.BlockSpec((t_len, w), qmap),
        out_shape=jax.ShapeDtypeStruct((batch * t_len, w), BF16),
        scratch_shapes=[pltpu.VMEM((heads, 2 * t_len, hd), BF16)] + _flash_scratch_wide(heads, 2 * t_len, hd),
        compiler_params=_cparams("parallel", "arbitrary"), name="diff_attention_past",
    )(lam.reshape(1, 1), q, kc, vc, kn, vn, g_sub.reshape(1, hd))


MLA_KEY_W = 2 * MLA_NOPE


def _mla_kernel(q_ref, k_ref, v_ref, o_ref, m_ref, acc_ref, *, heads, tq, tkv, s_len):
    i, j = pl.program_id(1), pl.program_id(2)
    q_of = lambda h: q_ref[:, h * MLA_KEY_W:(h + 1) * MLA_KEY_W]
    k_of = lambda h: k_ref[:, h * MLA_KEY_W:(h + 1) * MLA_KEY_W]
    v_of = lambda h: v_ref[:, h * MLA_NOPE:(h + 1) * MLA_NOPE]
    full = _full_kv_blocks(i, tq, tkv, 0, s_len)

    @pl.when(j == 0)
    def _():
        _flash_init_wide(m_ref, acc_ref)

    @pl.when(j < full)
    def _():
        _heads_sweep(q_of, k_of, v_of, None, 1, m_ref, acc_ref, heads)

    @pl.when((j >= full) & (j <= _last_kv_block(i, tq, tkv, 0, s_len)))
    def _():
        _heads_sweep(q_of, k_of, v_of, _mask_bias(i * tq, tq, j * tkv, tkv, s_len), 1, m_ref, acc_ref, heads)

    @pl.when(j == pl.num_programs(2) - 1)
    def _():
        for h in range(heads):
            o_ref[:, h * MLA_NOPE:(h + 1) * MLA_NOPE] = _flash_result_wide(acc_ref.at[h]).astype(o_ref.dtype)


def mla_attention(q, kv_up, batch, heads, t_len, tq, tkv):
    kw, vw = heads * MLA_KEY_W, heads * MLA_NOPE
    nq, nkv = t_len // tq, t_len // tkv
    last = functools.partial(_last_kv_block, tq=tq, tkv=tkv, q0=0, s_len=t_len)
    kvrow = lambda b, i, j: b * nkv + jnp.minimum(j, last(i))
    return pl.pallas_call(
        functools.partial(_mla_kernel, heads=heads, tq=tq, tkv=tkv, s_len=t_len),
        grid=(batch, nq, nkv),
        in_specs=[pl.BlockSpec((tq, kw), lambda b, i, j: (b * nq + i, 0)),
                  pl.BlockSpec((tkv, kw), lambda b, i, j: (kvrow(b, i, j), 0)),
                  pl.BlockSpec((tkv, vw), lambda b, i, j: (kvrow(b, i, j), kw // vw))],
        out_specs=pl.BlockSpec((tq, vw), lambda b, i, j: (b * nq + i, 0)),
        out_shape=jax.ShapeDtypeStruct((batch * t_len, vw), BF16),
        scratch_shapes=_flash_scratch_wide(heads, tq, MLA_NOPE),
        compiler_params=_cparams("parallel", "parallel", "arbitrary"), name="mla_attention",
    )(q, kv_up, kv_up)


def _mla_past_kernel(q_ref, wuk_ref, ckvc_ref, kpec_ref, ckvn_ref, kpen_ref, wuv_ref,
                     o_ref, qlat_ref, qpe_ref, m_ref, l_ref, acc_ref, *, heads, tq, tkv, p_len, s_len):
    j = pl.program_id(1)
    n_past = pl.num_programs(1) - 1

    @pl.when(j == 0)
    def _():
        _flash_init(m_ref, l_ref, acc_ref)
        for h in range(heads):
            lo = h * MLA_KEY_W
            qlat = jnp.dot(q_ref[:, lo:lo + MLA_NOPE], wuk_ref[h], preferred_element_type=F32)
            qlat_ref[h * tq:(h + 1) * tq, :] = qlat.astype(BF16)
            qpe_ref[h * tq:(h + 1) * tq, :] = q_ref[:, lo + MLA_NOPE:lo + MLA_NOPE + MLA_ROPE]

    def sweep(ckv, kpe, k_start):
        s = (lax.dot_general(qlat_ref[...], ckv, _NT, preferred_element_type=F32)
             + lax.dot_general(qpe_ref[...], kpe, _NT, preferred_element_type=F32))
        s = _add_bias(s, _mask_bias(p_len, tq, k_start, ckv.shape[0], s_len), heads)
        _flash_update(s, ckv, m_ref, l_ref, acc_ref)

    @pl.when(j < n_past)
    def _():
        sweep(ckvc_ref[...].astype(BF16), kpec_ref[...].astype(BF16), j * tkv)

    @pl.when(j == n_past)
    def _():
        sweep(ckvn_ref[...], kpen_ref[...], p_len)
        v_w = wuv_ref.shape[2]
        o_lat = _flash_result(l_ref, acc_ref).astype(BF16)
        for h in range(heads):
            o_h = jnp.dot(o_lat[h * tq:(h + 1) * tq], wuv_ref[h], preferred_element_type=F32)
            o_ref[:, h * v_w:(h + 1) * v_w] = o_h.astype(o_ref.dtype)


def mla_attention_past(q, wuk_t, wuv, l, ckv_c, kpe_c, ckv_n, kpe_n, batch, t_len):
    _, heads, _, rank = wuk_t.shape
    v_w = wuv.shape[3]
    p_len = ckv_c.shape[2]
    tn = ckv_n.shape[0] // batch
    tkv = _past_tile(p_len, 1024)
    n_past = p_len // tkv
    rows = heads * t_len
    layer4 = lambda a: pl.BlockSpec((None,) + a.shape[1:], lambda b, j: (l, 0, 0, 0))
    cache = lambda w: pl.BlockSpec((None, None, tkv, w), lambda b, j: (l, b, jnp.minimum(j, n_past - 1), 0))
    new = lambda w: pl.BlockSpec((tn, w), lambda b, j: (b, 0))
    return pl.pallas_call(
        functools.partial(_mla_past_kernel, heads=heads, tq=t_len, tkv=tkv, p_len=p_len, s_len=p_len + t_len),
        grid=(batch, n_past + 1),
        in_specs=[pl.BlockSpec((t_len, q.shape[1]), lambda b, j: (b, 0)), layer4(wuk_t),
                  cache(rank), cache(MLA_ROPE), new(rank), new(MLA_ROPE), layer4(wuv)],
        out_specs=pl.BlockSpec((t_len, heads * v_w), lambda b, j: (b, 0)),
        out_shape=jax.ShapeDtypeStruct((batch * t_len, heads * v_w), BF16),
        scratch_shapes=[pltpu.VMEM((rows, rank), BF16), pltpu.VMEM((rows, MLA_ROPE), BF16)]
                       + _flash_scratch(None, rows, rank),
        compiler_params=_cparams("parallel", "arbitrary"), name="mla_attention_past",
    )(q, wuk_t, ckv_c, kpe_c, ckv_n, kpe_n, wuv)


def _cross_block_kernel(y_ref, x_ref, g_mix_ref, g_pre_ref, wq_ref, k_ref, v_ref, wo_ref, g_post_ref, g_next_ref,
                        xo_ref, ho_ref, *, heads, hd):
    x1 = x_ref[...] + _rms(y_ref[...], g_mix_ref[...])
    q = jnp.dot(_rms(x1, g_pre_ref[...]).astype(BF16), wq_ref[...], preferred_element_type=F32).astype(BF16)
    outs = []
    for h in range(heads):
        sl = slice(h * hd, (h + 1) * hd)
        s = lax.dot_general(q[:, sl], k_ref[:, sl], _NT, preferred_element_type=F32) * (hd ** -0.5)
        p = jnp.exp(s - jnp.max(s, axis=-1, keepdims=True))
        o = jnp.dot(p.astype(BF16), v_ref[:, sl], preferred_element_type=F32)
        outs.append((o / jnp.sum(p, axis=-1, keepdims=True)).astype(BF16))
    y2 = jnp.dot(jnp.concatenate(outs, axis=1), wo_ref[...], preferred_element_type=F32)
    x2 = x1 + _rms(y2, g_post_ref[...])
    xo_ref[...] = x2
    ho_ref[...] = _rms(x2, g_next_ref[...]).astype(ho_ref.dtype)


def cross_block(y_mix, x, g_mix_post, g_pre, wq, mk, mv, wo, l, g_post, g_next, batch, t_len, mem_len, heads, tm):
    m, d = x.shape
    w = wq.shape[2]
    nq = t_len // tm
    row = pl.BlockSpec((tm, d), lambda b, i: (b * nq + i, 0))
    vec = pl.BlockSpec((1, d), lambda b, i: (0, 0))
    kv = pl.BlockSpec((mem_len, w), lambda b, i: (b, 0))
    once = pl.Buffered(1)
    return pl.pallas_call(
        functools.partial(_cross_block_kernel, heads=heads, hd=w // heads),
        grid=(batch, nq),
        in_specs=[row, row, vec, vec,
                  pl.BlockSpec((None, d, w), lambda b, i: (l, 0, 0), pipeline_mode=once), kv, kv,
                  pl.BlockSpec((None, w, d), lambda b, i: (l, 0, 0), pipeline_mode=once), vec, vec],
        out_specs=[row, row],
        out_shape=[jax.ShapeDtypeStruct((m, d), F32), jax.ShapeDtypeStruct((m, d), BF16)],
        compiler_params=_cparams("parallel", "parallel"), name="cross_block",
    )(y_mix, x, g_mix_post.reshape(1, d), g_pre.reshape(1, d), wq, mk, mv, wo,
      g_post.reshape(1, d), g_next.reshape(1, d))


def _rope_tables(pos):
    half = DIFF_D // 2
    inv = ROPE_THETA ** (-jnp.arange(half, dtype=F32) / half)
    ang = pos.astype(F32)[:, None] * inv[None, :]
    cos, sin, zero = jnp.cos(ang), jnp.sin(ang), jnp.zeros_like(ang)
    reps = LANES // DIFF_D
    return (jnp.tile(cos, (1, 2 * reps)), jnp.tile(jnp.concatenate([-sin, zero], 1), (1, reps)),
            jnp.tile(jnp.concatenate([zero, sin], 1), (1, reps)))


def _round_up(a, b):
    return -(-a // b) * b


def _kv_up_weight(w_uk, w_uv):
    depth, rank, heads, nope = w_uk.shape
    k_rows = jnp.concatenate([w_uk, jnp.zeros_like(w_uk)], axis=3).reshape(depth, rank, heads * MLA_KEY_W)
    slot = jnp.concatenate([jnp.zeros((MLA_ROPE, nope), F32), jnp.eye(MLA_ROPE, dtype=F32),
                            jnp.zeros((MLA_ROPE, MLA_KEY_W - nope - MLA_ROPE), F32)], axis=1)
    rope_rows = jnp.broadcast_to(jnp.tile(slot, (1, heads)), (depth, MLA_ROPE, heads * MLA_KEY_W))
    pad = LANES - MLA_ROPE
    w_k = jnp.concatenate([k_rows, rope_rows, jnp.zeros((depth, pad, heads * MLA_KEY_W), F32)], axis=1)
    w_v = jnp.concatenate([w_uv.reshape(depth, rank, -1), jnp.zeros((depth, LANES, heads * w_uv.shape[3]), F32)],
                          axis=1)
    return jnp.concatenate([w_k, w_v], axis=2).astype(BF16)


def _pad_rows(a, batch, rows):
    t_len = a.shape[0] // batch
    a = jnp.pad(a.reshape(batch, t_len, a.shape[1]), ((0, 0), (0, rows - t_len), (0, 0)))
    return a.reshape(batch * rows, a.shape[2])


def _mixer_and_memory(wts, l, x, h, batch, t_len, pos0, tabs, mem_k, mem_v, past, tiles, state):
    tm, tt, tq_diff, tkv_diff, tq_mla, tkv_mla, tq_x = tiles
    lam_init = 0.8 - 0.6 * math.exp(-0.3 * l)
    pool_w, diff_w, q_rank, kv_rank = wts["dims"]
    heads_d = diff_w // (2 * DIFF_D)
    heads_m = wts["w_uk_t"].shape[1]

    (u, qd, kd16, vd16, cqn, ckv16, kpe16, kc16), state = input_projection(
        h, wts["w_in"], tabs, wts["g_mla_q"][l], wts["g_mla_kv"][l], wts["dims"],
        min(tm, 512), tm, l, wts["depth"], state)

    keep = POOL_HALO - 1
    if past is None:
        prev = jnp.zeros((batch, POOL_HALO, pool_w), F32)
    else:
        prev = jnp.concatenate([jnp.zeros((batch, 1, pool_w), F32), past["pool"]], axis=1)
    u3 = u.reshape(batch, t_len, pool_w)
    if t_len >= keep:
        new_pool = u3[:, t_len - keep:]
    else:
        new_pool = jnp.concatenate([prev[:, 1 + t_len:], u3], axis=1)
    y_pool = pool_mix(u, prev, wts["w_pool"], l, wts["pool_scale"][l], batch, t_len, pos0, tt)

    lq = wts["diff_lambda"][l]
    lam = jnp.exp(jnp.sum(lq[0] * lq[1])) - jnp.exp(jnp.sum(lq[2] * lq[3])) + lam_init
    q = mla_query_projection(cqn, wts["w_uq"], tabs, l, tm, 1024)
    if past is None:
        o_diff = diff_attention(qd, kd16, vd16, lam, wts["g_diff_sub"][l], batch, heads_d, t_len,
                                1.0 - lam_init, tq_diff, tkv_diff)
        kv_up = matmul(kc16, wts["w_kv_up"], l, BF16, tm, heads_m * MLA_NOPE)
        o_mla = mla_attention(q, kv_up, batch, heads_m, t_len, tq_mla, tkv_mla)
    else:
        tn = _round_up(t_len, LANES)
        o_diff = diff_attention_past(qd, past["diff_k"], past["diff_v"], _pad_rows(kd16, batch, tn),
                                     _pad_rows(vd16, batch, tn), l, lam, wts["g_diff_sub"][l], batch, heads_d,
                                     t_len, 1.0 - lam_init)
        o_mla = mla_attention_past(q, wts["w_uk_t"], wts["w_uv"], l, past["ckv"], past["kpe"],
                                   _pad_rows(ckv16, batch, tn), _pad_rows(kpe16, batch, tn), batch, t_len)

    y = matmul_cat([y_pool, o_diff, o_mla], wts["w_out"], l, F32, tm, 1024)
    x, h = cross_block(y, x, wts["g_mix_post"][l], wts["g_x_pre"][l], wts["w_mem_q"], mem_k, mem_v,
                       wts["w_mem_o"], l, wts["g_x_post"][l], wts["g_ff_pre"][l], batch, t_len,
                       mem_k.shape[0] // batch, wts["mem_heads"], tq_x)
    return x, h, state, new_pool


def _ffn_pair(wts, l, x, h, xs, hs, g_next, tiles, tiles_s):
    a, a_s = matmul_swiglu_pair(h, hs, wts["w_gate"], wts["w_up"], l, 2 * tiles[0], 256)
    y, y_s = matmul_pair(a, a_s, wts["w_down"], l, F32, 512, 512)
    x, h = post_res(y, x, wts["g_ff_post"][l], g_next, tiles[1])
    xs, hs = post_res(y_s, xs, wts["g_ff_post"][l], g_next, tiles_s[1])
    return x, h, xs, hs


def kernel(x_prompt, x_sample, cache_diff_k, cache_diff_v, cache_mla_ckv, cache_mla_kpe, cache_pool, cache_mem_k, cache_mem_v, mem_prompt, g_mix_pre, w_in, w_pool, pool_scale, diff_lambda, g_diff_sub, g_mla_q, w_mla_uq, w_mla_uk, w_mla_uv, g_mla_kv, w_out, g_mix_post, g_mem, w_mem_k, w_mem_v, w_mem_q, w_mem_o, g_x_pre, g_x_post, g_ff_pre, w_gate, w_up, w_down, g_ff_post):
    depth = w_in.shape[0]
    bp, tp, d = x_prompt.shape
    bs, ts, _ = x_sample.shape
    past_len = cache_mla_ckv.shape[2]
    pool_w = cache_pool.shape[3]
    heads_d, diff_w = cache_diff_k.shape[3], cache_diff_k.shape[3] * cache_diff_k.shape[4]
    q_rank, kv_rank = g_mla_q.shape[1], g_mla_kv.shape[1]
    mla_heads = w_mla_uk.shape[2]
    mem_len, mem_heads, mem_hd = cache_mem_k.shape[2:]
    mem_w = mem_heads * mem_hd

    uq = w_mla_uq.reshape(depth, q_rank, mla_heads, MLA_NOPE + MLA_ROPE)
    wts = {
        "depth": depth, "dims": (pool_w, diff_w, q_rank, kv_rank), "mem_heads": mem_heads,
        "w_in": w_in.astype(BF16),
        "w_pool": w_pool.astype(BF16), "pool_scale": pool_scale, "diff_lambda": diff_lambda,
        "g_diff_sub": g_diff_sub, "g_mla_q": g_mla_q, "g_mla_kv": g_mla_kv,
        "w_uq": jnp.pad(uq, ((0, 0), (0, 0), (0, 0), (0, MLA_KEY_W - MLA_NOPE - MLA_ROPE))).reshape(
            depth, q_rank, mla_heads * MLA_KEY_W).astype(BF16),
        "w_uk_t": jnp.transpose(w_mla_uk, (0, 2, 3, 1)).astype(BF16),
        "w_uv": jnp.transpose(w_mla_uv, (0, 2, 1, 3)).astype(BF16),
        "w_kv_up": _kv_up_weight(w_mla_uk, w_mla_uv),
        "w_out": w_out.astype(BF16), "g_mix_post": g_mix_post,
        "w_mem_q": w_mem_q.astype(BF16), "w_mem_o": w_mem_o.astype(BF16),
        "g_x_pre": g_x_pre, "g_x_post": g_x_post, "g_ff_pre": g_ff_pre,
        "w_gate": w_gate, "w_up": w_up, "w_down": w_down.astype(BF16),
        "g_ff_post": g_ff_post,
    }
    w_mem_kv = jnp.concatenate([w_mem_k, w_mem_v], axis=2).astype(BF16)

    tabs_p = _rope_tables(jnp.arange(tp))
    tabs_s = tuple(jnp.tile(t, (bs, 1)) for t in _rope_tables(past_len + jnp.arange(ts)))
    tiles_p = (1024, 256, 512, 512, 512, 512, 256)
    tiles_s = (bs * ts, ts, ts, 0, ts, 0, ts)

    xp = x_prompt.reshape(bp * tp, d)
    xs = x_sample.reshape(bs * ts, d)
    hp = rmsnorm_bf16(xp, g_mix_pre[0], tiles_p[1])
    hs = rmsnorm_bf16(xs, g_mix_pre[0], tiles_s[1])
    mem_n = mem_prompt.reshape(bp * mem_len, d)
    mem_k_s = cache_mem_k.reshape(depth, bs * mem_len, mem_w).astype(BF16)
    mem_v_s = cache_mem_v.reshape(depth, bs * mem_len, mem_w).astype(BF16)
    past = {"diff_k": jnp.transpose(cache_diff_k, (0, 1, 3, 2, 4)).astype(BF16),
            "diff_v": jnp.transpose(cache_diff_v, (0, 1, 3, 2, 4)).astype(BF16),
            "ckv": cache_mla_ckv, "kpe": cache_mla_kpe}

    state_p = state_s = None
    pools_p, pools_s, mem_ks, mem_vs = [], [], [], []
    for l in range(depth):
        g_next = g_mix_pre[l + 1] if l + 1 < depth else None
        m = rmsnorm_bf16(mem_n, g_mem[l], 256)
        mkv = matmul(m, w_mem_kv, l, F32, 1024, 1024)
        mk, mv = mkv[:, :mem_w], mkv[:, mem_w:]
        mem_ks.append(mk.reshape(bp, mem_len, mem_heads, mem_hd))
        mem_vs.append(mv.reshape(bp, mem_len, mem_heads, mem_hd))
        xp, hp, state_p, pool_p = _mixer_and_memory(wts, l, xp, hp, bp, tp, 0, tabs_p, mk.astype(BF16),
                                                    mv.astype(BF16), None, tiles_p, state_p)
        pools_p.append(pool_p)
        xs, hs, state_s, pool_s = _mixer_and_memory(wts, l, xs, hs, bs, ts, past_len, tabs_s, mem_k_s[l],
                                                    mem_v_s[l], dict(past, pool=cache_pool[l]), tiles_s, state_s)
        pools_s.append(pool_s)
        xp, hp, xs, hs = _ffn_pair(wts, l, xp, hp, xs, hs, g_next, tiles_p, tiles_s)

    hd = 2 * DIFF_D
    shape_p = lambda a, tail: a.reshape((depth, bp, tp) + tail)
    shape_s = lambda a, tail: a.reshape((depth, bs, ts) + tail)
    return (xp.reshape(bp, tp, d), xs.reshape(bs, ts, d),
            shape_p(state_p[0], (heads_d, hd)), shape_p(state_p[1], (heads_d, hd)),
            shape_p(state_p[2], (kv_rank,)), shape_p(state_p[3], (MLA_ROPE,)),
            jnp.stack(pools_p), jnp.stack(mem_ks), jnp.stack(mem_vs),
            shape_s(state_s[0], (heads_d, hd)), shape_s(state_s[1], (heads_d, hd)),
            shape_s(state_s[2], (kv_rank,)), shape_s(state_s[3], (MLA_ROPE,)),
            jnp.stack(pools_s))
```

```python
import functools
import math

import jax
import jax.numpy as jnp
from jax import lax
from jax.experimental import pallas as pl
from jax.experimental.pallas import tpu as pltpu

F32 = jnp.float32
BF16 = jnp.bfloat16

EPS = 1e-6
CHUNK = 64
ROPE_THETA = 10000.0
POOL_WINDOWS = (2, 4, 8, 16)
POOL_HALO = 16
DIFF_D = 64
MLA_NOPE = 128
MLA_ROPE = 64
LANES = 128
MIB = 1024 * 1024
VMEM_LIMIT = 56 * MIB

_NT = (((1,), (1,)), ((), ()))


def _cparams(*sem):
    return pltpu.CompilerParams(dimension_semantics=sem, vmem_limit_bytes=VMEM_LIMIT)


def _rms(x, g):
    return x * lax.rsqrt(jnp.mean(x * x, axis=-1, keepdims=True) + EPS) * g


def _rmsnorm_kernel(x_ref, g_ref, o_ref):
    o_ref[...] = _rms(x_ref[...], g_ref[...]).astype(o_ref.dtype)


def rmsnorm_bf16(x, g, tm):
    m, d = x.shape
    return pl.pallas_call(
        _rmsnorm_kernel,
        grid=(m // tm,),
        in_specs=[pl.BlockSpec((tm, d), lambda i: (i, 0)), pl.BlockSpec((1, d), lambda i: (0, 0))],
        out_specs=pl.BlockSpec((tm, d), lambda i: (i, 0)),
        out_shape=jax.ShapeDtypeStruct((m, d), BF16),
        compiler_params=_cparams("parallel"),
        name="rmsnorm",
    )(x, g.reshape(1, d))


def _post_res_kernel(y_ref, x_ref, gp_ref, gn_ref, xo_ref, ho_ref):
    x = x_ref[...] + _rms(y_ref[...], gp_ref[...])
    xo_ref[...] = x
    ho_ref[...] = _rms(x, gn_ref[...]).astype(ho_ref.dtype)


def _post_res_last_kernel(y_ref, x_ref, gp_ref, xo_ref):
    xo_ref[...] = x_ref[...] + _rms(y_ref[...], gp_ref[...])


def post_res(y, x, g_post, g_next, tm):
    m, d = x.shape
    row = pl.BlockSpec((tm, d), lambda i: (i, 0))
    vec = pl.BlockSpec((1, d), lambda i: (0, 0))
    if g_next is None:
        return pl.pallas_call(
            _post_res_last_kernel, grid=(m // tm,), in_specs=[row, row, vec], out_specs=row,
            out_shape=jax.ShapeDtypeStruct((m, d), F32), compiler_params=_cparams("parallel"),
            name="post_res_last",
        )(y, x, g_post.reshape(1, d)), None
    return pl.pallas_call(
        _post_res_kernel, grid=(m // tm,), in_specs=[row, row, vec, vec], out_specs=[row, row],
        out_shape=[jax.ShapeDtypeStruct((m, d), F32), jax.ShapeDtypeStruct((m, d), BF16)],
        compiler_params=_cparams("parallel"), name="post_res",
    )(y, x, g_post.reshape(1, d), g_next.reshape(1, d))


def _mm_kernel(x_ref, w_ref, o_ref):
    o_ref[...] = jnp.dot(x_ref[...], w_ref[...], preferred_element_type=F32).astype(o_ref.dtype)


def matmul(x, w, l, out_dtype, tm, tn):
    m, k = x.shape
    n = w.shape[2]
    tm, tn = min(tm, m), min(tn, n)
    return pl.pallas_call(
        _mm_kernel, grid=(n // tn, m // tm),
        in_specs=[pl.BlockSpec((tm, k), lambda j, i: (i, 0)), pl.BlockSpec((None, k, tn), lambda j, i: (l, 0, j))],
        out_specs=pl.BlockSpec((tm, tn), lambda j, i: (i, j)),
        out_shape=jax.ShapeDtypeStruct((m, n), out_dtype),
        compiler_params=_cparams("parallel", "parallel"), name="matmul",
    )(x, w)


def _mm_cat_kernel(*refs):
    n_in = (len(refs) - 1) // 2
    acc = jnp.dot(refs[0][...], refs[n_in][...], preferred_element_type=F32)
    for a in range(1, n_in):
        acc += jnp.dot(refs[a][...], refs[n_in + a][...], preferred_element_type=F32)
    refs[-1][...] = acc.astype(refs[-1].dtype)


def matmul_cat(xs, w, l, out_dtype, tm, tn):
    m = xs[0].shape[0]
    n = w.shape[2]
    tm, tn = min(tm, m), min(tn, n)
    x_specs, w_specs, off = [], [], 0
    for x in xs:
        k = x.shape[1]
        assert off % k == 0, "each input's row band of w must start on a multiple of its own width"
        x_specs.append(pl.BlockSpec((tm, k), lambda j, i: (i, 0)))
        w_specs.append(pl.BlockSpec((None, k, tn), functools.partial(lambda j, i, r: (l, r, j), r=off // k)))
        off += k
    return pl.pallas_call(
        _mm_cat_kernel, grid=(n // tn, m // tm),
        in_specs=x_specs + w_specs,
        out_specs=pl.BlockSpec((tm, tn), lambda j, i: (i, j)),
        out_shape=jax.ShapeDtypeStruct((m, n), out_dtype),
        compiler_params=_cparams("parallel", "parallel"), name="matmul_cat",
    )(*xs, *([w] * len(xs)))


def _swiglu_kernel(x_ref, wg_ref, wu_ref, o_ref):
    x = x_ref[...]
    g = jnp.dot(x, wg_ref[...].astype(BF16), preferred_element_type=F32)
    u = jnp.dot(x, wu_ref[...].astype(BF16), preferred_element_type=F32)
    o_ref[...] = (g * jax.nn.sigmoid(g) * u).astype(o_ref.dtype)


def matmul_swiglu(x, wg, wu, l, tm, tn):
    m, k = x.shape
    n = wg.shape[2]
    tm = min(tm, m)
    wspec = pl.BlockSpec((None, k, tn), lambda i, j: (l, 0, j))
    return pl.pallas_call(
        _swiglu_kernel, grid=(m // tm, n // tn),
        in_specs=[pl.BlockSpec((tm, k), lambda i, j: (i, 0), pipeline_mode=pl.Buffered(1)), wspec, wspec],
        out_specs=pl.BlockSpec((tm, tn), lambda i, j: (i, j)),
        out_shape=jax.ShapeDtypeStruct((m, n), BF16),
        compiler_params=_cparams("parallel", "parallel"), name="matmul_swiglu",
    )(x, wg, wu)


def _swiglu_pair_kernel(x_ref, wg_ref, wu_ref, xs_ref, o_ref, os_ref):
    wg, wu = wg_ref[...].astype(BF16), wu_ref[...].astype(BF16)

    def gated(x):
        g = jnp.dot(x, wg, preferred_element_type=F32)
        u = jnp.dot(x, wu, preferred_element_type=F32)
        return (g * jax.nn.sigmoid(g) * u).astype(BF16)

    o_ref[...] = gated(x_ref[...])

    @pl.when(pl.program_id(0) == 0)
    def _():
        os_ref[...] = gated(xs_ref[...])


def matmul_swiglu_pair(x, xs, wg, wu, l, tm, tn):
    m, k = x.shape
    ms = xs.shape[0]
    n = wg.shape[2]
    tm = min(tm, m)
    nj = n // tn
    wspec = pl.BlockSpec((None, k, tn), lambda i, j: (l, 0, j))
    once = pl.Buffered(1)
    return pl.pallas_call(
        _swiglu_pair_kernel, grid=(m // tm, nj),
        in_specs=[pl.BlockSpec((tm, k), lambda i, j: (i, 0), pipeline_mode=once), wspec, wspec,
                  pl.BlockSpec((ms, k), lambda i, j: (0, 0), pipeline_mode=once)],
        out_specs=[pl.BlockSpec((tm, tn), lambda i, j: (i, j)),
                   pl.BlockSpec((ms, tn), lambda i, j: (0, jnp.where(i == 0, j, nj - 1)))],
        out_shape=[jax.ShapeDtypeStruct((m, n), BF16), jax.ShapeDtypeStruct((ms, n), BF16)],
        compiler_params=_cparams("arbitrary", "arbitrary"), name="matmul_swiglu_pair",
    )(x, wg, wu, xs)


def _mm_pair_kernel(x_ref, w_ref, xs_ref, o_ref, os_ref):
    o_ref[...] = jnp.dot(x_ref[...], w_ref[...], preferred_element_type=F32).astype(o_ref.dtype)

    @pl.when(pl.program_id(1) == 0)
    def _():
        os_ref[...] = jnp.dot(xs_ref[...], w_ref[...], preferred_element_type=F32).astype(os_ref.dtype)


def matmul_pair(x, xs, w, l, out_dtype, tm, tn):
    m, k = x.shape
    ms = xs.shape[0]
    n = w.shape[2]
    tm, tn = min(tm, m), min(tn, n)
    return pl.pallas_call(
        _mm_pair_kernel, grid=(n // tn, m // tm),
        in_specs=[pl.BlockSpec((tm, k), lambda j, i: (i, 0)), pl.BlockSpec((None, k, tn), lambda j, i: (l, 0, j)),
                  pl.BlockSpec((ms, k), lambda j, i: (0, 0), pipeline_mode=pl.Buffered(1))],
        out_specs=[pl.BlockSpec((tm, tn), lambda j, i: (i, j)), pl.BlockSpec((ms, tn), lambda j, i: (0, j))],
        out_shape=[jax.ShapeDtypeStruct((m, n), out_dtype), jax.ShapeDtypeStruct((ms, n), out_dtype)],
        compiler_params=_cparams("arbitrary", "arbitrary"), name="matmul_pair",
    )(x, w, xs)


def _q_proj_kernel(x_ref, w_ref, cos_ref, sa_ref, sb_ref, o_ref, *, scale):
    acc = jnp.dot(x_ref[...], w_ref[...], preferred_element_type=F32)
    cos, sa, sb = cos_ref[...], sa_ref[...], sb_ref[...]
    for s in range(acc.shape[1] // MLA_KEY_W):
        lo, mid, hi = s * MLA_KEY_W, s * MLA_KEY_W + MLA_NOPE, (s + 1) * MLA_KEY_W
        o_ref[:, lo:mid] = (acc[:, lo:mid] * scale).astype(o_ref.dtype)
        o_ref[:, mid:hi] = (_rope_chunk(acc[:, mid:hi], cos, sa, sb) * scale).astype(o_ref.dtype)


def mla_query_projection(cqn, w_slots, tabs, l, tm, tn):
    m, k = cqn.shape
    n = w_slots.shape[2]
    t_tab = tabs[0].shape[0]
    tm, tn = min(tm, t_tab), min(tn, n)
    nt = t_tab // tm
    tab = pl.BlockSpec((tm, LANES), lambda j, i: (i % nt, 0))
    return pl.pallas_call(
        functools.partial(_q_proj_kernel, scale=(MLA_NOPE + MLA_ROPE) ** -0.5),
        grid=(n // tn, m // tm),
        in_specs=[pl.BlockSpec((tm, k), lambda j, i: (i, 0)), pl.BlockSpec((None, k, tn), lambda j, i: (l, 0, j)),
                  tab, tab, tab],
        out_specs=pl.BlockSpec((tm, tn), lambda j, i: (i, j)),
        out_shape=jax.ShapeDtypeStruct((m, n), BF16),
        compiler_params=_cparams("parallel", "parallel"), name="mla_query_projection",
    )(cqn, w_slots, *tabs)


def _rope_chunk(x, cos, sa, sb):
    return x * cos + pltpu.roll(x, LANES - 32, 1) * sa + pltpu.roll(x, 32, 1) * sb


def _rope_cols(x, cos, sa, sb):
    return jnp.concatenate([_rope_chunk(x[:, c * LANES:(c + 1) * LANES], cos, sa, sb)
                            for c in range(x.shape[1] // LANES)], axis=1)


_IN_GROUPS = 5
_IN_INPUTS = 6


def _in_proj_kernel(*refs, q_scale):
    x_ref, w_ref, cos_ref, sa_ref, sb_ref, gq_ref = refs[:_IN_INPUTS]
    u_ref, qd_ref, kd32_ref, kd16_ref, vd32_ref, vd16_ref, cq_ref, acc_a, acc_b = refs[-9:]
    accs = (acc_a, acc_b)
    j = pl.program_id(1)
    rope = lambda a: _rope_cols(a, cos_ref[...], sa_ref[...], sb_ref[...])

    def finish(g, acc):
        if g == 0:
            u_ref[...] = acc
        elif g == 1:
            qd_ref[...] = (rope(acc) * q_scale).astype(BF16)
        elif g == 2:
            k = rope(acc)
            kd32_ref[...] = k
            kd16_ref[...] = k.astype(BF16)
        elif g == 3:
            vd32_ref[...] = acc
            vd16_ref[...] = acc.astype(BF16)
        else:
            cq_ref[...] = _rms(acc, gq_ref[...]).astype(BF16)

    for g in range(_IN_GROUPS + 1):
        @pl.when(j == g)
        def _(g=g):
            if g < _IN_GROUPS:
                accs[g % 2][...] = jnp.dot(x_ref[...], w_ref[...], preferred_element_type=F32)
            if g > 0:
                finish(g - 1, accs[(g - 1) % 2][...])


_LAT_INPUTS = 6


def _in_proj_latent_kernel(*refs):
    x_ref, w_ref, cos_ref, sa_ref, sb_ref, gkv_ref = refs[:_LAT_INPUTS]
    ckv32_ref, ckv16_ref, kpe32_ref, kpe16_ref, kc16_ref = refs[-5:]
    rank = ckv32_ref.shape[1]
    acc = jnp.dot(x_ref[...], w_ref[...], preferred_element_type=F32)
    ckv = _rms(acc[:, 0:rank], gkv_ref[...])
    ckv32_ref[...] = ckv
    ckv16_ref[...] = ckv.astype(BF16)
    lane = lax.broadcasted_iota(jnp.int32, (acc.shape[0], LANES), 1)
    kpe_raw = jnp.where(lane < MLA_ROPE, acc[:, rank:rank + LANES], 0.0)
    kpe_chunk = _rope_chunk(kpe_raw, cos_ref[...], sa_ref[...], sb_ref[...])
    kpe = kpe_chunk[:, :MLA_ROPE]
    kpe32_ref[...] = kpe
    kpe16_ref[...] = kpe.astype(BF16)
    kc16_ref[:, 0:rank] = ckv.astype(BF16)
    kc16_ref[:, rank:rank + LANES] = kpe_chunk.astype(BF16)


def input_projection(h, w_in, tabs, g_q, g_kv, dims, tm, tm_lat, l, depth, state):
    m, d = h.shape
    pool_w, diff_w, q_rank, kv_rank = dims
    gw = diff_w
    lw = kv_rank + LANES
    assert pool_w == gw and q_rank == gw and w_in.shape[2] == _IN_GROUPS * gw + kv_rank + MLA_ROPE
    assert (_IN_GROUPS * gw) % lw == 0, "the latent columns must start on a multiple of their block width"
    t_tab = tabs[0].shape[0]
    tm, tm_lat = min(tm, t_tab), min(tm_lat, t_tab)
    state = (None, None) if state is None else ((state[0], state[1]), (state[2], state[3]))
    sds = lambda w, dt: jax.ShapeDtypeStruct((m, w), dt)
    stk = lambda w: jax.ShapeDtypeStruct((depth, m, w), F32)
    alias = pl.BlockSpec(memory_space=pl.ANY)

    nt = t_tab // tm
    row = lambda w: pl.BlockSpec((tm, w), lambda i, j: (i, 0))
    lay = lambda w: pl.BlockSpec((None, tm, w), lambda i, j: (l, i, 0))
    tab = pl.BlockSpec((tm, LANES), lambda i, j: (i % nt, 0))
    prev = () if state[0] is None else state[0]
    u, qd, kd32, kd16, vd32, vd16, cqn = pl.pallas_call(
        functools.partial(_in_proj_kernel, q_scale=DIFF_D ** -0.5),
        grid=(m // tm, _IN_GROUPS + 1),
        in_specs=[pl.BlockSpec((tm, d), lambda i, j: (jnp.minimum(i + j // _IN_GROUPS, m // tm - 1), 0)),
                  pl.BlockSpec((None, d, gw), lambda i, j: (l, 0, j % _IN_GROUPS)),
                  tab, tab, tab, pl.BlockSpec((1, gw), lambda i, j: (0, 0))] + [alias] * len(prev),
        out_specs=[row(gw), row(gw), lay(gw), row(gw), lay(gw), row(gw), row(gw)],
        out_shape=[sds(gw, F32), sds(gw, BF16), stk(gw), sds(gw, BF16), stk(gw), sds(gw, BF16), sds(gw, BF16)],
        scratch_shapes=[pltpu.VMEM((tm, gw), F32), pltpu.VMEM((tm, gw), F32)],
        input_output_aliases={_IN_INPUTS + a: o for a, o in enumerate((2, 4)[:len(prev)])},
        compiler_params=_cparams("parallel", "arbitrary"), name="in_proj",
    )(h, w_in, *tabs, g_q.reshape(1, gw), *prev)

    nt = t_tab // tm_lat
    row = lambda w: pl.BlockSpec((tm_lat, w), lambda i: (i, 0))
    lay = lambda w: pl.BlockSpec((None, tm_lat, w), lambda i: (l, i, 0))
    tab = pl.BlockSpec((tm_lat, LANES), lambda i: (i % nt, 0))
    prev = () if state[1] is None else state[1]
    ckv32, ckv16, kpe32, kpe16, kc16 = pl.pallas_call(
        _in_proj_latent_kernel,
        grid=(m // tm_lat,),
        in_specs=[pl.BlockSpec((tm_lat, d), lambda i: (i, 0)),
                  pl.BlockSpec((None, d, lw), lambda i: (l, 0, _IN_GROUPS * gw // lw)),
                  tab, tab, tab, pl.BlockSpec((1, kv_rank), lambda i: (0, 0))] + [alias] * len(prev),
        out_specs=[lay(kv_rank), row(kv_rank), lay(MLA_ROPE), row(MLA_ROPE), row(lw)],
        out_shape=[stk(kv_rank), sds(kv_rank, BF16), stk(MLA_ROPE), sds(MLA_ROPE, BF16), sds(lw, BF16)],
        input_output_aliases={_LAT_INPUTS + a: o for a, o in enumerate((0, 2)[:len(prev)])},
        compiler_params=_cparams("parallel"), name="in_proj_latent",
    )(h, w_in, *tabs, g_kv.reshape(1, kv_rank), *prev)
    return (u, qd, kd16, vd16, cqn, ckv16, kpe16, kc16), (kd32, vd32, ckv32, kpe32)


def _pool_kernel(z_ref, halo_ref, prev_ref, wp_ref, sc_ref, o_ref, ext_ref, *, tt, pos0):
    i = pl.program_id(1)
    ext_ref[0:POOL_HALO, :] = jnp.where(i == 0, prev_ref[0], halo_ref[...])
    ext_ref[POOL_HALO:POOL_HALO + tt, :] = z_ref[...]
    pos = pos0 + i * tt + lax.broadcasted_iota(jnp.int32, (tt, 1), 0)
    group = wp_ref.shape[1]
    for g, w in enumerate(POOL_WINDOWS):
        cs = slice(g * group, (g + 1) * group)
        s = ext_ref[POOL_HALO:POOL_HALO + tt, cs]
        u = s
        for j in range(1, w):
            s = s + ext_ref[POOL_HALO - j:POOL_HALO - j + tt, cs]
        cnt = jnp.minimum(w, pos + 1).astype(F32)
        d = s / cnt - u
        y = jnp.dot(d.astype(BF16), wp_ref[g], preferred_element_type=F32) * sc_ref[:, cs]
        o_ref[:, cs] = y.astype(o_ref.dtype)


def pool_mix(z, prev, w_pool, l, scale, batch, t_len, pos0, tt):
    m = z.shape[0]
    c = prev.shape[2]
    nt = t_len // tt
    hb = tt // POOL_HALO
    return pl.pallas_call(
        functools.partial(_pool_kernel, tt=tt, pos0=pos0),
        grid=(batch, nt),
        in_specs=[pl.BlockSpec((tt, c), lambda b, i: (b * nt + i, 0)),
                  pl.BlockSpec((POOL_HALO, c), lambda b, i: (jnp.maximum((b * nt + i) * hb - 1, 0), 0)),
                  pl.BlockSpec((1, POOL_HALO, c), lambda b, i: (b, 0, 0)),
                  pl.BlockSpec((None,) + w_pool.shape[1:], lambda b, i: (l, 0, 0, 0)),
                  pl.BlockSpec((1, c), lambda b, i: (0, 0))],
        out_specs=pl.BlockSpec((tt, c), lambda b, i: (b * nt + i, 0)),
        out_shape=jax.ShapeDtypeStruct((m, c), BF16),
        scratch_shapes=[pltpu.VMEM((POOL_HALO + tt, c), F32)],
        compiler_params=_cparams("parallel", "parallel"), name="pool_mix",
    )(z, z, prev, w_pool, scale.reshape(1, c))


def _last_kv_block(i, tq, tkv, q0, s_len):
    last_q = q0 + (i + 1) * tq - 1
    visible = jnp.minimum((last_q // CHUNK + 1) * CHUNK, s_len)
    return (visible - 1) // tkv


def _mask_bias(q_start, tq, k_start, tkv, s_len):
    qpos = q_start + lax.broadcasted_iota(jnp.int32, (tq, 1), 0)
    kpos = k_start + lax.broadcasted_iota(jnp.int32, (1, tkv), 1)
    shift = CHUNK.bit_length() - 1
    ok = (jnp.right_shift(kpos, shift) <= jnp.right_shift(qpos, shift)) & (kpos < s_len)
    return jnp.where(ok, 0.0, -jnp.inf).astype(F32)


def _add_bias(s, bias, groups):
    rows, tkv = s.shape
    return (s.reshape(groups, rows // groups, tkv) + bias[None]).reshape(rows, tkv)


def _lane_tile(x, width):
    return x if width == LANES else jnp.concatenate([x] * (width // LANES), axis=1)


def _flash_update(s, v, m_ref, l_ref, acc_ref):
    tkv = s.shape[1]
    m_prev = m_ref[...]
    m_new = jnp.maximum(m_prev, jnp.max(s, axis=-1, keepdims=True))
    alpha = jnp.exp(m_prev - m_new)
    p = jnp.exp(s - _lane_tile(m_new, tkv))
    psum = p[:, :LANES]
    for c in range(1, tkv // LANES):
        psum = psum + p[:, c * LANES:(c + 1) * LANES]
    l_ref[...] = alpha * l_ref[...] + psum
    acc_ref[...] = (_lane_tile(alpha, acc_ref.shape[1]) * acc_ref[...]
                    + jnp.dot(p.astype(BF16), v, preferred_element_type=F32))
    m_ref[...] = m_new


def _flash_init(m_ref, l_ref, acc_ref):
    m_ref[...] = jnp.full(m_ref.shape, -jnp.inf, F32)
    l_ref[...] = jnp.zeros(l_ref.shape, F32)
    acc_ref[...] = jnp.zeros(acc_ref.shape, F32)


def _flash_result(l_ref, acc_ref):
    return acc_ref[...] / jnp.sum(l_ref[...], axis=-1, keepdims=True)


def _flash_scratch(groups, rows, dv):
    lead = () if groups is None else (groups,)
    return [pltpu.VMEM(lead + (rows, LANES), F32), pltpu.VMEM(lead + (rows, LANES), F32),
            pltpu.VMEM(lead + (rows, dv), F32)]


def _full_kv_blocks(i, tq, tkv, q0, s_len):
    first_q = q0 + i * tq
    return jnp.minimum((first_q // CHUNK + 1) * CHUNK, s_len) // tkv


def _with_ones(v):
    return jnp.concatenate([v, jnp.ones((v.shape[0], LANES), v.dtype)], axis=1)


def _flash_update_wide(s, v_ones, m_ref, acc_ref):
    tkv = s.shape[1]
    m_prev = m_ref[...]
    m_new = jnp.maximum(m_prev, jnp.max(s, axis=-1, keepdims=True))
    alpha = jnp.exp(m_prev - m_new)
    p = jnp.exp((s - _lane_tile(m_new, tkv)).astype(BF16))
    acc_ref[...] = (_lane_tile(alpha, acc_ref.shape[1]) * acc_ref[...]
                    + jnp.dot(p, v_ones, preferred_element_type=F32))
    m_ref[...] = m_new


def _flash_init_wide(m_ref, acc_ref):
    m_ref[...] = jnp.full(m_ref.shape, -jnp.inf, F32)
    acc_ref[...] = jnp.zeros(acc_ref.shape, F32)


def _flash_result_wide(acc_ref):
    dv = acc_ref.shape[-1] - LANES
    assert dv == LANES
    return acc_ref[:, 0:dv] / acc_ref[:, dv:dv + LANES]


def _flash_scratch_wide(groups, rows, dv):
    return [pltpu.VMEM((groups, rows, LANES), F32), pltpu.VMEM((groups, rows, dv + LANES), F32)]


def _past_tile(p_len, cap):
    tile = max(t for t in range(LANES, cap + 1, LANES) if p_len % t == 0)
    return tile


def _diff_split_q(q_ref, qs_ref, heads, tq):
    hd = 2 * DIFF_D
    lane = lax.broadcasted_iota(jnp.int32, (tq, hd), 1)
    for h in range(heads):
        q = q_ref[:, h * hd:(h + 1) * hd]
        qs_ref[h, 0:tq, :] = jnp.where(lane < DIFF_D, q, jnp.zeros_like(q))
        qs_ref[h, tq:2 * tq, :] = jnp.where(lane >= DIFF_D, q, jnp.zeros_like(q))


def _heads_sweep(q_of, k_of, v_of, bias, groups, m_ref, acc_ref, heads):
    for h in range(heads):
        s = lax.dot_general(q_of(h), k_of(h), _NT, preferred_element_type=F32)
        if bias is not None:
            s = s + bias if groups == 1 else _add_bias(s, bias, groups)
        _flash_update_wide(s, _with_ones(v_of(h)), m_ref.at[h], acc_ref.at[h])


def _diff_finish(lam_ref, g_ref, o_ref, acc_ref, heads, tq, out_scale):
    hd = 2 * DIFF_D
    lam = lam_ref[0, 0]
    for h in range(heads):
        a = _flash_result_wide(acc_ref.at[h])
        o = a[0:tq] - lam * a[tq:2 * tq]
        o_ref[:, h * hd:(h + 1) * hd] = (_rms(o, g_ref[...]) * out_scale).astype(o_ref.dtype)


def _diff_kernel(lam_ref, q_ref, k_ref, v_ref, g_ref, o_ref, qs_ref, m_ref, acc_ref, *,
                 heads, tq, tkv, s_len, out_scale):
    i, j = pl.program_id(1), pl.program_id(2)
    hd = 2 * DIFF_D
    k_of = lambda h: k_ref[:, h * hd:(h + 1) * hd]
    v_of = lambda h: v_ref[:, h * hd:(h + 1) * hd]
    full = _full_kv_blocks(i, tq, tkv, 0, s_len)

    @pl.when(j == 0)
    def _():
        _flash_init_wide(m_ref, acc_ref)
        _diff_split_q(q_ref, qs_ref, heads, tq)

    @pl.when(j < full)
    def _():
        _heads_sweep(lambda h: qs_ref[h], k_of, v_of, None, 2, m_ref, acc_ref, heads)

    @pl.when((j >= full) & (j <= _last_kv_block(i, tq, tkv, 0, s_len)))
    def _():
        _heads_sweep(lambda h: qs_ref[h], k_of, v_of, _mask_bias(i * tq, tq, j * tkv, tkv, s_len), 2,
                     m_ref, acc_ref, heads)

    @pl.when(j == pl.num_programs(2) - 1)
    def _():
        _diff_finish(lam_ref, g_ref, o_ref, acc_ref, heads, tq, out_scale)


def diff_attention(q, k, v, lam, g_sub, batch, heads, t_len, out_scale, tq, tkv):
    hd = 2 * DIFF_D
    w = heads * hd
    nq, nkv = t_len // tq, t_len // tkv
    last = functools.partial(_last_kv_block, tq=tq, tkv=tkv, q0=0, s_len=t_len)
    kvmap = lambda b, i, j: (b * nkv + jnp.minimum(j, last(i)), 0)
    qmap = lambda b, i, j: (b * nq + i, 0)
    return pl.pallas_call(
        functools.partial(_diff_kernel, heads=heads, tq=tq, tkv=tkv, s_len=t_len, out_scale=out_scale),
        grid=(batch, nq, nkv),
        in_specs=[pl.BlockSpec(memory_space=pltpu.SMEM),
                  pl.BlockSpec((tq, w), qmap),
                  pl.BlockSpec((tkv, w), kvmap),
                  pl.BlockSpec((tkv, w), kvmap),
                  pl.BlockSpec((1, hd), lambda b, i, j: (0, 0))],
        out_specs=pl.BlockSpec((tq, w), qmap),
        out_shape=jax.ShapeDtypeStruct((batch * t_len, w), BF16),
        scratch_shapes=[pltpu.VMEM((heads, 2 * tq, hd), BF16)] + _flash_scratch_wide(heads, 2 * tq, hd),
        compiler_params=_cparams("parallel", "parallel", "arbitrary"), name="diff_attention",
    )(lam.reshape(1, 1), q, k, v, g_sub.reshape(1, hd))


def _diff_past_kernel(lam_ref, q_ref, kc_ref, vc_ref, kn_ref, vn_ref, g_ref, o_ref,
                      qs_ref, m_ref, acc_ref, *, heads, tq, tkv, p_len, s_len, out_scale):
    j = pl.program_id(1)
    n_past = pl.num_programs(1) - 1
    hd = 2 * DIFF_D

    @pl.when(j == 0)
    def _():
        _flash_init_wide(m_ref, acc_ref)
        _diff_split_q(q_ref, qs_ref, heads, tq)

    @pl.when(j < n_past)
    def _():
        _heads_sweep(lambda h: qs_ref[h], lambda h: kc_ref[h], lambda h: vc_ref[h],
                     _mask_bias(p_len, tq, j * tkv, tkv, s_len), 2, m_ref, acc_ref, heads)

    @pl.when(j == n_past)
    def _():
        _heads_sweep(lambda h: qs_ref[h], lambda h: kn_ref[:, h * hd:(h + 1) * hd],
                     lambda h: vn_ref[:, h * hd:(h + 1) * hd],
                     _mask_bias(p_len, tq, p_len, kn_ref.shape[0], s_len), 2, m_ref, acc_ref, heads)
        _diff_finish(lam_ref, g_ref, o_ref, acc_ref, heads, tq, out_scale)


def diff_attention_past(q, kc, vc, kn, vn, l, lam, g_sub, batch, heads, t_len, out_scale):
    hd = 2 * DIFF_D
    w = heads * hd
    p_len = kc.shape[3]
    tn = kn.shape[0] // batch
    tkv = _past_tile(p_len, 1024)
    n_past = p_len // tkv
    cache = pl.BlockSpec((None, None, heads, tkv, hd), lambda b, j: (l, b, 0, jnp.minimum(j, n_past - 1), 0))
    new = pl.BlockSpec((tn, w), lambda b, j: (b, 0))
    qmap = lambda b, j: (b, 0)
    return pl.pallas_call(
        functools.partial(_diff_past_kernel, heads=heads, tq=t_len, tkv=tkv, p_len=p_len, s_len=p_len + t_len,
                          out_scale=out_scale),
        grid=(batch, n_past + 1),
        in_specs=[pl.BlockSpec(memory_space=pltpu.SMEM), pl.BlockSpec((t_len, w), qmap), cache, cache, new, new,
                  pl.BlockSpec((1, hd), lambda b, j: (0, 0))],
        out_specs=pl.BlockSpec((t_len, w), qmap),
        out_shape=jax.ShapeDtypeStruct((batch * t_len, w), BF16),
        scratch_shapes=[pltpu.VMEM((heads, 2 * t_len, hd), BF16)] + _flash_scratch_wide(heads, 2 * t_len, hd),
        compiler_params=_cparams("parallel", "arbitrary"), name="diff_attention_past",
    )(lam.reshape(1, 1), q, kc, vc, kn, vn, g_sub.reshape(1, hd))


MLA_KEY_W = 2 * MLA_NOPE


def _mla_kernel(q_ref, k_ref, v_ref, o_ref, m_ref, acc_ref, *, heads, tq, tkv, s_len):
    i, j = pl.program_id(1), pl.program_id(2)
    q_of = lambda h: q_ref[:, h * MLA_KEY_W:(h + 1) * MLA_KEY_W]
    k_of = lambda h: k_ref[:, h * MLA_KEY_W:(h + 1) * MLA_KEY_W]
    v_of = lambda h: v_ref[:, h * MLA_NOPE:(h + 1) * MLA_NOPE]
    full = _full_kv_blocks(i, tq, tkv, 0, s_len)

    @pl.when(j == 0)
    def _():
        _flash_init_wide(m_ref, acc_ref)

    @pl.when(j < full)
    def _():
        _heads_sweep(q_of, k_of, v_of, None, 1, m_ref, acc_ref, heads)

    @pl.when((j >= full) & (j <= _last_kv_block(i, tq, tkv, 0, s_len)))
    def _():
        _heads_sweep(q_of, k_of, v_of, _mask_bias(i * tq, tq, j * tkv, tkv, s_len), 1, m_ref, acc_ref, heads)

    @pl.when(j == pl.num_programs(2) - 1)
    def _():
        for h in range(heads):
            o_ref[:, h * MLA_NOPE:(h + 1) * MLA_NOPE] = _flash_result_wide(acc_ref.at[h]).astype(o_ref.dtype)


def mla_attention(q, kv_up, batch, heads, t_len, tq, tkv):
    kw, vw = heads * MLA_KEY_W, heads * MLA_NOPE
    nq, nkv = t_len // tq, t_len // tkv
    last = functools.partial(_last_kv_block, tq=tq, tkv=tkv, q0=0, s_len=t_len)
    kvrow = lambda b, i, j: b * nkv + jnp.minimum(j, last(i))
    return pl.pallas_call(
        functools.partial(_mla_kernel, heads=heads, tq=tq, tkv=tkv, s_len=t_len),
        grid=(batch, nq, nkv),
        in_specs=[pl.BlockSpec((tq, kw), lambda b, i, j: (b * nq + i, 0)),
                  pl.BlockSpec((tkv, kw), lambda b, i, j: (kvrow(b, i, j), 0)),
                  pl.BlockSpec((tkv, vw), lambda b, i, j: (kvrow(b, i, j), kw // vw))],
        out_specs=pl.BlockSpec((tq, vw), lambda b, i, j: (b * nq + i, 0)),
        out_shape=jax.ShapeDtypeStruct((batch * t_len, vw), BF16),
        scratch_shapes=_flash_scratch_wide(heads, tq, MLA_NOPE),
        compiler_params=_cparams("parallel", "parallel", "arbitrary"), name="mla_attention",
    )(q, kv_up, kv_up)


def _mla_past_kernel(q_ref, wuk_ref, ckvc_ref, kpec_ref, ckvn_ref, kpen_ref, wuv_ref,
                     o_ref, qlat_ref, qpe_ref, m_ref, l_ref, acc_ref, *, heads, tq, tkv, p_len, s_len):
    j = pl.program_id(1)
    n_past = pl.num_programs(1) - 1

    @pl.when(j == 0)
    def _():
        _flash_init(m_ref, l_ref, acc_ref)
        for h in range(heads):
            lo = h * MLA_KEY_W
            qlat = jnp.dot(q_ref[:, lo:lo + MLA_NOPE], wuk_ref[h], preferred_element_type=F32)
            qlat_ref[h * tq:(h + 1) * tq, :] = qlat.astype(BF16)
            qpe_ref[h * tq:(h + 1) * tq, :] = q_ref[:, lo + MLA_NOPE:lo + MLA_NOPE + MLA_ROPE]

    def sweep(ckv, kpe, k_start):
        s = (lax.dot_general(qlat_ref[...], ckv, _NT, preferred_element_type=F32)
             + lax.dot_general(qpe_ref[...], kpe, _NT, preferred_element_type=F32))
        s = _add_bias(s, _mask_bias(p_len, tq, k_start, ckv.shape[0], s_len), heads)
        _flash_update(s, ckv, m_ref, l_ref, acc_ref)

    @pl.when(j < n_past)
    def _():
        sweep(ckvc_ref[...].astype(BF16), kpec_ref[...].astype(BF16), j * tkv)

    @pl.when(j == n_past)
    def _():
        sweep(ckvn_ref[...], kpen_ref[...], p_len)
        v_w = wuv_ref.shape[2]
        o_lat = _flash_result(l_ref, acc_ref).astype(BF16)
        for h in range(heads):
            o_h = jnp.dot(o_lat[h * tq:(h + 1) * tq], wuv_ref[h], preferred_element_type=F32)
            o_ref[:, h * v_w:(h + 1) * v_w] = o_h.astype(o_ref.dtype)


def mla_attention_past(q, wuk_t, wuv, l, ckv_c, kpe_c, ckv_n, kpe_n, batch, t_len):
    _, heads, _, rank = wuk_t.shape
    v_w = wuv.shape[3]
    p_len = ckv_c.shape[2]
    tn = ckv_n.shape[0] // batch
    tkv = _past_tile(p_len, 1024)
    n_past = p_len // tkv
    rows = heads * t_len
    layer4 = lambda a: pl.BlockSpec((None,) + a.shape[1:], lambda b, j: (l, 0, 0, 0))
    cache = lambda w: pl.BlockSpec((None, None, tkv, w), lambda b, j: (l, b, jnp.minimum(j, n_past - 1), 0))
    new = lambda w: pl.BlockSpec((tn, w), lambda b, j: (b, 0))
    return pl.pallas_call(
        functools.partial(_mla_past_kernel, heads=heads, tq=t_len, tkv=tkv, p_len=p_len, s_len=p_len + t_len),
        grid=(batch, n_past + 1),
        in_specs=[pl.BlockSpec((t_len, q.shape[1]), lambda b, j: (b, 0)), layer4(wuk_t),
                  cache(rank), cache(MLA_ROPE), new(rank), new(MLA_ROPE), layer4(wuv)],
        out_specs=pl.BlockSpec((t_len, heads * v_w), lambda b, j: (b, 0)),
        out_shape=jax.ShapeDtypeStruct((batch * t_len, heads * v_w), BF16),
        scratch_shapes=[pltpu.VMEM((rows, rank), BF16), pltpu.VMEM((rows, MLA_ROPE), BF16)]
                       + _flash_scratch(None, rows, rank),
        compiler_params=_cparams("parallel", "arbitrary"), name="mla_attention_past",
    )(q, wuk_t, ckv_c, kpe_c, ckv_n, kpe_n, wuv)


def _cross_block_kernel(y_ref, x_ref, g_mix_ref, g_pre_ref, wq_ref, k_ref, v_ref, wo_ref, g_post_ref, g_next_ref,
                        xo_ref, ho_ref, *, heads, hd):
    x1 = x_ref[...] + _rms(y_ref[...], g_mix_ref[...])
    q = jnp.dot(_rms(x1, g_pre_ref[...]).astype(BF16), wq_ref[...], preferred_element_type=F32).astype(BF16)
    outs = []
    for h in range(heads):
        sl = slice(h * hd, (h + 1) * hd)
        s = lax.dot_general(q[:, sl], k_ref[:, sl], _NT, preferred_element_type=F32) * (hd ** -0.5)
        p = jnp.exp(s - jnp.max(s, axis=-1, keepdims=True))
        o = jnp.dot(p.astype(BF16), v_ref[:, sl], preferred_element_type=F32)
        outs.append((o / jnp.sum(p, axis=-1, keepdims=True)).astype(BF16))
    y2 = jnp.dot(jnp.concatenate(outs, axis=1), wo_ref[...], preferred_element_type=F32)
    x2 = x1 + _rms(y2, g_post_ref[...])
    xo_ref[...] = x2
    ho_ref[...] = _rms(x2, g_next_ref[...]).astype(ho_ref.dtype)


def cross_block(y_mix, x, g_mix_post, g_pre, wq, mk, mv, wo, l, g_post, g_next, batch, t_len, mem_len, heads, tm):
    m, d = x.shape
    w = wq.shape[2]
    nq = t_len // tm
    row = pl.BlockSpec((tm, d), lambda b, i: (b * nq + i, 0))
    vec = pl.BlockSpec((1, d), lambda b, i: (0, 0))
    kv = pl.BlockSpec((mem_len, w), lambda b, i: (b, 0))
    once = pl.Buffered(1)
    return pl.pallas_call(
        functools.partial(_cross_block_kernel, heads=heads, hd=w // heads),
        grid=(batch, nq),
        in_specs=[row, row, vec, vec,
                  pl.BlockSpec((None, d, w), lambda b, i: (l, 0, 0), pipeline_mode=once), kv, kv,
                  pl.BlockSpec((None, w, d), lambda b, i: (l, 0, 0), pipeline_mode=once), vec, vec],
        out_specs=[row, row],
        out_shape=[jax.ShapeDtypeStruct((m, d), F32), jax.ShapeDtypeStruct((m, d), BF16)],
        compiler_params=_cparams("parallel", "parallel"), name="cross_block",
    )(y_mix, x, g_mix_post.reshape(1, d), g_pre.reshape(1, d), wq, mk, mv, wo,
      g_post.reshape(1, d), g_next.reshape(1, d))


def _rope_tables(pos):
    half = DIFF_D // 2
    inv = ROPE_THETA ** (-jnp.arange(half, dtype=F32) / half)
    ang = pos.astype(F32)[:, None] * inv[None, :]
    cos, sin, zero = jnp.cos(ang), jnp.sin(ang), jnp.zeros_like(ang)
    reps = LANES // DIFF_D
    return (jnp.tile(cos, (1, 2 * reps)), jnp.tile(jnp.concatenate([-sin, zero], 1), (1, reps)),
            jnp.tile(jnp.concatenate([zero, sin], 1), (1, reps)))


def _round_up(a, b):
    return -(-a // b) * b


def _kv_up_weight(w_uk, w_uv):
    depth, rank, heads, nope = w_uk.shape
    k_rows = jnp.concatenate([w_uk, jnp.zeros_like(w_uk)], axis=3).reshape(depth, rank, heads * MLA_KEY_W)
    slot = jnp.concatenate([jnp.zeros((MLA_ROPE, nope), F32), jnp.eye(MLA_ROPE, dtype=F32),
                            jnp.zeros((MLA_ROPE, MLA_KEY_W - nope - MLA_ROPE), F32)], axis=1)
    rope_rows = jnp.broadcast_to(jnp.tile(slot, (1, heads)), (depth, MLA_ROPE, heads * MLA_KEY_W))
    pad = LANES - MLA_ROPE
    w_k = jnp.concatenate([k_rows, rope_rows, jnp.zeros((depth, pad, heads * MLA_KEY_W), F32)], axis=1)
    w_v = jnp.concatenate([w_uv.reshape(depth, rank, -1), jnp.zeros((depth, LANES, heads * w_uv.shape[3]), F32)],
                          axis=1)
    return jnp.concatenate([w_k, w_v], axis=2).astype(BF16)


def _pad_rows(a, batch, rows):
    t_len = a.shape[0] // batch
    a = jnp.pad(a.reshape(batch, t_len, a.shape[1]), ((0, 0), (0, rows - t_len), (0, 0)))
    return a.reshape(batch * rows, a.shape[2])


def _mixer_and_memory(wts, l, x, h, batch, t_len, pos0, tabs, mem_k, mem_v, past, tiles, state):
    tm, tt, tq_diff, tkv_diff, tq_mla, tkv_mla, tq_x = tiles
    lam_init = 0.8 - 0.6 * math.exp(-0.3 * l)
    pool_w, diff_w, q_rank, kv_rank = wts["dims"]
    heads_d = diff_w // (2 * DIFF_D)
    heads_m = wts["w_uk_t"].shape[1]

    (u, qd, kd16, vd16, cqn, ckv16, kpe16, kc16), state = input_projection(
        h, wts["w_in"], tabs, wts["g_mla_q"][l], wts["g_mla_kv"][l], wts["dims"],
        min(tm, 512), tm, l, wts["depth"], state)

    keep = POOL_HALO - 1
    if past is None:
        prev = jnp.zeros((batch, POOL_HALO, pool_w), F32)
    else:
        prev = jnp.concatenate([jnp.zeros((batch, 1, pool_w), F32), past["pool"]], axis=1)
    u3 = u.reshape(batch, t_len, pool_w)
    if t_len >= keep:
        new_pool = u3[:, t_len - keep:]
    else:
        new_pool = jnp.concatenate([prev[:, 1 + t_len:], u3], axis=1)
    y_pool = pool_mix(u, prev, wts["w_pool"], l, wts["pool_scale"][l], batch, t_len, pos0, tt)

    lq = wts["diff_lambda"][l]
    lam = jnp.exp(jnp.sum(lq[0] * lq[1])) - jnp.exp(jnp.sum(lq[2] * lq[3])) + lam_init
    q = mla_query_projection(cqn, wts["w_uq"], tabs, l, tm, 1024)
    if past is None:
        o_diff = diff_attention(qd, kd16, vd16, lam, wts["g_diff_sub"][l], batch, heads_d, t_len,
                                1.0 - lam_init, tq_diff, tkv_diff)
        kv_up = matmul(kc16, wts["w_kv_up"], l, BF16, tm, heads_m * MLA_NOPE)
        o_mla = mla_attention(q, kv_up, batch, heads_m, t_len, tq_mla, tkv_mla)
    else:
        tn = _round_up(t_len, LANES)
        o_diff = diff_attention_past(qd, past["diff_k"], past["diff_v"], _pad_rows(kd16, batch, tn),
                                     _pad_rows(vd16, batch, tn), l, lam, wts["g_diff_sub"][l], batch, heads_d,
                                     t_len, 1.0 - lam_init)
        o_mla = mla_attention_past(q, wts["w_uk_t"], wts["w_uv"], l, past["ckv"], past["kpe"],
                                   _pad_rows(ckv16, batch, tn), _pad_rows(kpe16, batch, tn), batch, t_len)

    y = matmul_cat([y_pool, o_diff, o_mla], wts["w_out"], l, F32, tm, 1024)
    x, h = cross_block(y, x, wts["g_mix_post"][l], wts["g_x_pre"][l], wts["w_mem_q"], mem_k, mem_v,
                       wts["w_mem_o"], l, wts["g_x_post"][l], wts["g_ff_pre"][l], batch, t_len,
                       mem_k.shape[0] // batch, wts["mem_heads"], tq_x)
    return x, h, state, new_pool


def _ffn_pair(wts, l, x, h, xs, hs, g_next, tiles, tiles_s):
    a, a_s = matmul_swiglu_pair(h, hs, wts["w_gate"], wts["w_up"], l, 2 * tiles[0], 256)
    y, y_s = matmul_pair(a, a_s, wts["w_down"], l, F32, 512, 512)
    x, h = post_res(y, x, wts["g_ff_post"][l], g_next, tiles[1])
    xs, hs = post_res(y_s, xs, wts["g_ff_post"][l], g_next, tiles_s[1])
    return x, h, xs, hs


def kernel(x_prompt, x_sample, cache_diff_k, cache_diff_v, cache_mla_ckv, cache_mla_kpe, cache_pool, cache_mem_k, cache_mem_v, mem_prompt, g_mix_pre, w_in, w_pool, pool_scale, diff_lambda, g_diff_sub, g_mla_q, w_mla_uq, w_mla_uk, w_mla_uv, g_mla_kv, w_out, g_mix_post, g_mem, w_mem_k, w_mem_v, w_mem_q, w_mem_o, g_x_pre, g_x_post, g_ff_pre, w_gate, w_up, w_down, g_ff_post):
    depth = w_in.shape[0]
    bp, tp, d = x_prompt.shape
    bs, ts, _ = x_sample.shape
    past_len = cache_mla_ckv.shape[2]
    pool_w = cache_pool.shape[3]
    heads_d, diff_w = cache_diff_k.shape[3], cache_diff_k.shape[3] * cache_diff_k.shape[4]
    q_rank, kv_rank = g_mla_q.shape[1], g_mla_kv.shape[1]
    mla_heads = w_mla_uk.shape[2]
    mem_len, mem_heads, mem_hd = cache_mem_k.shape[2:]
    mem_w = mem_heads * mem_hd

    uq = w_mla_uq.reshape(depth, q_rank, mla_heads, MLA_NOPE + MLA_ROPE)
    wts = {
        "depth": depth, "dims": (pool_w, diff_w, q_rank, kv_rank), "mem_heads": mem_heads,
        "w_in": w_in.astype(BF16),
        "w_pool": w_pool.astype(BF16), "pool_scale": pool_scale, "diff_lambda": diff_lambda,
        "g_diff_sub": g_diff_sub, "g_mla_q": g_mla_q, "g_mla_kv": g_mla_kv,
        "w_uq": jnp.pad(uq, ((0, 0), (0, 0), (0, 0), (0, MLA_KEY_W - MLA_NOPE - MLA_ROPE))).reshape(
            depth, q_rank, mla_heads * MLA_KEY_W).astype(BF16),
        "w_uk_t": jnp.transpose(w_mla_uk, (0, 2, 3, 1)).astype(BF16),
        "w_uv": jnp.transpose(w_mla_uv, (0, 2, 1, 3)).astype(BF16),
        "w_kv_up": _kv_up_weight(w_mla_uk, w_mla_uv),
        "w_out": w_out.astype(BF16), "g_mix_post": g_mix_post,
        "w_mem_q": w_mem_q.astype(BF16), "w_mem_o": w_mem_o.astype(BF16),
        "g_x_pre": g_x_pre, "g_x_post": g_x_post, "g_ff_pre": g_ff_pre,
        "w_gate": w_gate, "w_up": w_up, "w_down": w_down.astype(BF16),
        "g_ff_post": g_ff_post,
    }
    w_mem_kv = jnp.concatenate([w_mem_k, w_mem_v], axis=2).astype(BF16)

    tabs_p = _rope_tables(jnp.arange(tp))
    tabs_s = tuple(jnp.tile(t, (bs, 1)) for t in _rope_tables(past_len + jnp.arange(ts)))
    tiles_p = (1024, 256, 512, 512, 512, 512, 256)
    tiles_s = (bs * ts, ts, ts, 0, ts, 0, ts)

    xp = x_prompt.reshape(bp * tp, d)
    xs = x_sample.reshape(bs * ts, d)
    hp = rmsnorm_bf16(xp, g_mix_pre[0], tiles_p[1])
    hs = rmsnorm_bf16(xs, g_mix_pre[0], tiles_s[1])
    mem_n = mem_prompt.reshape(bp * mem_len, d)
    mem_k_s = cache_mem_k.reshape(depth, bs * mem_len, mem_w).astype(BF16)
    mem_v_s = cache_mem_v.reshape(depth, bs * mem_len, mem_w).astype(BF16)
    past = {"diff_k": jnp.transpose(cache_diff_k, (0, 1, 3, 2, 4)).astype(BF16),
            "diff_v": jnp.transpose(cache_diff_v, (0, 1, 3, 2, 4)).astype(BF16),
            "ckv": cache_mla_ckv, "kpe": cache_mla_kpe}

    state_p = state_s = None
    pools_p, pools_s, mem_ks, mem_vs = [], [], [], []
    for l in range(depth):
        g_next = g_mix_pre[l + 1] if l + 1 < depth else None
        m = rmsnorm_bf16(mem_n, g_mem[l], 256)
        mkv = matmul(m, w_mem_kv, l, F32, 1024, 1024)
        mk, mv = mkv[:, :mem_w], mkv[:, mem_w:]
        mem_ks.append(mk.reshape(bp, mem_len, mem_heads, mem_hd))
        mem_vs.append(mv.reshape(bp, mem_len, mem_heads, mem_hd))
        xp, hp, state_p, pool_p = _mixer_and_memory(wts, l, xp, hp, bp, tp, 0, tabs_p, mk.astype(BF16),
                                                    mv.astype(BF16), None, tiles_p, state_p)
        pools_p.append(pool_p)
        xs, hs, state_s, pool_s = _mixer_and_memory(wts, l, xs, hs, bs, ts, past_len, tabs_s, mem_k_s[l],
                                                    mem_v_s[l], dict(past, pool=cache_pool[l]), tiles_s, state_s)
        pools_s.append(pool_s)
        xp, hp, xs, hs = _ffn_pair(wts, l, xp, hp, xs, hs, g_next, tiles_p, tiles_s)

    hd = 2 * DIFF_D
    shape_p = lambda a, tail: a.reshape((depth, bp, tp) + tail)
    shape_s = lambda a, tail: a.reshape((depth, bs, ts) + tail)
    return (xp.reshape(bp, tp, d), xs.reshape(bs, ts, d),
            shape_p(state_p[0], (heads_d, hd)), shape_p(state_p[1], (heads_d, hd)),
            shape_p(state_p[2], (kv_rank,)), shape_p(state_p[3], (MLA_ROPE,)),
            jnp.stack(pools_p), jnp.stack(mem_ks), jnp.stack(mem_vs),
            shape_s(state_s[0], (heads_d, hd)), shape_s(state_s[1], (heads_d, hd)),
            shape_s(state_s[2], (kv_rank,)), shape_s(state_s[3], (MLA_ROPE,)),
            jnp.stack(pools_s))
```

```python
import functools
import math

import jax
import jax.numpy as jnp
from jax import lax
from jax.experimental import pallas as pl
from jax.experimental.pallas import tpu as pltpu

F32 = jnp.float32
BF16 = jnp.bfloat16

EPS = 1e-6
CHUNK = 64
ROPE_THETA = 10000.0
POOL_WINDOWS = (2, 4, 8, 16)
POOL_HALO = 16
DIFF_D = 64
MLA_NOPE = 128
MLA_ROPE = 64
LANES = 128
MIB = 1024 * 1024
VMEM_LIMIT = 56 * MIB

_NT = (((1,), (1,)), ((), ()))


def _cparams(*sem):
    return pltpu.CompilerParams(dimension_semantics=sem, vmem_limit_bytes=VMEM_LIMIT)


def _rms(x, g):
    return x * lax.rsqrt(jnp.mean(x * x, axis=-1, keepdims=True) + EPS) * g


def _rmsnorm_kernel(x_ref, g_ref, o_ref):
    o_ref[...] = _rms(x_ref[...], g_ref[...]).astype(o_ref.dtype)


def rmsnorm_bf16(x, g, tm):
    m, d = x.shape
    return pl.pallas_call(
        _rmsnorm_kernel,
        grid=(m // tm,),
        in_specs=[pl.BlockSpec((tm, d), lambda i: (i, 0)), pl.BlockSpec((1, d), lambda i: (0, 0))],
        out_specs=pl.BlockSpec((tm, d), lambda i: (i, 0)),
        out_shape=jax.ShapeDtypeStruct((m, d), BF16),
        compiler_params=_cparams("parallel"),
        name="rmsnorm",
    )(x, g.reshape(1, d))


def _post_res_kernel(y_ref, x_ref, gp_ref, gn_ref, xo_ref, ho_ref):
    x = x_ref[...] + _rms(y_ref[...], gp_ref[...])
    xo_ref[...] = x
    ho_ref[...] = _rms(x, gn_ref[...]).astype(ho_ref.dtype)


def _post_res_last_kernel(y_ref, x_ref, gp_ref, xo_ref):
    xo_ref[...] = x_ref[...] + _rms(y_ref[...], gp_ref[...])


def post_res(y, x, g_post, g_next, tm):
    m, d = x.shape
    row = pl.BlockSpec((tm, d), lambda i: (i, 0))
    vec = pl.BlockSpec((1, d), lambda i: (0, 0))
    if g_next is None:
        return pl.pallas_call(
            _post_res_last_kernel, grid=(m // tm,), in_specs=[row, row, vec], out_specs=row,
            out_shape=jax.ShapeDtypeStruct((m, d), F32), compiler_params=_cparams("parallel"),
            name="post_res_last",
        )(y, x, g_post.reshape(1, d)), None
    return pl.pallas_call(
        _post_res_kernel, grid=(m // tm,), in_specs=[row, row, vec, vec], out_specs=[row, row],
        out_shape=[jax.ShapeDtypeStruct((m, d), F32), jax.ShapeDtypeStruct((m, d), BF16)],
        compiler_params=_cparams("parallel"), name="post_res",
    )(y, x, g_post.reshape(1, d), g_next.reshape(1, d))


def _mm_kernel(x_ref, w_ref, o_ref):
    o_ref[...] = jnp.dot(x_ref[...], w_ref[...], preferred_element_type=F32).astype(o_ref.dtype)


def matmul(x, w, l, out_dtype, tm, tn):
    m, k = x.shape
    n = w.shape[2]
    tm, tn = min(tm, m), min(tn, n)
    return pl.pallas_call(
        _mm_kernel, grid=(n // tn, m // tm),
        in_specs=[pl.BlockSpec((tm, k), lambda j, i: (i, 0)), pl.BlockSpec((None, k, tn), lambda j, i: (l, 0, j))],
        out_specs=pl.BlockSpec((tm, tn), lambda j, i: (i, j)),
        out_shape=jax.ShapeDtypeStruct((m, n), out_dtype),
        compiler_params=_cparams("parallel", "parallel"), name="matmul",
    )(x, w)


def _mm_cat_kernel(*refs):
    n_in = (len(refs) - 1) // 2
    acc = jnp.dot(refs[0][...], refs[n_in][...], preferred_element_type=F32)
    for a in range(1, n_in):
        acc += jnp.dot(refs[a][...], refs[n_in + a][...], preferred_element_type=F32)
    refs[-1][...] = acc.astype(refs[-1].dtype)


def matmul_cat(xs, w, l, out_dtype, tm, tn):
    m = xs[0].shape[0]
    n = w.shape[2]
    tm, tn = min(tm, m), min(tn, n)
    x_specs, w_specs, off = [], [], 0
    for x in xs:
        k = x.shape[1]
        assert off % k == 0, "each input's row band of w must start on a multiple of its own width"
        x_specs.append(pl.BlockSpec((tm, k), lambda j, i: (i, 0)))
        w_specs.append(pl.BlockSpec((None, k, tn), functools.partial(lambda j, i, r: (l, r, j), r=off // k)))
        off += k
    return pl.pallas_call(
        _mm_cat_kernel, grid=(n // tn, m // tm),
        in_specs=x_specs + w_specs,
        out_specs=pl.BlockSpec((tm, tn), lambda j, i: (i, j)),
        out_shape=jax.ShapeDtypeStruct((m, n), out_dtype),
        compiler_params=_cparams("parallel", "parallel"), name="matmul_cat",
    )(*xs, *([w] * len(xs)))


def _swiglu_kernel(x_ref, wg_ref, wu_ref, o_ref):
    x = x_ref[...]
    g = jnp.dot(x, wg_ref[...].astype(BF16), preferred_element_type=F32)
    u = jnp.dot(x, wu_ref[...].astype(BF16), preferred_element_type=F32)
    o_ref[...] = (g * jax.nn.sigmoid(g) * u).astype(o_ref.dtype)


def matmul_swiglu(x, wg, wu, l, tm, tn):
    m, k = x.shape
    n = wg.shape[2]
    tm = min(tm, m)
    wspec = pl.BlockSpec((None, k, tn), lambda i, j: (l, 0, j))
    return pl.pallas_call(
        _swiglu_kernel, grid=(m // tm, n // tn),
        in_specs=[pl.BlockSpec((tm, k), lambda i, j: (i, 0), pipeline_mode=pl.Buffered(1)), wspec, wspec],
        out_specs=pl.BlockSpec((tm, tn), lambda i, j: (i, j)),
        out_shape=jax.ShapeDtypeStruct((m, n), BF16),
        compiler_params=_cparams("parallel", "parallel"), name="matmul_swiglu",
    )(x, wg, wu)


def _swiglu_pair_kernel(x_ref, wg_ref, wu_ref, xs_ref, o_ref, os_ref):
    wg, wu = wg_ref[...].astype(BF16), wu_ref[...].astype(BF16)

    def gated(x):
        g = jnp.dot(x, wg, preferred_element_type=F32)
        u = jnp.dot(x, wu, preferred_element_type=F32)
        return (g * jax.nn.sigmoid(g) * u).astype(BF16)

    o_ref[...] = gated(x_ref[...])

    @pl.when(pl.program_id(0) == 0)
    def _():
        os_ref[...] = gated(xs_ref[...])


def matmul_swiglu_pair(x, xs, wg, wu, l, tm, tn):
    m, k = x.shape
    ms = xs.shape[0]
    n = wg.shape[2]
    tm = min(tm, m)
    nj = n // tn
    wspec = pl.BlockSpec((None, k, tn), lambda i, j: (l, 0, j))
    once = pl.Buffered(1)
    return pl.pallas_call(
        _swiglu_pair_kernel, grid=(m // tm, nj),
        in_specs=[pl.BlockSpec((tm, k), lambda i, j: (i, 0), pipeline_mode=once), wspec, wspec,
                  pl.BlockSpec((ms, k), lambda i, j: (0, 0), pipeline_mode=once)],
        out_specs=[pl.BlockSpec((tm, tn), lambda i, j: (i, j)),
                   pl.BlockSpec((ms, tn), lambda i, j: (0, jnp.where(i == 0, j, nj - 1)))],
        out_shape=[jax.ShapeDtypeStruct((m, n), BF16), jax.ShapeDtypeStruct((ms, n), BF16)],
        compiler_params=_cparams("arbitrary", "arbitrary"), name="matmul_swiglu_pair",
    )(x, wg, wu, xs)


def _mm_pair_kernel(x_ref, w_ref, xs_ref, o_ref, os_ref):
    o_ref[...] = jnp.dot(x_ref[...], w_ref[...], preferred_element_type=F32).astype(o_ref.dtype)

    @pl.when(pl.program_id(1) == 0)
    def _():
        os_ref[...] = jnp.dot(xs_ref[...], w_ref[...], preferred_element_type=F32).astype(os_ref.dtype)


def matmul_pair(x, xs, w, l, out_dtype, tm, tn):
    m, k = x.shape
    ms = xs.shape[0]
    n = w.shape[2]
    tm, tn = min(tm, m), min(tn, n)
    return pl.pallas_call(
        _mm_pair_kernel, grid=(n // tn, m // tm),
        in_specs=[pl.BlockSpec((tm, k), lambda j, i: (i, 0)), pl.BlockSpec((None, k, tn), lambda j, i: (l, 0, j)),
                  pl.BlockSpec((ms, k), lambda j, i: (0, 0), pipeline_mode=pl.Buffered(1))],
        out_specs=[pl.BlockSpec((tm, tn), lambda j, i: (i, j)), pl.BlockSpec((ms, tn), lambda j, i: (0, j))],
        out_shape=[jax.ShapeDtypeStruct((m, n), out_dtype), jax.ShapeDtypeStruct((ms, n), out_dtype)],
        compiler_params=_cparams("arbitrary", "arbitrary"), name="matmul_pair",
    )(x, w, xs)


def _q_proj_kernel(x_ref, w_ref, cos_ref, sa_ref, sb_ref, o_ref, *, scale):
    acc = jnp.dot(x_ref[...], w_ref[...], preferred_element_type=F32)
    cos, sa, sb = cos_ref[...], sa_ref[...], sb_ref[...]
    for s in range(acc.shape[1] // MLA_KEY_W):
        lo, mid, hi = s * MLA_KEY_W, s * MLA_KEY_W + MLA_NOPE, (s + 1) * MLA_KEY_W
        o_ref[:, lo:mid] = (acc[:, lo:mid] * scale).astype(o_ref.dtype)
        o_ref[:, mid:hi] = (_rope_chunk(acc[:, mid:hi], cos, sa, sb) * scale).astype(o_ref.dtype)


def mla_query_projection(cqn, w_slots, tabs, l, tm, tn):
    m, k = cqn.shape
    n = w_slots.shape[2]
    t_tab = tabs[0].shape[0]
    tm, tn = min(tm, t_tab), min(tn, n)
    nt = t_tab // tm
    tab = pl.BlockSpec((tm, LANES), lambda j, i: (i % nt, 0))
    return pl.pallas_call(
        functools.partial(_q_proj_kernel, scale=(MLA_NOPE + MLA_ROPE) ** -0.5),
        grid=(n // tn, m // tm),
        in_specs=[pl.BlockSpec((tm, k), lambda j, i: (i, 0)), pl.BlockSpec((None, k, tn), lambda j, i: (l, 0, j)),
                  tab, tab, tab],
        out_specs=pl.BlockSpec((tm, tn), lambda j, i: (i, j)),
        out_shape=jax.ShapeDtypeStruct((m, n), BF16),
        compiler_params=_cparams("parallel", "parallel"), name="mla_query_projection",
    )(cqn, w_slots, *tabs)


def _rope_chunk(x, cos, sa, sb):
    return x * cos + pltpu.roll(x, LANES - 32, 1) * sa + pltpu.roll(x, 32, 1) * sb


def _rope_cols(x, cos, sa, sb):
    return jnp.concatenate([_rope_chunk(x[:, c * LANES:(c + 1) * LANES], cos, sa, sb)
                            for c in range(x.shape[1] // LANES)], axis=1)


_IN_GROUPS = 5
_IN_INPUTS = 6


def _in_proj_kernel(*refs, q_scale):
    x_ref, w_ref, cos_ref, sa_ref, sb_ref, gq_ref = refs[:_IN_INPUTS]
    u_ref, qd_ref, kd32_ref, kd16_ref, vd32_ref, vd16_ref, cq_ref, acc_a, acc_b = refs[-9:]
    accs = (acc_a, acc_b)
    j = pl.program_id(1)
    rope = lambda a: _rope_cols(a, cos_ref[...], sa_ref[...], sb_ref[...])

    def finish(g, acc):
        if g == 0:
            u_ref[...] = acc
        elif g == 1:
            qd_ref[...] = (rope(acc) * q_scale).astype(BF16)
        elif g == 2:
            k = rope(acc)
            kd32_ref[...] = k
            kd16_ref[...] = k.astype(BF16)
        elif g == 3:
            vd32_ref[...] = acc
            vd16_ref[...] = acc.astype(BF16)
        else:
            cq_ref[...] = _rms(acc, gq_ref[...]).astype(BF16)

    for g in range(_IN_GROUPS + 1):
        @pl.when(j == g)
        def _(g=g):
            if g < _IN_GROUPS:
                accs[g % 2][...] = jnp.dot(x_ref[...], w_ref[...], preferred_element_type=F32)
            if g > 0:
                finish(g - 1, accs[(g - 1) % 2][...])


_LAT_INPUTS = 6


def _in_proj_latent_kernel(*refs):
    x_ref, w_ref, cos_ref, sa_ref, sb_ref, gkv_ref = refs[:_LAT_INPUTS]
    ckv32_ref, ckv16_ref, kpe32_ref, kpe16_ref, kc16_ref = refs[-5:]
    rank = ckv32_ref.shape[1]
    acc = jnp.dot(x_ref[...], w_ref[...], preferred_element_type=F32)
    ckv = _rms(acc[:, 0:rank], gkv_ref[...])
    ckv32_ref[...] = ckv
    ckv16_ref[...] = ckv.astype(BF16)
    lane = lax.broadcasted_iota(jnp.int32, (acc.shape[0], LANES), 1)
    kpe_raw = jnp.where(lane < MLA_ROPE, acc[:, rank:rank + LANES], 0.0)
    kpe_chunk = _rope_chunk(kpe_raw, cos_ref[...], sa_ref[...], sb_ref[...])
    kpe = kpe_chunk[:, :MLA_ROPE]
    kpe32_ref[...] = kpe
    kpe16_ref[...] = kpe.astype(BF16)
    kc16_ref[:, 0:rank] = ckv.astype(BF16)
    kc16_ref[:, rank:rank + LANES] = kpe_chunk.astype(BF16)


def input_projection(h, w_in, tabs, g_q, g_kv, dims, tm, tm_lat, l, depth, state):
    m, d = h.shape
    pool_w, diff_w, q_rank, kv_rank = dims
    gw = diff_w
    lw = kv_rank + LANES
    assert pool_w == gw and q_rank == gw and w_in.shape[2] == _IN_GROUPS * gw + kv_rank + MLA_ROPE
    assert (_IN_GROUPS * gw) % lw == 0, "the latent columns must start on a multiple of their block width"
    t_tab = tabs[0].shape[0]
    tm, tm_lat = min(tm, t_tab), min(tm_lat, t_tab)
    state = (None, None) if state is None else ((state[0], state[1]), (state[2], state[3]))
    sds = lambda w, dt: jax.ShapeDtypeStruct((m, w), dt)
    stk = lambda w: jax.ShapeDtypeStruct((depth, m, w), F32)
    alias = pl.BlockSpec(memory_space=pl.ANY)

    nt = t_tab // tm
    row = lambda w: pl.BlockSpec((tm, w), lambda i, j: (i, 0))
    lay = lambda w: pl.BlockSpec((None, tm, w), lambda i, j: (l, i, 0))
    tab = pl.BlockSpec((tm, LANES), lambda i, j: (i % nt, 0))
    prev = () if state[0] is None else state[0]
    u, qd, kd32, kd16, vd32, vd16, cqn = pl.pallas_call(
        functools.partial(_in_proj_kernel, q_scale=DIFF_D ** -0.5),
        grid=(m // tm, _IN_GROUPS + 1),
        in_specs=[pl.BlockSpec((tm, d), lambda i, j: (jnp.minimum(i + j // _IN_GROUPS, m // tm - 1), 0)),
                  pl.BlockSpec((None, d, gw), lambda i, j: (l, 0, j % _IN_GROUPS)),
                  tab, tab, tab, pl.BlockSpec((1, gw), lambda i, j: (0, 0))] + [alias] * len(prev),
        out_specs=[row(gw), row(gw), lay(gw), row(gw), lay(gw), row(gw), row(gw)],
        out_shape=[sds(gw, F32), sds(gw, BF16), stk(gw), sds(gw, BF16), stk(gw), sds(gw, BF16), sds(gw, BF16)],
        scratch_shapes=[pltpu.VMEM((tm, gw), F32), pltpu.VMEM((tm, gw), F32)],
        input_output_aliases={_IN_INPUTS + a: o for a, o in enumerate((2, 4)[:len(prev)])},
        compiler_params=_cparams("parallel", "arbitrary"), name="in_proj",
    )(h, w_in, *tabs, g_q.reshape(1, gw), *prev)

    nt = t_tab // tm_lat
    row = lambda w: pl.BlockSpec((tm_lat, w), lambda i: (i, 0))
    lay = lambda w: pl.BlockSpec((None, tm_lat, w), lambda i: (l, i, 0))
    tab = pl.BlockSpec((tm_lat, LANES), lambda i: (i % nt, 0))
    prev = () if state[1] is None else state[1]
    ckv32, ckv16, kpe32, kpe16, kc16 = pl.pallas_call(
        _in_proj_latent_kernel,
        grid=(m // tm_lat,),
        in_specs=[pl.BlockSpec((tm_lat, d), lambda i: (i, 0)),
                  pl.BlockSpec((None, d, lw), lambda i: (l, 0, _IN_GROUPS * gw // lw)),
                  tab, tab, tab, pl.BlockSpec((1, kv_rank), lambda i: (0, 0))] + [alias] * len(prev),
        out_specs=[lay(kv_rank), row(kv_rank), lay(MLA_ROPE), row(MLA_ROPE), row(lw)],
        out_shape=[stk(kv_rank), sds(kv_rank, BF16), stk(MLA_ROPE), sds(MLA_ROPE, BF16), sds(lw, BF16)],
        input_output_aliases={_LAT_INPUTS + a: o for a, o in enumerate((0, 2)[:len(prev)])},
        compiler_params=_cparams("parallel"), name="in_proj_latent",
    )(h, w_in, *tabs, g_kv.reshape(1, kv_rank), *prev)
    return (u, qd, kd16, vd16, cqn, ckv16, kpe16, kc16), (kd32, vd32, ckv32, kpe32)


def _pool_kernel(z_ref, halo_ref, prev_ref, wp_ref, sc_ref, o_ref, ext_ref, *, tt, pos0):
    i = pl.program_id(1)
    ext_ref[0:POOL_HALO, :] = jnp.where(i == 0, prev_ref[0], halo_ref[...])
    ext_ref[POOL_HALO:POOL_HALO + tt, :] = z_ref[...]
    pos = pos0 + i * tt + lax.broadcasted_iota(jnp.int32, (tt, 1), 0)
    group = wp_ref.shape[1]
    for g, w in enumerate(POOL_WINDOWS):
        cs = slice(g * group, (g + 1) * group)
        s = ext_ref[POOL_HALO:POOL_HALO + tt, cs]
        u = s
        for j in range(1, w):
            s = s + ext_ref[POOL_HALO - j:POOL_HALO - j + tt, cs]
        cnt = jnp.minimum(w, pos + 1).astype(F32)
        d = s / cnt - u
        y = jnp.dot(d.astype(BF16), wp_ref[g], preferred_element_type=F32) * sc_ref[:, cs]
        o_ref[:, cs] = y.astype(o_ref.dtype)


def pool_mix(z, prev, w_pool, l, scale, batch, t_len, pos0, tt):
    m = z.shape[0]
    c = prev.shape[2]
    nt = t_len // tt
    hb = tt // POOL_HALO
    return pl.pallas_call(
        functools.partial(_pool_kernel, tt=tt, pos0=pos0),
        grid=(batch, nt),
        in_specs=[pl.BlockSpec((tt, c), lambda b, i: (b * nt + i, 0)),
                  pl.BlockSpec((POOL_HALO, c), lambda b, i: (jnp.maximum((b * nt + i) * hb - 1, 0), 0)),
                  pl.BlockSpec((1, POOL_HALO, c), lambda b, i: (b, 0, 0)),
                  pl.BlockSpec((None,) + w_pool.shape[1:], lambda b, i: (l, 0, 0, 0)),
                  pl.BlockSpec((1, c), lambda b, i: (0, 0))],
        out_specs=pl.BlockSpec((tt, c), lambda b, i: (b * nt + i, 0)),
        out_shape=jax.ShapeDtypeStruct((m, c), BF16),
        scratch_shapes=[pltpu.VMEM((POOL_HALO + tt, c), F32)],
        compiler_params=_cparams("parallel", "parallel"), name="pool_mix",
    )(z, z, prev, w_pool, scale.reshape(1, c))


def _last_kv_block(i, tq, tkv, q0, s_len):
    last_q = q0 + (i + 1) * tq - 1
    visible = jnp.minimum((last_q // CHUNK + 1) * CHUNK, s_len)
    return (visible - 1) // tkv


def _mask_bias(q_start, tq, k_start, tkv, s_len):
    qpos = q_start + lax.broadcasted_iota(jnp.int32, (tq, 1), 0)
    kpos = k_start + lax.broadcasted_iota(jnp.int32, (1, tkv), 1)
    shift = CHUNK.bit_length() - 1
    ok = (jnp.right_shift(kpos, shift) <= jnp.right_shift(qpos, shift)) & (kpos < s_len)
    return jnp.where(ok, 0.0, -jnp.inf).astype(F32)


def _add_bias(s, bias, groups):
    rows, tkv = s.shape
    return (s.reshape(groups, rows // groups, tkv) + bias[None]).reshape(rows, tkv)


def _lane_tile(x, width):
    return x if width == LANES else jnp.concatenate([x] * (width // LANES), axis=1)


def _flash_update(s, v, m_ref, l_ref, acc_ref):
    tkv = s.shape[1]
    m_prev = m_ref[...]
    m_new = jnp.maximum(m_prev, jnp.max(s, axis=-1, keepdims=True))
    alpha = jnp.exp(m_prev - m_new)
    p = jnp.exp(s - _lane_tile(m_new, tkv))
    psum = p[:, :LANES]
    for c in range(1, tkv // LANES):
        psum = psum + p[:, c * LANES:(c + 1) * LANES]
    l_ref[...] = alpha * l_ref[...] + psum
    acc_ref[...] = (_lane_tile(alpha, acc_ref.shape[1]) * acc_ref[...]
                    + jnp.dot(p.astype(BF16), v, preferred_element_type=F32))
    m_ref[...] = m_new


def _flash_init(m_ref, l_ref, acc_ref):
    m_ref[...] = jnp.full(m_ref.shape, -jnp.inf, F32)
    l_ref[...] = jnp.zeros(l_ref.shape, F32)
    acc_ref[...] = jnp.zeros(acc_ref.shape, F32)


def _flash_result(l_ref, acc_ref):
    return acc_ref[...] / jnp.sum(l_ref[...], axis=-1, keepdims=True)


def _flash_scratch(groups, rows, dv):
    lead = () if groups is None else (groups,)
    return [pltpu.VMEM(lead + (rows, LANES), F32), pltpu.VMEM(lead + (rows, LANES), F32),
            pltpu.VMEM(lead + (rows, dv), F32)]


def _full_kv_blocks(i, tq, tkv, q0, s_len):
    first_q = q0 + i * tq
    return jnp.minimum((first_q // CHUNK + 1) * CHUNK, s_len) // tkv


def _causal_windows(nq, nkv, tq, tkv, t_len):
    last = functools.partial(_last_kv_block, tq=tq, tkv=tkv, q0=0, s_len=t_len)
    q_row = lambda b, i, j: b * nq + jnp.where(j <= last(i), i, jnp.minimum(i + 1, nq - 1))
    kv_row = lambda b, i, j: b * nkv + jnp.where(j <= last(i), j, 0)
    return q_row, kv_row


def _with_ones(v):
    return jnp.concatenate([v, jnp.ones((v.shape[0], LANES), v.dtype)], axis=1)


def _flash_update_wide(s, v_ones, m_ref, acc_ref):
    tkv = s.shape[1]
    m_prev = m_ref[...]
    m_new = jnp.maximum(m_prev, jnp.max(s, axis=-1, keepdims=True))
    alpha = jnp.exp(m_prev - m_new)
    p = jnp.exp((s - _lane_tile(m_new, tkv)).astype(BF16))
    acc_ref[...] = (_lane_tile(alpha, acc_ref.shape[1]) * acc_ref[...]
                    + jnp.dot(p, v_ones, preferred_element_type=F32))
    m_ref[...] = m_new


def _flash_init_wide(m_ref, acc_ref):
    m_ref[...] = jnp.full(m_ref.shape, -jnp.inf, F32)
    acc_ref[...] = jnp.zeros(acc_ref.shape, F32)


def _flash_result_wide(acc_ref):
    dv = acc_ref.shape[-1] - LANES
    assert dv == LANES
    return acc_ref[:, 0:dv] / acc_ref[:, dv:dv + LANES]


def _flash_scratch_wide(groups, rows, dv):
    return [pltpu.VMEM((groups, rows, LANES), F32), pltpu.VMEM((groups, rows, dv + LANES), F32)]


def _past_tile(p_len, cap):
    tile = max(t for t in range(LANES, cap + 1, LANES) if p_len % t == 0)
    return tile


def _diff_split_q(q_ref, qs_ref, heads, tq):
    hd = 2 * DIFF_D
    lane = lax.broadcasted_iota(jnp.int32, (tq, hd), 1)
    for h in range(heads):
        q = q_ref[:, h * hd:(h + 1) * hd]
        qs_ref[h, 0:tq, :] = jnp.where(lane < DIFF_D, q, jnp.zeros_like(q))
        qs_ref[h, tq:2 * tq, :] = jnp.where(lane >= DIFF_D, q, jnp.zeros_like(q))


def _heads_sweep(q_of, k_of, v_of, bias, groups, m_ref, acc_ref, heads):
    for h in range(heads):
        s = lax.dot_general(q_of(h), k_of(h), _NT, preferred_element_type=F32)
        if bias is not None:
            s = s + bias if groups == 1 else _add_bias(s, bias, groups)
        _flash_update_wide(s, _with_ones(v_of(h)), m_ref.at[h], acc_ref.at[h])


def _diff_finish(lam_ref, g_ref, o_ref, acc_ref, heads, tq, out_scale):
    hd = 2 * DIFF_D
    lam = lam_ref[0, 0]
    for h in range(heads):
        a = _flash_result_wide(acc_ref.at[h])
        o = a[0:tq] - lam * a[tq:2 * tq]
        o_ref[:, h * hd:(h + 1) * hd] = (_rms(o, g_ref[...]) * out_scale).astype(o_ref.dtype)


def _diff_kernel(lam_ref, q_ref, k_ref, v_ref, g_ref, o_ref, qs_ref, m_ref, acc_ref, *,
                 heads, tq, tkv, s_len, out_scale):
    i, j = pl.program_id(1), pl.program_id(2)
    hd = 2 * DIFF_D
    k_of = lambda h: k_ref[:, h * hd:(h + 1) * hd]
    v_of = lambda h: v_ref[:, h * hd:(h + 1) * hd]
    full = _full_kv_blocks(i, tq, tkv, 0, s_len)

    @pl.when(j == 0)
    def _():
        _flash_init_wide(m_ref, acc_ref)
        _diff_split_q(q_ref, qs_ref, heads, tq)

    @pl.when(j < full)
    def _():
        _heads_sweep(lambda h: qs_ref[h], k_of, v_of, None, 2, m_ref, acc_ref, heads)

    @pl.when((j >= full) & (j <= _last_kv_block(i, tq, tkv, 0, s_len)))
    def _():
        _heads_sweep(lambda h: qs_ref[h], k_of, v_of, _mask_bias(i * tq, tq, j * tkv, tkv, s_len), 2,
                     m_ref, acc_ref, heads)

    @pl.when(j == pl.num_programs(2) - 1)
    def _():
        _diff_finish(lam_ref, g_ref, o_ref, acc_ref, heads, tq, out_scale)


def diff_attention(q, k, v, lam, g_sub, batch, heads, t_len, out_scale, tq, tkv):
    hd = 2 * DIFF_D
    w = heads * hd
    nq, nkv = t_len // tq, t_len // tkv
    q_row, kv_row = _causal_windows(nq, nkv, tq, tkv, t_len)
    kvmap = lambda b, i, j: (kv_row(b, i, j), 0)
    return pl.pallas_call(
        functools.partial(_diff_kernel, heads=heads, tq=tq, tkv=tkv, s_len=t_len, out_scale=out_scale),
        grid=(batch, nq, nkv),
        in_specs=[pl.BlockSpec(memory_space=pltpu.SMEM),
                  pl.BlockSpec((tq, w), lambda b, i, j: (q_row(b, i, j), 0)),
                  pl.BlockSpec((tkv, w), kvmap),
                  pl.BlockSpec((tkv, w), kvmap),
                  pl.BlockSpec((1, hd), lambda b, i, j: (0, 0))],
        out_specs=pl.BlockSpec((tq, w), lambda b, i, j: (b * nq + i, 0)),
        out_shape=jax.ShapeDtypeStruct((batch * t_len, w), BF16),
        scratch_shapes=[pltpu.VMEM((heads, 2 * tq, hd), BF16)] + _flash_scratch_wide(heads, 2 * tq, hd),
        compiler_params=_cparams("parallel", "parallel", "arbitrary"), name="diff_attention",
    )(lam.reshape(1, 1), q, k, v, g_sub.reshape(1, hd))


def _diff_past_kernel(lam_ref, q_ref, kc_ref, vc_ref, kn_ref, vn_ref, g_ref, o_ref,
                      qs_ref, m_ref, acc_ref, *, heads, tq, tkv, p_len, s_len, out_scale):
    j = pl.program_id(1)
    n_past = pl.num_programs(1) - 1
    hd = 2 * DIFF_D

    @pl.when(j == 0)
    def _():
        _flash_init_wide(m_ref, acc_ref)
        _diff_split_q(q_ref, qs_ref, heads, tq)

    @pl.when(j < n_past)
    def _():
        _heads_sweep(lambda h: qs_ref[h], lambda h: kc_ref[h], lambda h: vc_ref[h],
                     _mask_bias(p_len, tq, j * tkv, tkv, s_len), 2, m_ref, acc_ref, heads)

    @pl.when(j == n_past)
    def _():
        _heads_sweep(lambda h: qs_ref[h], lambda h: kn_ref[:, h * hd:(h + 1) * hd],
                     lambda h: vn_ref[:, h * hd:(h + 1) * hd],
                     _mask_bias(p_len, tq, p_len, kn_ref.shape[0], s_len), 2, m_ref, acc_ref, heads)
        _diff_finish(lam_ref, g_ref, o_ref, acc_ref, heads, tq, out_scale)


def diff_attention_past(q, kc, vc, kn, vn, l, lam, g_sub, batch, heads, t_len, out_scale):
    hd = 2 * DIFF_D
    w = heads * hd
    p_len = kc.shape[3]
    tn = kn.shape[0] // batch
    tkv = _past_tile(p_len, 1024)
    n_past = p_len // tkv
    cache = pl.BlockSpec((None, None, heads, tkv, hd), lambda b, j: (l, b, 0, jnp.minimum(j, n_past - 1), 0))
    new = pl.BlockSpec((tn, w), lambda b, j: (b, 0))
    qmap = lambda b, j: (b, 0)
    return pl.pallas_call(
        functools.partial(_diff_past_kernel, heads=heads, tq=t_len, tkv=tkv, p_len=p_len, s_len=p_len + t_len,
                          out_scale=out_scale),
        grid=(batch, n_past + 1),
        in_specs=[pl.BlockSpec(memory_space=pltpu.SMEM), pl.BlockSpec((t_len, w), qmap), cache, cache, new, new,
                  pl.BlockSpec((1, hd), lambda b, j: (0, 0))],
        out_specs=pl.BlockSpec((t_len, w), qmap),
        out_shape=jax.ShapeDtypeStruct((batch * t_len, w), BF16),
        scratch_shapes=[pltpu.VMEM((heads, 2 * t_len, hd), BF16)] + _flash_scratch_wide(heads, 2 * t_len, hd),
        compiler_params=_cparams("parallel", "arbitrary"), name="diff_attention_past",
    )(lam.reshape(1, 1), q, kc, vc, kn, vn, g_sub.reshape(1, hd))


MLA_KEY_W = 2 * MLA_NOPE


def _mla_kernel(q_ref, k_ref, v_ref, o_ref, m_ref, acc_ref, *, heads, tq, tkv, s_len):
    i, j = pl.program_id(1), pl.program_id(2)
    q_of = lambda h: q_ref[:, h * MLA_KEY_W:(h + 1) * MLA_KEY_W]
    k_of = lambda h: k_ref[:, h * MLA_KEY_W:(h + 1) * MLA_KEY_W]
    v_of = lambda h: v_ref[:, h * MLA_NOPE:(h + 1) * MLA_NOPE]
    full = _full_kv_blocks(i, tq, tkv, 0, s_len)

    @pl.when(j == 0)
    def _():
        _flash_init_wide(m_ref, acc_ref)

    @pl.when(j < full)
    def _():
        _heads_sweep(q_of, k_of, v_of, None, 1, m_ref, acc_ref, heads)

    @pl.when((j >= full) & (j <= _last_kv_block(i, tq, tkv, 0, s_len)))
    def _():
        _heads_sweep(q_of, k_of, v_of, _mask_bias(i * tq, tq, j * tkv, tkv, s_len), 1, m_ref, acc_ref, heads)

    @pl.when(j == pl.num_programs(2) - 1)
    def _():
        for h in range(heads):
            o_ref[:, h * MLA_NOPE:(h + 1) * MLA_NOPE] = _flash_result_wide(acc_ref.at[h]).astype(o_ref.dtype)


def mla_attention(q, kv_up, batch, heads, t_len, tq, tkv):
    kw, vw = heads * MLA_KEY_W, heads * MLA_NOPE
    nq, nkv = t_len // tq, t_len // tkv
    q_row, kv_row = _causal_windows(nq, nkv, tq, tkv, t_len)
    return pl.pallas_call(
        functools.partial(_mla_kernel, heads=heads, tq=tq, tkv=tkv, s_len=t_len),
        grid=(batch, nq, nkv),
        in_specs=[pl.BlockSpec((tq, kw), lambda b, i, j: (q_row(b, i, j), 0)),
                  pl.BlockSpec((tkv, kw), lambda b, i, j: (kv_row(b, i, j), 0)),
                  pl.BlockSpec((tkv, vw), lambda b, i, j: (kv_row(b, i, j), kw // vw))],
        out_specs=pl.BlockSpec((tq, vw), lambda b, i, j: (b * nq + i, 0)),
        out_shape=jax.ShapeDtypeStruct((batch * t_len, vw), BF16),
        scratch_shapes=_flash_scratch_wide(heads, tq, MLA_NOPE),
        compiler_params=_cparams("parallel", "parallel", "arbitrary"), name="mla_attention",
    )(q, kv_up, kv_up)


def _mla_past_kernel(q_ref, wuk_ref, ckvc_ref, kpec_ref, ckvn_ref, kpen_ref, wuv_ref,
                     o_ref, qlat_ref, qpe_ref, m_ref, l_ref, acc_ref, *, heads, tq, tkv, p_len, s_len):
    j = pl.program_id(1)
    n_past = pl.num_programs(1) - 1

    @pl.when(j == 0)
    def _():
        _flash_init(m_ref, l_ref, acc_ref)
        for h in range(heads):
            lo = h * MLA_KEY_W
            qlat = jnp.dot(q_ref[:, lo:lo + MLA_NOPE], wuk_ref[h], preferred_element_type=F32)
            qlat_ref[h * tq:(h + 1) * tq, :] = qlat.astype(BF16)
            qpe_ref[h * tq:(h + 1) * tq, :] = q_ref[:, lo + MLA_NOPE:lo + MLA_NOPE + MLA_ROPE]

    def sweep(ckv, kpe, k_start):
        s = (lax.dot_general(qlat_ref[...], ckv, _NT, preferred_element_type=F32)
             + lax.dot_general(qpe_ref[...], kpe, _NT, preferred_element_type=F32))
        s = _add_bias(s, _mask_bias(p_len, tq, k_start, ckv.shape[0], s_len), heads)
        _flash_update(s, ckv, m_ref, l_ref, acc_ref)

    @pl.when(j < n_past)
    def _():
        sweep(ckvc_ref[...].astype(BF16), kpec_ref[...].astype(BF16), j * tkv)

    @pl.when(j == n_past)
    def _():
        sweep(ckvn_ref[...], kpen_ref[...], p_len)
        v_w = wuv_ref.shape[2]
        o_lat = _flash_result(l_ref, acc_ref).astype(BF16)
        for h in range(heads):
            o_h = jnp.dot(o_lat[h * tq:(h + 1) * tq], wuv_ref[h], preferred_element_type=F32)
            o_ref[:, h * v_w:(h + 1) * v_w] = o_h.astype(o_ref.dtype)


def mla_attention_past(q, wuk_t, wuv, l, ckv_c, kpe_c, ckv_n, kpe_n, batch, t_len):
    _, heads, _, rank = wuk_t.shape
    v_w = wuv.shape[3]
    p_len = ckv_c.shape[2]
    tn = ckv_n.shape[0] // batch
    tkv = _past_tile(p_len, 1024)
    n_past = p_len // tkv
    rows = heads * t_len
    layer4 = lambda a: pl.BlockSpec((None,) + a.shape[1:], lambda b, j: (l, 0, 0, 0))
    cache = lambda w: pl.BlockSpec((None, None, tkv, w), lambda b, j: (l, b, jnp.minimum(j, n_past - 1), 0))
    new = lambda w: pl.BlockSpec((tn, w), lambda b, j: (b, 0))
    return pl.pallas_call(
        functools.partial(_mla_past_kernel, heads=heads, tq=t_len, tkv=tkv, p_len=p_len, s_len=p_len + t_len),
        grid=(batch, n_past + 1),
        in_specs=[pl.BlockSpec((t_len, q.shape[1]), lambda b, j: (b, 0)), layer4(wuk_t),
                  cache(rank), cache(MLA_ROPE), new(rank), new(MLA_ROPE), layer4(wuv)],
        out_specs=pl.BlockSpec((t_len, heads * v_w), lambda b, j: (b, 0)),
        out_shape=jax.ShapeDtypeStruct((batch * t_len, heads * v_w), BF16),
        scratch_shapes=[pltpu.VMEM((rows, rank), BF16), pltpu.VMEM((rows, MLA_ROPE), BF16)]
                       + _flash_scratch(None, rows, rank),
        compiler_params=_cparams("parallel", "arbitrary"), name="mla_attention_past",
    )(q, wuk_t, ckv_c, kpe_c, ckv_n, kpe_n, wuv)


def _cross_block_kernel(y_ref, x_ref, g_mix_ref, g_pre_ref, wq_ref, k_ref, v_ref, wo_ref, g_post_ref, g_next_ref,
                        xo_ref, ho_ref, *, heads, hd):
    x1 = x_ref[...] + _rms(y_ref[...], g_mix_ref[...])
    q = jnp.dot(_rms(x1, g_pre_ref[...]).astype(BF16), wq_ref[...], preferred_element_type=F32).astype(BF16)
    outs = []
    for h in range(heads):
        sl = slice(h * hd, (h + 1) * hd)
        s = lax.dot_general(q[:, sl], k_ref[:, sl], _NT, preferred_element_type=F32) * (hd ** -0.5)
        p = jnp.exp(s - jnp.max(s, axis=-1, keepdims=True))
        o = jnp.dot(p.astype(BF16), v_ref[:, sl], preferred_element_type=F32)
        outs.append((o / jnp.sum(p, axis=-1, keepdims=True)).astype(BF16))
    y2 = jnp.dot(jnp.concatenate(outs, axis=1), wo_ref[...], preferred_element_type=F32)
    x2 = x1 + _rms(y2, g_post_ref[...])
    xo_ref[...] = x2
    ho_ref[...] = _rms(x2, g_next_ref[...]).astype(ho_ref.dtype)


def cross_block(y_mix, x, g_mix_post, g_pre, wq, mk, mv, wo, l, g_post, g_next, batch, t_len, mem_len, heads, tm):
    m, d = x.shape
    w = wq.shape[2]
    nq = t_len // tm
    row = pl.BlockSpec((tm, d), lambda b, i: (b * nq + i, 0))
    vec = pl.BlockSpec((1, d), lambda b, i: (0, 0))
    kv = pl.BlockSpec((mem_len, w), lambda b, i: (b, 0))
    once = pl.Buffered(1)
    return pl.pallas_call(
        functools.partial(_cross_block_kernel, heads=heads, hd=w // heads),
        grid=(batch, nq),
        in_specs=[row, row, vec, vec,
                  pl.BlockSpec((None, d, w), lambda b, i: (l, 0, 0), pipeline_mode=once), kv, kv,
                  pl.BlockSpec((None, w, d), lambda b, i: (l, 0, 0), pipeline_mode=once), vec, vec],
        out_specs=[row, row],
        out_shape=[jax.ShapeDtypeStruct((m, d), F32), jax.ShapeDtypeStruct((m, d), BF16)],
        compiler_params=_cparams("parallel", "parallel"), name="cross_block",
    )(y_mix, x, g_mix_post.reshape(1, d), g_pre.reshape(1, d), wq, mk, mv, wo,
      g_post.reshape(1, d), g_next.reshape(1, d))


def _rope_tables(pos):
    half = DIFF_D // 2
    inv = ROPE_THETA ** (-jnp.arange(half, dtype=F32) / half)
    ang = pos.astype(F32)[:, None] * inv[None, :]
    cos, sin, zero = jnp.cos(ang), jnp.sin(ang), jnp.zeros_like(ang)
    reps = LANES // DIFF_D
    return (jnp.tile(cos, (1, 2 * reps)), jnp.tile(jnp.concatenate([-sin, zero], 1), (1, reps)),
            jnp.tile(jnp.concatenate([zero, sin], 1), (1, reps)))


def _round_up(a, b):
    return -(-a // b) * b


def _kv_up_weight(w_uk, w_uv):
    depth, rank, heads, nope = w_uk.shape
    k_rows = jnp.concatenate([w_uk, jnp.zeros_like(w_uk)], axis=3).reshape(depth, rank, heads * MLA_KEY_W)
    slot = jnp.concatenate([jnp.zeros((MLA_ROPE, nope), F32), jnp.eye(MLA_ROPE, dtype=F32),
                            jnp.zeros((MLA_ROPE, MLA_KEY_W - nope - MLA_ROPE), F32)], axis=1)
    rope_rows = jnp.broadcast_to(jnp.tile(slot, (1, heads)), (depth, MLA_ROPE, heads * MLA_KEY_W))
    pad = LANES - MLA_ROPE
    w_k = jnp.concatenate([k_rows, rope_rows, jnp.zeros((depth, pad, heads * MLA_KEY_W), F32)], axis=1)
    w_v = jnp.concatenate([w_uv.reshape(depth, rank, -1), jnp.zeros((depth, LANES, heads * w_uv.shape[3]), F32)],
                          axis=1)
    return jnp.concatenate([w_k, w_v], axis=2).astype(BF16)


def _pad_rows(a, batch, rows):
    t_len = a.shape[0] // batch
    a = jnp.pad(a.reshape(batch, t_len, a.shape[1]), ((0, 0), (0, rows - t_len), (0, 0)))
    return a.reshape(batch * rows, a.shape[2])


def _mixer_and_memory(wts, l, x, h, batch, t_len, pos0, tabs, mem_k, mem_v, past, tiles, state):
    tm, tt, tq_diff, tkv_diff, tq_mla, tkv_mla, tq_x = tiles
    lam_init = 0.8 - 0.6 * math.exp(-0.3 * l)
    pool_w, diff_w, q_rank, kv_rank = wts["dims"]
    heads_d = diff_w // (2 * DIFF_D)
    heads_m = wts["w_uk_t"].shape[1]

    (u, qd, kd16, vd16, cqn, ckv16, kpe16, kc16), state = input_projection(
        h, wts["w_in"], tabs, wts["g_mla_q"][l], wts["g_mla_kv"][l], wts["dims"],
        min(tm, 512), tm, l, wts["depth"], state)

    keep = POOL_HALO - 1
    if past is None:
        prev = jnp.zeros((batch, POOL_HALO, pool_w), F32)
    else:
        prev = jnp.concatenate([jnp.zeros((batch, 1, pool_w), F32), past["pool"]], axis=1)
    u3 = u.reshape(batch, t_len, pool_w)
    if t_len >= keep:
        new_pool = u3[:, t_len - keep:]
    else:
        new_pool = jnp.concatenate([prev[:, 1 + t_len:], u3], axis=1)
    y_pool = pool_mix(u, prev, wts["w_pool"], l, wts["pool_scale"][l], batch, t_len, pos0, tt)

    lq = wts["diff_lambda"][l]
    lam = jnp.exp(jnp.sum(lq[0] * lq[1])) - jnp.exp(jnp.sum(lq[2] * lq[3])) + lam_init
    q = mla_query_projection(cqn, wts["w_uq"], tabs, l, tm, 1024)
    if past is None:
        o_diff = diff_attention(qd, kd16, vd16, lam, wts["g_diff_sub"][l], batch, heads_d, t_len,
                                1.0 - lam_init, tq_diff, tkv_diff)
        kv_up = matmul(kc16, wts["w_kv_up"], l, BF16, tm, heads_m * MLA_NOPE)
        o_mla = mla_attention(q, kv_up, batch, heads_m, t_len, tq_mla, tkv_mla)
    else:
        tn = _round_up(t_len, LANES)
        o_diff = diff_attention_past(qd, past["diff_k"], past["diff_v"], _pad_rows(kd16, batch, tn),
                                     _pad_rows(vd16, batch, tn), l, lam, wts["g_diff_sub"][l], batch, heads_d,
                                     t_len, 1.0 - lam_init)
        o_mla = mla_attention_past(q, wts["w_uk_t"], wts["w_uv"], l, past["ckv"], past["kpe"],
                                   _pad_rows(ckv16, batch, tn), _pad_rows(kpe16, batch, tn), batch, t_len)

    y = matmul_cat([y_pool, o_diff, o_mla], wts["w_out"], l, F32, tm, 1024)
    x, h = cross_block(y, x, wts["g_mix_post"][l], wts["g_x_pre"][l], wts["w_mem_q"], mem_k, mem_v,
                       wts["w_mem_o"], l, wts["g_x_post"][l], wts["g_ff_pre"][l], batch, t_len,
                       mem_k.shape[0] // batch, wts["mem_heads"], tq_x)
    return x, h, state, new_pool


def _ffn_pair(wts, l, x, h, xs, hs, g_next, tiles, tiles_s):
    a, a_s = matmul_swiglu_pair(h, hs, wts["w_gate"], wts["w_up"], l, 2 * tiles[0], 256)
    y, y_s = matmul_pair(a, a_s, wts["w_down"], l, F32, 512, 512)
    x, h = post_res(y, x, wts["g_ff_post"][l], g_next, tiles[1])
    xs, hs = post_res(y_s, xs, wts["g_ff_post"][l], g_next, tiles_s[1])
    return x, h, xs, hs


def kernel(x_prompt, x_sample, cache_diff_k, cache_diff_v, cache_mla_ckv, cache_mla_kpe, cache_pool, cache_mem_k, cache_mem_v, mem_prompt, g_mix_pre, w_in, w_pool, pool_scale, diff_lambda, g_diff_sub, g_mla_q, w_mla_uq, w_mla_uk, w_mla_uv, g_mla_kv, w_out, g_mix_post, g_mem, w_mem_k, w_mem_v, w_mem_q, w_mem_o, g_x_pre, g_x_post, g_ff_pre, w_gate, w_up, w_down, g_ff_post):
    depth = w_in.shape[0]
    bp, tp, d = x_prompt.shape
    bs, ts, _ = x_sample.shape
    past_len = cache_mla_ckv.shape[2]
    pool_w = cache_pool.shape[3]
    heads_d, diff_w = cache_diff_k.shape[3], cache_diff_k.shape[3] * cache_diff_k.shape[4]
    q_rank, kv_rank = g_mla_q.shape[1], g_mla_kv.shape[1]
    mla_heads = w_mla_uk.shape[2]
    mem_len, mem_heads, mem_hd = cache_mem_k.shape[2:]
    mem_w = mem_heads * mem_hd

    uq = w_mla_uq.reshape(depth, q_rank, mla_heads, MLA_NOPE + MLA_ROPE)
    wts = {
        "depth": depth, "dims": (pool_w, diff_w, q_rank, kv_rank), "mem_heads": mem_heads,
        "w_in": w_in.astype(BF16),
        "w_pool": w_pool.astype(BF16), "pool_scale": pool_scale, "diff_lambda": diff_lambda,
        "g_diff_sub": g_diff_sub, "g_mla_q": g_mla_q, "g_mla_kv": g_mla_kv,
        "w_uq": jnp.pad(uq, ((0, 0), (0, 0), (0, 0), (0, MLA_KEY_W - MLA_NOPE - MLA_ROPE))).reshape(
            depth, q_rank, mla_heads * MLA_KEY_W).astype(BF16),
        "w_uk_t": jnp.transpose(w_mla_uk, (0, 2, 3, 1)).astype(BF16),
        "w_uv": jnp.transpose(w_mla_uv, (0, 2, 1, 3)).astype(BF16),
        "w_kv_up": _kv_up_weight(w_mla_uk, w_mla_uv),
        "w_out": w_out.astype(BF16), "g_mix_post": g_mix_post,
        "w_mem_q": w_mem_q.astype(BF16), "w_mem_o": w_mem_o.astype(BF16),
        "g_x_pre": g_x_pre, "g_x_post": g_x_post, "g_ff_pre": g_ff_pre,
        "w_gate": w_gate, "w_up": w_up, "w_down": w_down.astype(BF16),
        "g_ff_post": g_ff_post,
    }
    w_mem_kv = jnp.concatenate([w_mem_k, w_mem_v], axis=2).astype(BF16)

    tabs_p = _rope_tables(jnp.arange(tp))
    tabs_s = tuple(jnp.tile(t, (bs, 1)) for t in _rope_tables(past_len + jnp.arange(ts)))
    tiles_p = (1024, 256, 512, 512, 512, 512, 256)
    tiles_s = (bs * ts, ts, ts, 0, ts, 0, ts)

    xp = x_prompt.reshape(bp * tp, d)
    xs = x_sample.reshape(bs * ts, d)
    hp = rmsnorm_bf16(xp, g_mix_pre[0], tiles_p[1])
    hs = rmsnorm_bf16(xs, g_mix_pre[0], tiles_s[1])
    mem_n = mem_prompt.reshape(bp * mem_len, d)
    mem_k_s = cache_mem_k.reshape(depth, bs * mem_len, mem_w).astype(BF16)
    mem_v_s = cache_mem_v.reshape(depth, bs * mem_len, mem_w).astype(BF16)
    past = {"diff_k": jnp.transpose(cache_diff_k, (0, 1, 3, 2, 4)).astype(BF16),
            "diff_v": jnp.transpose(cache_diff_v, (0, 1, 3, 2, 4)).astype(BF16),
            "ckv": cache_mla_ckv, "kpe": cache_mla_kpe}

    state_p = state_s = None
    pools_p, pools_s, mem_ks, mem_vs = [], [], [], []
    for l in range(depth):
        g_next = g_mix_pre[l + 1] if l + 1 < depth else None
        m = rmsnorm_bf16(mem_n, g_mem[l], 256)
        mkv = matmul(m, w_mem_kv, l, F32, 1024, 1024)
        mk, mv = mkv[:, :mem_w], mkv[:, mem_w:]
        mem_ks.append(mk.reshape(bp, mem_len, mem_heads, mem_hd))
        mem_vs.append(mv.reshape(bp, mem_len, mem_heads, mem_hd))
        xp, hp, state_p, pool_p = _mixer_and_memory(wts, l, xp, hp, bp, tp, 0, tabs_p, mk.astype(BF16),
                                                    mv.astype(BF16), None, tiles_p, state_p)
        pools_p.append(pool_p)
        xs, hs, state_s, pool_s = _mixer_and_memory(wts, l, xs, hs, bs, ts, past_len, tabs_s, mem_k_s[l],
                                                    mem_v_s[l], dict(past, pool=cache_pool[l]), tiles_s, state_s)
        pools_s.append(pool_s)
        xp, hp, xs, hs = _ffn_pair(wts, l, xp, hp, xs, hs, g_next, tiles_p, tiles_s)

    hd = 2 * DIFF_D
    shape_p = lambda a, tail: a.reshape((depth, bp, tp) + tail)
    shape_s = lambda a, tail: a.reshape((depth, bs, ts) + tail)
    return (xp.reshape(bp, tp, d), xs.reshape(bs, ts, d),
            shape_p(state_p[0], (heads_d, hd)), shape_p(state_p[1], (heads_d, hd)),
            shape_p(state_p[2], (kv_rank,)), shape_p(state_p[3], (MLA_ROPE,)),
            jnp.stack(pools_p), jnp.stack(mem_ks), jnp.stack(mem_vs),
            shape_s(state_s[0], (heads_d, hd)), shape_s(state_s[1], (heads_d, hd)),
            shape_s(state_s[2], (kv_rank,)), shape_s(state_s[3], (MLA_ROPE,)),
            jnp.stack(pools_s))
```

```python
import functools
import math

import jax
import jax.numpy as jnp
from jax import lax
from jax.experimental import pallas as pl
from jax.experimental.pallas import tpu as pltpu

F32 = jnp.float32
BF16 = jnp.bfloat16

EPS = 1e-6
CHUNK = 64
ROPE_THETA = 10000.0
POOL_WINDOWS = (2, 4, 8, 16)
POOL_HALO = 16
DIFF_D = 64
MLA_NOPE = 128
MLA_ROPE = 64
LANES = 128
MIB = 1024 * 1024
VMEM_LIMIT = 56 * MIB

_NT = (((1,), (1,)), ((), ()))


def _cparams(*sem):
    return pltpu.CompilerParams(dimension_semantics=sem, vmem_limit_bytes=VMEM_LIMIT)


def _rms(x, g):
    return x * lax.rsqrt(jnp.mean(x * x, axis=-1, keepdims=True) + EPS) * g


def _rmsnorm_kernel(x_ref, g_ref, o_ref):
    o_ref[...] = _rms(x_ref[...], g_ref[...]).astype(o_ref.dtype)


def rmsnorm_bf16(x, g, tm):
    m, d = x.shape
    return pl.pallas_call(
        _rmsnorm_kernel,
        grid=(m // tm,),
        in_specs=[pl.BlockSpec((tm, d), lambda i: (i, 0)), pl.BlockSpec((1, d), lambda i: (0, 0))],
        out_specs=pl.BlockSpec((tm, d), lambda i: (i, 0)),
        out_shape=jax.ShapeDtypeStruct((m, d), BF16),
        compiler_params=_cparams("parallel"),
        name="rmsnorm",
    )(x, g.reshape(1, d))


def _post_res_kernel(y_ref, x_ref, gp_ref, gn_ref, xo_ref, ho_ref):
    x = x_ref[...] + _rms(y_ref[...], gp_ref[...])
    xo_ref[...] = x
    ho_ref[...] = _rms(x, gn_ref[...]).astype(ho_ref.dtype)


def _post_res_last_kernel(y_ref, x_ref, gp_ref, xo_ref):
    xo_ref[...] = x_ref[...] + _rms(y_ref[...], gp_ref[...])


def post_res(y, x, g_post, g_next, tm):
    m, d = x.shape
    row = pl.BlockSpec((tm, d), lambda i: (i, 0))
    vec = pl.BlockSpec((1, d), lambda i: (0, 0))
    if g_next is None:
        return pl.pallas_call(
            _post_res_last_kernel, grid=(m // tm,), in_specs=[row, row, vec], out_specs=row,
            out_shape=jax.ShapeDtypeStruct((m, d), F32), compiler_params=_cparams("parallel"),
            name="post_res_last",
        )(y, x, g_post.reshape(1, d)), None
    return pl.pallas_call(
        _post_res_kernel, grid=(m // tm,), in_specs=[row, row, vec, vec], out_specs=[row, row],
        out_shape=[jax.ShapeDtypeStruct((m, d), F32), jax.ShapeDtypeStruct((m, d), BF16)],
        compiler_params=_cparams("parallel"), name="post_res",
    )(y, x, g_post.reshape(1, d), g_next.reshape(1, d))


def _mm_kernel(x_ref, w_ref, o_ref):
    o_ref[...] = jnp.dot(x_ref[...], w_ref[...], preferred_element_type=F32).astype(o_ref.dtype)


def matmul(x, w, l, out_dtype, tm, tn):
    m, k = x.shape
    n = w.shape[2]
    tm, tn = min(tm, m), min(tn, n)
    return pl.pallas_call(
        _mm_kernel, grid=(n // tn, m // tm),
        in_specs=[pl.BlockSpec((tm, k), lambda j, i: (i, 0)), pl.BlockSpec((None, k, tn), lambda j, i: (l, 0, j))],
        out_specs=pl.BlockSpec((tm, tn), lambda j, i: (i, j)),
        out_shape=jax.ShapeDtypeStruct((m, n), out_dtype),
        compiler_params=_cparams("parallel", "parallel"), name="matmul",
    )(x, w)


def _mm_cat_kernel(*refs):
    n_in = (len(refs) - 1) // 2
    acc = jnp.dot(refs[0][...], refs[n_in][...], preferred_element_type=F32)
    for a in range(1, n_in):
        acc += jnp.dot(refs[a][...], refs[n_in + a][...], preferred_element_type=F32)
    refs[-1][...] = acc.astype(refs[-1].dtype)


def matmul_cat(xs, w, l, out_dtype, tm, tn):
    m = xs[0].shape[0]
    n = w.shape[2]
    tm, tn = min(tm, m), min(tn, n)
    x_specs, w_specs, off = [], [], 0
    for x in xs:
        k = x.shape[1]
        assert off % k == 0, "each input's row band of w must start on a multiple of its own width"
        x_specs.append(pl.BlockSpec((tm, k), lambda j, i: (i, 0)))
        w_specs.append(pl.BlockSpec((None, k, tn), functools.partial(lambda j, i, r: (l, r, j), r=off // k)))
        off += k
    return pl.pallas_call(
        _mm_cat_kernel, grid=(n // tn, m // tm),
        in_specs=x_specs + w_specs,
        out_specs=pl.BlockSpec((tm, tn), lambda j, i: (i, j)),
        out_shape=jax.ShapeDtypeStruct((m, n), out_dtype),
        compiler_params=_cparams("parallel", "parallel"), name="matmul_cat",
    )(*xs, *([w] * len(xs)))


def _swiglu_kernel(x_ref, wg_ref, wu_ref, o_ref):
    x = x_ref[...]
    g = jnp.dot(x, wg_ref[...].astype(BF16), preferred_element_type=F32)
    u = jnp.dot(x, wu_ref[...].astype(BF16), preferred_element_type=F32)
    o_ref[...] = (g * jax.nn.sigmoid(g) * u).astype(o_ref.dtype)


def matmul_swiglu(x, wg, wu, l, tm, tn):
    m, k = x.shape
    n = wg.shape[2]
    tm = min(tm, m)
    wspec = pl.BlockSpec((None, k, tn), lambda i, j: (l, 0, j))
    return pl.pallas_call(
        _swiglu_kernel, grid=(m // tm, n // tn),
        in_specs=[pl.BlockSpec((tm, k), lambda i, j: (i, 0), pipeline_mode=pl.Buffered(1)), wspec, wspec],
        out_specs=pl.BlockSpec((tm, tn), lambda i, j: (i, j)),
        out_shape=jax.ShapeDtypeStruct((m, n), BF16),
        compiler_params=_cparams("parallel", "parallel"), name="matmul_swiglu",
    )(x, wg, wu)


def _swiglu_pair_kernel(x_ref, wg_ref, wu_ref, xs_ref, o_ref, os_ref):
    wg, wu = wg_ref[...].astype(BF16), wu_ref[...].astype(BF16)

    def gated(x):
        g = jnp.dot(x, wg, preferred_element_type=F32)
        u = jnp.dot(x, wu, preferred_element_type=F32)
        return (g * jax.nn.sigmoid(g) * u).astype(BF16)

    o_ref[...] = gated(x_ref[...])

    @pl.when(pl.program_id(0) == 0)
    def _():
        os_ref[...] = gated(xs_ref[...])


def matmul_swiglu_pair(x, xs, wg, wu, l, tm, tn):
    m, k = x.shape
    ms = xs.shape[0]
    n = wg.shape[2]
    tm = min(tm, m)
    nj = n // tn
    wspec = pl.BlockSpec((None, k, tn), lambda i, j: (l, 0, j))
    once = pl.Buffered(1)
    return pl.pallas_call(
        _swiglu_pair_kernel, grid=(m // tm, nj),
        in_specs=[pl.BlockSpec((tm, k), lambda i, j: (i, 0), pipeline_mode=once), wspec, wspec,
                  pl.BlockSpec((ms, k), lambda i, j: (0, 0), pipeline_mode=once)],
        out_specs=[pl.BlockSpec((tm, tn), lambda i, j: (i, j)),
                   pl.BlockSpec((ms, tn), lambda i, j: (0, jnp.where(i == 0, j, nj - 1)))],
        out_shape=[jax.ShapeDtypeStruct((m, n), BF16), jax.ShapeDtypeStruct((ms, n), BF16)],
        compiler_params=_cparams("arbitrary", "arbitrary"), name="matmul_swiglu_pair",
    )(x, wg, wu, xs)


def _mm_pair_kernel(x_ref, w_ref, xs_ref, o_ref, os_ref):
    o_ref[...] = jnp.dot(x_ref[...], w_ref[...], preferred_element_type=F32).astype(o_ref.dtype)

    @pl.when(pl.program_id(1) == 0)
    def _():
        os_ref[...] = jnp.dot(xs_ref[...], w_ref[...], preferred_element_type=F32).astype(os_ref.dtype)


def matmul_pair(x, xs, w, l, out_dtype, tm, tn):
    m, k = x.shape
    ms = xs.shape[0]
    n = w.shape[2]
    tm, tn = min(tm, m), min(tn, n)
    return pl.pallas_call(
        _mm_pair_kernel, grid=(n // tn, m // tm),
        in_specs=[pl.BlockSpec((tm, k), lambda j, i: (i, 0)), pl.BlockSpec((None, k, tn), lambda j, i: (l, 0, j)),
                  pl.BlockSpec((ms, k), lambda j, i: (0, 0), pipeline_mode=pl.Buffered(1))],
        out_specs=[pl.BlockSpec((tm, tn), lambda j, i: (i, j)), pl.BlockSpec((ms, tn), lambda j, i: (0, j))],
        out_shape=[jax.ShapeDtypeStruct((m, n), out_dtype), jax.ShapeDtypeStruct((ms, n), out_dtype)],
        compiler_params=_cparams("arbitrary", "arbitrary"), name="matmul_pair",
    )(x, w, xs)


def _q_proj_kernel(x_ref, w_ref, cos_ref, sa_ref, sb_ref, o_ref, *, scale):
    acc = jnp.dot(x_ref[...], w_ref[...], preferred_element_type=F32)
    cos, sa, sb = cos_ref[...], sa_ref[...], sb_ref[...]
    for s in range(acc.shape[1] // MLA_KEY_W):
        lo, mid, hi = s * MLA_KEY_W, s * MLA_KEY_W + MLA_NOPE, (s + 1) * MLA_KEY_W
        o_ref[:, lo:mid] = (acc[:, lo:mid] * scale).astype(o_ref.dtype)
        o_ref[:, mid:hi] = (_rope_chunk(acc[:, mid:hi], cos, sa, sb) * scale).astype(o_ref.dtype)


def mla_query_projection(cqn, w_slots, tabs, l, tm, tn):
    m, k = cqn.shape
    n = w_slots.shape[2]
    t_tab = tabs[0].shape[0]
    tm, tn = min(tm, t_tab), min(tn, n)
    nt = t_tab // tm
    tab = pl.BlockSpec((tm, LANES), lambda j, i: (i % nt, 0))
    return pl.pallas_call(
        functools.partial(_q_proj_kernel, scale=(MLA_NOPE + MLA_ROPE) ** -0.5),
        grid=(n // tn, m // tm),
        in_specs=[pl.BlockSpec((tm, k), lambda j, i: (i, 0)), pl.BlockSpec((None, k, tn), lambda j, i: (l, 0, j)),
                  tab, tab, tab],
        out_specs=pl.BlockSpec((tm, tn), lambda j, i: (i, j)),
        out_shape=jax.ShapeDtypeStruct((m, n), BF16),
        compiler_params=_cparams("parallel", "parallel"), name="mla_query_projection",
    )(cqn, w_slots, *tabs)


def _rope_chunk(x, cos, sa, sb):
    return x * cos + pltpu.roll(x, LANES - 32, 1) * sa + pltpu.roll(x, 32, 1) * sb


def _rope_cols(x, cos, sa, sb):
    return jnp.concatenate([_rope_chunk(x[:, c * LANES:(c + 1) * LANES], cos, sa, sb)
                            for c in range(x.shape[1] // LANES)], axis=1)


_IN_GROUPS = 5
_IN_INPUTS = 6


def _in_proj_kernel(*refs, q_scale):
    x_ref, w_ref, cos_ref, sa_ref, sb_ref, gq_ref = refs[:_IN_INPUTS]
    u_ref, qd_ref, kd32_ref, kd16_ref, vd32_ref, vd16_ref, cq_ref, acc_a, acc_b = refs[-9:]
    accs = (acc_a, acc_b)
    j = pl.program_id(1)
    rope = lambda a: _rope_cols(a, cos_ref[...], sa_ref[...], sb_ref[...])

    def finish(g, acc):
        if g == 0:
            u_ref[...] = acc
        elif g == 1:
            qd_ref[...] = (rope(acc) * q_scale).astype(BF16)
        elif g == 2:
            k = rope(acc)
            kd32_ref[...] = k
            kd16_ref[...] = k.astype(BF16)
        elif g == 3:
            vd32_ref[...] = acc
            vd16_ref[...] = acc.astype(BF16)
        else:
            cq_ref[...] = _rms(acc, gq_ref[...]).astype(BF16)

    for g in range(_IN_GROUPS + 1):
        @pl.when(j == g)
        def _(g=g):
            if g < _IN_GROUPS:
                accs[g % 2][...] = jnp.dot(x_ref[...], w_ref[...], preferred_element_type=F32)
            if g > 0:
                finish(g - 1, accs[(g - 1) % 2][...])


_LAT_INPUTS = 6


def _in_proj_latent_kernel(*refs):
    x_ref, w_ref, cos_ref, sa_ref, sb_ref, gkv_ref = refs[:_LAT_INPUTS]
    ckv32_ref, ckv16_ref, kpe32_ref, kpe16_ref, kc16_ref = refs[-5:]
    rank = ckv32_ref.shape[1]
    acc = jnp.dot(x_ref[...], w_ref[...], preferred_element_type=F32)
    ckv = _rms(acc[:, 0:rank], gkv_ref[...])
    ckv32_ref[...] = ckv
    ckv16_ref[...] = ckv.astype(BF16)
    lane = lax.broadcasted_iota(jnp.int32, (acc.shape[0], LANES), 1)
    kpe_raw = jnp.where(lane < MLA_ROPE, acc[:, rank:rank + LANES], 0.0)
    kpe_chunk = _rope_chunk(kpe_raw, cos_ref[...], sa_ref[...], sb_ref[...])
    kpe = kpe_chunk[:, :MLA_ROPE]
    kpe32_ref[...] = kpe
    kpe16_ref[...] = kpe.astype(BF16)
    kc16_ref[:, 0:rank] = ckv.astype(BF16)
    kc16_ref[:, rank:rank + LANES] = kpe_chunk.astype(BF16)


def input_projection(h, w_in, tabs, g_q, g_kv, dims, tm, tm_lat, l, depth, state):
    m, d = h.shape
    pool_w, diff_w, q_rank, kv_rank = dims
    gw = diff_w
    lw = kv_rank + LANES
    assert pool_w == gw and q_rank == gw and w_in.shape[2] == _IN_GROUPS * gw + kv_rank + MLA_ROPE
    assert (_IN_GROUPS * gw) % lw == 0, "the latent columns must start on a multiple of their block width"
    t_tab = tabs[0].shape[0]
    tm, tm_lat = min(tm, t_tab), min(tm_lat, t_tab)
    state = (None, None) if state is None else ((state[0], state[1]), (state[2], state[3]))
    sds = lambda w, dt: jax.ShapeDtypeStruct((m, w), dt)
    stk = lambda w: jax.ShapeDtypeStruct((depth, m, w), F32)
    alias = pl.BlockSpec(memory_space=pl.ANY)

    nt = t_tab // tm
    row = lambda w: pl.BlockSpec((tm, w), lambda i, j: (i, 0))
    lay = lambda w: pl.BlockSpec((None, tm, w), lambda i, j: (l, i, 0))
    tab = pl.BlockSpec((tm, LANES), lambda i, j: (i % nt, 0))
    prev = () if state[0] is None else state[0]
    u, qd, kd32, kd16, vd32, vd16, cqn = pl.pallas_call(
        functools.partial(_in_proj_kernel, q_scale=DIFF_D ** -0.5),
        grid=(m // tm, _IN_GROUPS + 1),
        in_specs=[pl.BlockSpec((tm, d), lambda i, j: (jnp.minimum(i + j // _IN_GROUPS, m // tm - 1), 0)),
                  pl.BlockSpec((None, d, gw), lambda i, j: (l, 0, j % _IN_GROUPS)),
                  tab, tab, tab, pl.BlockSpec((1, gw), lambda i, j: (0, 0))] + [alias] * len(prev),
        out_specs=[row(gw), row(gw), lay(gw), row(gw), lay(gw), row(gw), row(gw)],
        out_shape=[sds(gw, F32), sds(gw, BF16), stk(gw), sds(gw, BF16), stk(gw), sds(gw, BF16), sds(gw, BF16)],
        scratch_shapes=[pltpu.VMEM((tm, gw), F32), pltpu.VMEM((tm, gw), F32)],
        input_output_aliases={_IN_INPUTS + a: o for a, o in enumerate((2, 4)[:len(prev)])},
        compiler_params=_cparams("parallel", "arbitrary"), name="in_proj",
    )(h, w_in, *tabs, g_q.reshape(1, gw), *prev)

    nt = t_tab // tm_lat
    row = lambda w: pl.BlockSpec((tm_lat, w), lambda i: (i, 0))
    lay = lambda w: pl.BlockSpec((None, tm_lat, w), lambda i: (l, i, 0))
    tab = pl.BlockSpec((tm_lat, LANES), lambda i: (i % nt, 0))
    prev = () if state[1] is None else state[1]
    ckv32, ckv16, kpe32, kpe16, kc16 = pl.pallas_call(
        _in_proj_latent_kernel,
        grid=(m // tm_lat,),
        in_specs=[pl.BlockSpec((tm_lat, d), lambda i: (i, 0)),
                  pl.BlockSpec((None, d, lw), lambda i: (l, 0, _IN_GROUPS * gw // lw)),
                  tab, tab, tab, pl.BlockSpec((1, kv_rank), lambda i: (0, 0))] + [alias] * len(prev),
        out_specs=[lay(kv_rank), row(kv_rank), lay(MLA_ROPE), row(MLA_ROPE), row(lw)],
        out_shape=[stk(kv_rank), sds(kv_rank, BF16), stk(MLA_ROPE), sds(MLA_ROPE, BF16), sds(lw, BF16)],
        input_output_aliases={_LAT_INPUTS + a: o for a, o in enumerate((0, 2)[:len(prev)])},
        compiler_params=_cparams("parallel"), name="in_proj_latent",
    )(h, w_in, *tabs, g_kv.reshape(1, kv_rank), *prev)
    return (u, qd, kd16, vd16, cqn, ckv16, kpe16, kc16), (kd32, vd32, ckv32, kpe32)


def _pool_kernel(z_ref, halo_ref, prev_ref, wp_ref, sc_ref, o_ref, ext_ref, *, tt, pos0):
    i = pl.program_id(1)
    ext_ref[0:POOL_HALO, :] = jnp.where(i == 0, prev_ref[0], halo_ref[...])
    ext_ref[POOL_HALO:POOL_HALO + tt, :] = z_ref[...]
    pos = pos0 + i * tt + lax.broadcasted_iota(jnp.int32, (tt, 1), 0)
    group = wp_ref.shape[1]
    for g, w in enumerate(POOL_WINDOWS):
        cs = slice(g * group, (g + 1) * group)
        s = ext_ref[POOL_HALO:POOL_HALO + tt, cs]
        u = s
        for j in range(1, w):
            s = s + ext_ref[POOL_HALO - j:POOL_HALO - j + tt, cs]
        cnt = jnp.minimum(w, pos + 1).astype(F32)
        d = s / cnt - u
        y = jnp.dot(d.astype(BF16), wp_ref[g], preferred_element_type=F32) * sc_ref[:, cs]
        o_ref[:, cs] = y.astype(o_ref.dtype)


def pool_mix(z, prev, w_pool, l, scale, batch, t_len, pos0, tt):
    m = z.shape[0]
    c = prev.shape[2]
    nt = t_len // tt
    hb = tt // POOL_HALO
    return pl.pallas_call(
        functools.partial(_pool_kernel, tt=tt, pos0=pos0),
        grid=(batch, nt),
        in_specs=[pl.BlockSpec((tt, c), lambda b, i: (b * nt + i, 0)),
                  pl.BlockSpec((POOL_HALO, c), lambda b, i: (jnp.maximum((b * nt + i) * hb - 1, 0), 0)),
                  pl.BlockSpec((1, POOL_HALO, c), lambda b, i: (b, 0, 0)),
                  pl.BlockSpec((None,) + w_pool.shape[1:], lambda b, i: (l, 0, 0, 0)),
                  pl.BlockSpec((1, c), lambda b, i: (0, 0))],
        out_specs=pl.BlockSpec((tt, c), lambda b, i: (b * nt + i, 0)),
        out_shape=jax.ShapeDtypeStruct((m, c), BF16),
        scratch_shapes=[pltpu.VMEM((POOL_HALO + tt, c), F32)],
        compiler_params=_cparams("parallel", "parallel"), name="pool_mix",
    )(z, z, prev, w_pool, scale.reshape(1, c))


def _last_kv_block(i, tq, tkv, q0, s_len):
    last_q = q0 + (i + 1) * tq - 1
    visible = jnp.minimum((last_q // CHUNK + 1) * CHUNK, s_len)
    return (visible - 1) // tkv


def _mask_bias(q_start, tq, k_start, tkv, s_len):
    qpos = q_start + lax.broadcasted_iota(jnp.int32, (tq, 1), 0)
    kpos = k_start + lax.broadcasted_iota(jnp.int32, (1, tkv), 1)
    shift = CHUNK.bit_length() - 1
    ok = (jnp.right_shift(kpos, shift) <= jnp.right_shift(qpos, shift)) & (kpos < s_len)
    return jnp.where(ok, 0.0, -jnp.inf).astype(F32)


def _add_bias(s, bias, groups):
    rows, tkv = s.shape
    return (s.reshape(groups, rows // groups, tkv) + bias[None]).reshape(rows, tkv)


def _lane_tile(x, width):
    return x if width == LANES else jnp.concatenate([x] * (width // LANES), axis=1)


def _flash_update(s, v, m_ref, l_ref, acc_ref):
    tkv = s.shape[1]
    m_prev = m_ref[...]
    m_new = jnp.maximum(m_prev, jnp.max(s, axis=-1, keepdims=True))
    alpha = jnp.exp(m_prev - m_new)
    p = jnp.exp(s - _lane_tile(m_new, tkv))
    psum = p[:, :LANES]
    for c in range(1, tkv // LANES):
        psum = psum + p[:, c * LANES:(c + 1) * LANES]
    l_ref[...] = alpha * l_ref[...] + psum
    acc_ref[...] = (_lane_tile(alpha, acc_ref.shape[1]) * acc_ref[...]
                    + jnp.dot(p.astype(BF16), v, preferred_element_type=F32))
    m_ref[...] = m_new


def _flash_init(m_ref, l_ref, acc_ref):
    m_ref[...] = jnp.full(m_ref.shape, -jnp.inf, F32)
    l_ref[...] = jnp.zeros(l_ref.shape, F32)
    acc_ref[...] = jnp.zeros(acc_ref.shape, F32)


def _flash_result(l_ref, acc_ref):
    return acc_ref[...] / jnp.sum(l_ref[...], axis=-1, keepdims=True)


def _flash_scratch(groups, rows, dv):
    lead = () if groups is None else (groups,)
    return [pltpu.VMEM(lead + (rows, LANES), F32), pltpu.VMEM(lead + (rows, LANES), F32),
            pltpu.VMEM(lead + (rows, dv), F32)]


def _full_kv_blocks(i, tq, tkv, q0, s_len):
    first_q = q0 + i * tq
    return jnp.minimum((first_q // CHUNK + 1) * CHUNK, s_len) // tkv


def _causal_windows(nq, nkv, tq, tkv, t_len):
    last = functools.partial(_last_kv_block, tq=tq, tkv=tkv, q0=0, s_len=t_len)
    q_row = lambda b, i, j: b * nq + jnp.where(j <= last(i), i, jnp.minimum(i + 1, nq - 1))
    kv_row = lambda b, i, j: b * nkv + jnp.where(j <= last(i), j, 0)
    return q_row, kv_row


def _with_ones(v):
    return jnp.concatenate([v, jnp.ones((v.shape[0], LANES), v.dtype)], axis=1)


def _flash_update_wide(s, v_ones, m_ref, acc_ref):
    tkv = s.shape[1]
    m_prev = m_ref[...]
    m_new = jnp.maximum(m_prev, jnp.max(s, axis=-1, keepdims=True))
    alpha = jnp.exp(m_prev - m_new)
    p = jnp.exp((s - _lane_tile(m_new, tkv)).astype(BF16))
    acc_ref[...] = (_lane_tile(alpha, acc_ref.shape[1]) * acc_ref[...]
                    + jnp.dot(p, v_ones, preferred_element_type=F32))
    m_ref[...] = m_new


def _flash_init_wide(m_ref, acc_ref):
    m_ref[...] = jnp.full(m_ref.shape, -jnp.inf, F32)
    acc_ref[...] = jnp.zeros(acc_ref.shape, F32)


def _flash_result_wide(acc_ref):
    dv = acc_ref.shape[-1] - LANES
    assert dv == LANES
    return acc_ref[:, 0:dv] / acc_ref[:, dv:dv + LANES]


def _flash_scratch_wide(groups, rows, dv):
    return [pltpu.VMEM((groups, rows, LANES), F32), pltpu.VMEM((groups, rows, dv + LANES), F32)]


def _past_tile(p_len, cap):
    tile = max(t for t in range(LANES, cap + 1, LANES) if p_len % t == 0)
    return tile


def _diff_split_q(q_ref, qs_ref, heads, tq):
    hd = 2 * DIFF_D
    lane = lax.broadcasted_iota(jnp.int32, (tq, hd), 1)
    for h in range(heads):
        q = q_ref[:, h * hd:(h + 1) * hd]
        qs_ref[h, 0:tq, :] = jnp.where(lane < DIFF_D, q, jnp.zeros_like(q))
        qs_ref[h, tq:2 * tq, :] = jnp.where(lane >= DIFF_D, q, jnp.zeros_like(q))


def _heads_sweep(q_of, k_of, v_of, bias, groups, m_ref, acc_ref, heads):
    for h in range(heads):
        s = lax.dot_general(q_of(h), k_of(h), _NT, preferred_element_type=F32)
        if bias is not None:
            s = s + bias if groups == 1 else _add_bias(s, bias, groups)
        _flash_update_wide(s, _with_ones(v_of(h)), m_ref.at[h], acc_ref.at[h])


def _diff_finish(lam_ref, g_ref, o_ref, acc_ref, heads, tq, out_scale):
    hd = 2 * DIFF_D
    lam = lam_ref[0, 0]
    for h in range(heads):
        a = _flash_result_wide(acc_ref.at[h])
        o = a[0:tq] - lam * a[tq:2 * tq]
        o_ref[:, h * hd:(h + 1) * hd] = (_rms(o, g_ref[...]) * out_scale).astype(o_ref.dtype)


def _diff_kernel(lam_ref, q_ref, k_ref, v_ref, g_ref, o_ref, qs_ref, m_ref, acc_ref, *,
                 heads, tq, tkv, s_len, out_scale):
    i, j = pl.program_id(1), pl.program_id(2)
    hd = 2 * DIFF_D
    k_of = lambda h: k_ref[:, h * hd:(h + 1) * hd]
    v_of = lambda h: v_ref[:, h * hd:(h + 1) * hd]
    full = _full_kv_blocks(i, tq, tkv, 0, s_len)

    @pl.when(j == 0)
    def _():
        _flash_init_wide(m_ref, acc_ref)
        _diff_split_q(q_ref, qs_ref, heads, tq)

    @pl.when(j < full)
    def _():
        _heads_sweep(lambda h: qs_ref[h], k_of, v_of, None, 2, m_ref, acc_ref, heads)

    @pl.when((j >= full) & (j <= _last_kv_block(i, tq, tkv, 0, s_len)))
    def _():
        _heads_sweep(lambda h: qs_ref[h], k_of, v_of, _mask_bias(i * tq, tq, j * tkv, tkv, s_len), 2,
                     m_ref, acc_ref, heads)

    @pl.when(j == pl.num_programs(2) - 1)
    def _():
        _diff_finish(lam_ref, g_ref, o_ref, acc_ref, heads, tq, out_scale)


def diff_attention(q, k, v, lam, g_sub, batch, heads, t_len, out_scale, tq, tkv):
    hd = 2 * DIFF_D
    w = heads * hd
    nq, nkv = t_len // tq, t_len // tkv
    q_row, kv_row = _causal_windows(nq, nkv, tq, tkv, t_len)
    kvmap = lambda b, i, j: (kv_row(b, i, j), 0)
    return pl.pallas_call(
        functools.partial(_diff_kernel, heads=heads, tq=tq, tkv=tkv, s_len=t_len, out_scale=out_scale),
        grid=(batch, nq, nkv),
        in_specs=[pl.BlockSpec(memory_space=pltpu.SMEM),
                  pl.BlockSpec((tq, w), lambda b, i, j: (q_row(b, i, j), 0)),
                  pl.BlockSpec((tkv, w), kvmap),
                  pl.BlockSpec((tkv, w), kvmap),
                  pl.BlockSpec((1, hd), lambda b, i, j: (0, 0))],
        out_specs=pl.BlockSpec((tq, w), lambda b, i, j: (b * nq + i, 0)),
        out_shape=jax.ShapeDtypeStruct((batch * t_len, w), BF16),
        scratch_shapes=[pltpu.VMEM((heads, 2 * tq, hd), BF16)] + _flash_scratch_wide(heads, 2 * tq, hd),
        compiler_params=_cparams("parallel", "parallel", "arbitrary"), name="diff_attention",
    )(lam.reshape(1, 1), q, k, v, g_sub.reshape(1, hd))


def _diff_past_kernel(lam_ref, q_ref, kc_ref, vc_ref, kn_ref, vn_ref, g_ref, o_ref,
                      qs_ref, m_ref, acc_ref, *, heads, tq, tkv, p_len, s_len, out_scale):
    j = pl.program_id(1)
    n_past = pl.num_programs(1) - 1
    hd = 2 * DIFF_D

    @pl.when(j == 0)
    def _():
        _flash_init_wide(m_ref, acc_ref)
        _diff_split_q(q_ref, qs_ref, heads, tq)

    @pl.when(j < n_past)
    def _():
        _heads_sweep(lambda h: qs_ref[h], lambda h: kc_ref[h], lambda h: vc_ref[h],
                     _mask_bias(p_len, tq, j * tkv, tkv, s_len), 2, m_ref, acc_ref, heads)

    @pl.when(j == n_past)
    def _():
        _heads_sweep(lambda h: qs_ref[h], lambda h: kn_ref[:, h * hd:(h + 1) * hd],
                     lambda h: vn_ref[:, h * hd:(h + 1) * hd],
                     _mask_bias(p_len, tq, p_len, kn_ref.shape[0], s_len), 2, m_ref, acc_ref, heads)
        _diff_finish(lam_ref, g_ref, o_ref, acc_ref, heads, tq, out_scale)


def diff_attention_past(q, kc, vc, kn, vn, l, lam, g_sub, batch, heads, t_len, out_scale):
    hd = 2 * DIFF_D
    w = heads * hd
    p_len = kc.shape[3]
    tn = kn.shape[0] // batch
    tkv = _past_tile(p_len, 1024)
    n_past = p_len // tkv
    nxt = lambda b, j: jnp.where(j == n_past, jnp.minimum(b + 1, batch - 1), b)
    cache = pl.BlockSpec((None, None, heads, tkv, hd), lambda b, j: (l, nxt(b, j), 0, j % n_past, 0))
    new = pl.BlockSpec((tn, w), lambda b, j: (b, 0))
    qmap = lambda b, j: (b, 0)
    return pl.pallas_call(
        functools.partial(_diff_past_kernel, heads=heads, tq=t_len, tkv=tkv, p_len=p_len, s_len=p_len + t_len,
                          out_scale=out_scale),
        grid=(batch, n_past + 1),
        in_specs=[pl.BlockSpec(memory_space=pltpu.SMEM), pl.BlockSpec((t_len, w), qmap), cache, cache, new, new,
                  pl.BlockSpec((1, hd), lambda b, j: (0, 0))],
        out_specs=pl.BlockSpec((t_len, w), qmap),
        out_shape=jax.ShapeDtypeStruct((batch * t_len, w), BF16),
        scratch_shapes=[pltpu.VMEM((heads, 2 * t_len, hd), BF16)] + _flash_scratch_wide(heads, 2 * t_len, hd),
        compiler_params=_cparams("parallel", "arbitrary"), name="diff_attention_past",
    )(lam.reshape(1, 1), q, kc, vc, kn, vn, g_sub.reshape(1, hd))


MLA_KEY_W = 2 * MLA_NOPE


def _mla_kernel(q_ref, k_ref, v_ref, o_ref, m_ref, acc_ref, *, heads, tq, tkv, s_len):
    i, j = pl.program_id(1), pl.program_id(2)
    q_of = lambda h: q_ref[:, h * MLA_KEY_W:(h + 1) * MLA_KEY_W]
    k_of = lambda h: k_ref[:, h * MLA_KEY_W:(h + 1) * MLA_KEY_W]
    v_of = lambda h: v_ref[:, h * MLA_NOPE:(h + 1) * MLA_NOPE]
    full = _full_kv_blocks(i, tq, tkv, 0, s_len)

    @pl.when(j == 0)
    def _():
        _flash_init_wide(m_ref, acc_ref)

    @pl.when(j < full)
    def _():
        _heads_sweep(q_of, k_of, v_of, None, 1, m_ref, acc_ref, heads)

    @pl.when((j >= full) & (j <= _last_kv_block(i, tq, tkv, 0, s_len)))
    def _():
        _heads_sweep(q_of, k_of, v_of, _mask_bias(i * tq, tq, j * tkv, tkv, s_len), 1, m_ref, acc_ref, heads)

    @pl.when(j == pl.num_programs(2) - 1)
    def _():
        for h in range(heads):
            o_ref[:, h * MLA_NOPE:(h + 1) * MLA_NOPE] = _flash_result_wide(acc_ref.at[h]).astype(o_ref.dtype)


def mla_attention(q, kv_up, batch, heads, t_len, tq, tkv):
    kw, vw = heads * MLA_KEY_W, heads * MLA_NOPE
    nq, nkv = t_len // tq, t_len // tkv
    q_row, kv_row = _causal_windows(nq, nkv, tq, tkv, t_len)
    return pl.pallas_call(
        functools.partial(_mla_kernel, heads=heads, tq=tq, tkv=tkv, s_len=t_len),
        grid=(batch, nq, nkv),
        in_specs=[pl.BlockSpec((tq, kw), lambda b, i, j: (q_row(b, i, j), 0)),
                  pl.BlockSpec((tkv, kw), lambda b, i, j: (kv_row(b, i, j), 0)),
                  pl.BlockSpec((tkv, vw), lambda b, i, j: (kv_row(b, i, j), kw // vw))],
        out_specs=pl.BlockSpec((tq, vw), lambda b, i, j: (b * nq + i, 0)),
        out_shape=jax.ShapeDtypeStruct((batch * t_len, vw), BF16),
        scratch_shapes=_flash_scratch_wide(heads, tq, MLA_NOPE),
        compiler_params=_cparams("parallel", "parallel", "arbitrary"), name="mla_attention",
    )(q, kv_up, kv_up)


def _mla_past_kernel(q_ref, wuk_ref, ckvc_ref, kpec_ref, ckvn_ref, kpen_ref, wuv_ref,
                     o_ref, qlat_ref, qpe_ref, m_ref, l_ref, acc_ref, *, heads, tq, tkv, p_len, s_len):
    j = pl.program_id(1)
    n_past = pl.num_programs(1) - 1

    @pl.when(j == 0)
    def _():
        _flash_init(m_ref, l_ref, acc_ref)
        for h in range(heads):
            lo = h * MLA_KEY_W
            qlat = jnp.dot(q_ref[:, lo:lo + MLA_NOPE], wuk_ref[h], preferred_element_type=F32)
            qlat_ref[h * tq:(h + 1) * tq, :] = qlat.astype(BF16)
            qpe_ref[h * tq:(h + 1) * tq, :] = q_ref[:, lo + MLA_NOPE:lo + MLA_NOPE + MLA_ROPE]

    def sweep(ckv, kpe, k_start):
        s = (lax.dot_general(qlat_ref[...], ckv, _NT, preferred_element_type=F32)
             + lax.dot_general(qpe_ref[...], kpe, _NT, preferred_element_type=F32))
        s = _add_bias(s, _mask_bias(p_len, tq, k_start, ckv.shape[0], s_len), heads)
        _flash_update(s, ckv, m_ref, l_ref, acc_ref)

    @pl.when(j < n_past)
    def _():
        sweep(ckvc_ref[...].astype(BF16), kpec_ref[...].astype(BF16), j * tkv)

    @pl.when(j == n_past)
    def _():
        sweep(ckvn_ref[...], kpen_ref[...], p_len)
        v_w = wuv_ref.shape[2]
        o_lat = _flash_result(l_ref, acc_ref).astype(BF16)
        for h in range(heads):
            o_h = jnp.dot(o_lat[h * tq:(h + 1) * tq], wuv_ref[h], preferred_element_type=F32)
            o_ref[:, h * v_w:(h + 1) * v_w] = o_h.astype(o_ref.dtype)


def mla_attention_past(q, wuk_t, wuv, l, ckv_c, kpe_c, ckv_n, kpe_n, batch, t_len):
    _, heads, _, rank = wuk_t.shape
    v_w = wuv.shape[3]
    p_len = ckv_c.shape[2]
    tn = ckv_n.shape[0] // batch
    tkv = _past_tile(p_len, 1024)
    n_past = p_len // tkv
    rows = heads * t_len
    layer4 = lambda a: pl.BlockSpec((None,) + a.shape[1:], lambda b, j: (l, 0, 0, 0))
    nxt = lambda b, j: jnp.where(j == n_past, jnp.minimum(b + 1, batch - 1), b)
    cache = lambda w: pl.BlockSpec((None, None, tkv, w), lambda b, j: (l, nxt(b, j), j % n_past, 0))
    new = lambda w: pl.BlockSpec((tn, w), lambda b, j: (b, 0))
    return pl.pallas_call(
        functools.partial(_mla_past_kernel, heads=heads, tq=t_len, tkv=tkv, p_len=p_len, s_len=p_len + t_len),
        grid=(batch, n_past + 1),
        in_specs=[pl.BlockSpec((t_len, q.shape[1]), lambda b, j: (b, 0)), layer4(wuk_t),
                  cache(rank), cache(MLA_ROPE), new(rank), new(MLA_ROPE), layer4(wuv)],
        out_specs=pl.BlockSpec((t_len, heads * v_w), lambda b, j: (b, 0)),
        out_shape=jax.ShapeDtypeStruct((batch * t_len, heads * v_w), BF16),
        scratch_shapes=[pltpu.VMEM((rows, rank), BF16), pltpu.VMEM((rows, MLA_ROPE), BF16)]
                       + _flash_scratch(None, rows, rank),
        compiler_params=_cparams("parallel", "arbitrary"), name="mla_attention_past",
    )(q, wuk_t, ckv_c, kpe_c, ckv_n, kpe_n, wuv)


def _cross_block_kernel(y_ref, x_ref, g_mix_ref, g_pre_ref, wq_ref, k_ref, v_ref, wo_ref, g_post_ref, g_next_ref,
                        xo_ref, ho_ref, *, heads, hd):
    x1 = x_ref[...] + _rms(y_ref[...], g_mix_ref[...])
    q = jnp.dot(_rms(x1, g_pre_ref[...]).astype(BF16), wq_ref[...], preferred_element_type=F32).astype(BF16)
    outs = []
    for h in range(heads):
        sl = slice(h * hd, (h + 1) * hd)
        s = lax.dot_general(q[:, sl], k_ref[:, sl], _NT, preferred_element_type=F32) * (hd ** -0.5)
        p = jnp.exp(s - jnp.max(s, axis=-1, keepdims=True))
        o = jnp.dot(p.astype(BF16), v_ref[:, sl], preferred_element_type=F32)
        outs.append((o / jnp.sum(p, axis=-1, keepdims=True)).astype(BF16))
    y2 = jnp.dot(jnp.concatenate(outs, axis=1), wo_ref[...], preferred_element_type=F32)
    x2 = x1 + _rms(y2, g_post_ref[...])
    xo_ref[...] = x2
    ho_ref[...] = _rms(x2, g_next_ref[...]).astype(ho_ref.dtype)


def cross_block(y_mix, x, g_mix_post, g_pre, wq, mk, mv, wo, l, g_post, g_next, batch, t_len, mem_len, heads, tm):
    m, d = x.shape
    w = wq.shape[2]
    nq = t_len // tm
    row = pl.BlockSpec((tm, d), lambda b, i: (b * nq + i, 0))
    vec = pl.BlockSpec((1, d), lambda b, i: (0, 0))
    kv = pl.BlockSpec((mem_len, w), lambda b, i: (b, 0))
    once = pl.Buffered(1)
    return pl.pallas_call(
        functools.partial(_cross_block_kernel, heads=heads, hd=w // heads),
        grid=(batch, nq),
        in_specs=[row, row, vec, vec,
                  pl.BlockSpec((None, d, w), lambda b, i: (l, 0, 0), pipeline_mode=once), kv, kv,
                  pl.BlockSpec((None, w, d), lambda b, i: (l, 0, 0), pipeline_mode=once), vec, vec],
        out_specs=[row, row],
        out_shape=[jax.ShapeDtypeStruct((m, d), F32), jax.ShapeDtypeStruct((m, d), BF16)],
        compiler_params=_cparams("parallel", "parallel"), name="cross_block",
    )(y_mix, x, g_mix_post.reshape(1, d), g_pre.reshape(1, d), wq, mk, mv, wo,
      g_post.reshape(1, d), g_next.reshape(1, d))


def _rope_tables(pos):
    half = DIFF_D // 2
    inv = ROPE_THETA ** (-jnp.arange(half, dtype=F32) / half)
    ang = pos.astype(F32)[:, None] * inv[None, :]
    cos, sin, zero = jnp.cos(ang), jnp.sin(ang), jnp.zeros_like(ang)
    reps = LANES // DIFF_D
    return (jnp.tile(cos, (1, 2 * reps)), jnp.tile(jnp.concatenate([-sin, zero], 1), (1, reps)),
            jnp.tile(jnp.concatenate([zero, sin], 1), (1, reps)))


def _round_up(a, b):
    return -(-a // b) * b


def _kv_up_weight(w_uk, w_uv):
    depth, rank, heads, nope = w_uk.shape
    k_rows = jnp.concatenate([w_uk, jnp.zeros_like(w_uk)], axis=3).reshape(depth, rank, heads * MLA_KEY_W)
    slot = jnp.concatenate([jnp.zeros((MLA_ROPE, nope), F32), jnp.eye(MLA_ROPE, dtype=F32),
                            jnp.zeros((MLA_ROPE, MLA_KEY_W - nope - MLA_ROPE), F32)], axis=1)
    rope_rows = jnp.broadcast_to(jnp.tile(slot, (1, heads)), (depth, MLA_ROPE, heads * MLA_KEY_W))
    pad = LANES - MLA_ROPE
    w_k = jnp.concatenate([k_rows, rope_rows, jnp.zeros((depth, pad, heads * MLA_KEY_W), F32)], axis=1)
    w_v = jnp.concatenate([w_uv.reshape(depth, rank, -1), jnp.zeros((depth, LANES, heads * w_uv.shape[3]), F32)],
                          axis=1)
    return jnp.concatenate([w_k, w_v], axis=2).astype(BF16)


def _pad_rows(a, batch, rows):
    t_len = a.shape[0] // batch
    a = jnp.pad(a.reshape(batch, t_len, a.shape[1]), ((0, 0), (0, rows - t_len), (0, 0)))
    return a.reshape(batch * rows, a.shape[2])


def _mixer_and_memory(wts, l, x, h, batch, t_len, pos0, tabs, mem_k, mem_v, past, tiles, state):
    tm, tt, tq_diff, tkv_diff, tq_mla, tkv_mla, tq_x = tiles
    lam_init = 0.8 - 0.6 * math.exp(-0.3 * l)
    pool_w, diff_w, q_rank, kv_rank = wts["dims"]
    heads_d = diff_w // (2 * DIFF_D)
    heads_m = wts["w_uk_t"].shape[1]

    (u, qd, kd16, vd16, cqn, ckv16, kpe16, kc16), state = input_projection(
        h, wts["w_in"], tabs, wts["g_mla_q"][l], wts["g_mla_kv"][l], wts["dims"],
        min(tm, 512), tm, l, wts["depth"], state)

    keep = POOL_HALO - 1
    if past is None:
        prev = jnp.zeros((batch, POOL_HALO, pool_w), F32)
    else:
        prev = jnp.concatenate([jnp.zeros((batch, 1, pool_w), F32), past["pool"]], axis=1)
    u3 = u.reshape(batch, t_len, pool_w)
    if t_len >= keep:
        new_pool = u3[:, t_len - keep:]
    else:
        new_pool = jnp.concatenate([prev[:, 1 + t_len:], u3], axis=1)
    y_pool = pool_mix(u, prev, wts["w_pool"], l, wts["pool_scale"][l], batch, t_len, pos0, tt)

    lq = wts["diff_lambda"][l]
    lam = jnp.exp(jnp.sum(lq[0] * lq[1])) - jnp.exp(jnp.sum(lq[2] * lq[3])) + lam_init
    q = mla_query_projection(cqn, wts["w_uq"], tabs, l, tm, 2048)
    if past is None:
        o_diff = diff_attention(qd, kd16, vd16, lam, wts["g_diff_sub"][l], batch, heads_d, t_len,
                                1.0 - lam_init, tq_diff, tkv_diff)
        kv_up = matmul(kc16, wts["w_kv_up"], l, BF16, tm, heads_m * MLA_NOPE)
        o_mla = mla_attention(q, kv_up, batch, heads_m, t_len, tq_mla, tkv_mla)
    else:
        tn = _round_up(t_len, LANES)
        o_diff = diff_attention_past(qd, past["diff_k"], past["diff_v"], _pad_rows(kd16, batch, tn),
                                     _pad_rows(vd16, batch, tn), l, lam, wts["g_diff_sub"][l], batch, heads_d,
                                     t_len, 1.0 - lam_init)
        o_mla = mla_attention_past(q, wts["w_uk_t"], wts["w_uv"], l, past["ckv"], past["kpe"],
                                   _pad_rows(ckv16, batch, tn), _pad_rows(kpe16, batch, tn), batch, t_len)

    y = matmul_cat([y_pool, o_diff, o_mla], wts["w_out"], l, F32, tm, 1024)
    x, h = cross_block(y, x, wts["g_mix_post"][l], wts["g_x_pre"][l], wts["w_mem_q"], mem_k, mem_v,
                       wts["w_mem_o"], l, wts["g_x_post"][l], wts["g_ff_pre"][l], batch, t_len,
                       mem_k.shape[0] // batch, wts["mem_heads"], tq_x)
    return x, h, state, new_pool


def _ffn_pair(wts, l, x, h, xs, hs, g_next, tiles, tiles_s):
    a, a_s = matmul_swiglu_pair(h, hs, wts["w_gate"], wts["w_up"], l, 2 * tiles[0], 256)
    y, y_s = matmul_pair(a, a_s, wts["w_down"], l, F32, 512, 512)
    x, h = post_res(y, x, wts["g_ff_post"][l], g_next, tiles[1])
    xs, hs = post_res(y_s, xs, wts["g_ff_post"][l], g_next, tiles_s[1])
    return x, h, xs, hs


def kernel(x_prompt, x_sample, cache_diff_k, cache_diff_v, cache_mla_ckv, cache_mla_kpe, cache_pool, cache_mem_k, cache_mem_v, mem_prompt, g_mix_pre, w_in, w_pool, pool_scale, diff_lambda, g_diff_sub, g_mla_q, w_mla_uq, w_mla_uk, w_mla_uv, g_mla_kv, w_out, g_mix_post, g_mem, w_mem_k, w_mem_v, w_mem_q, w_mem_o, g_x_pre, g_x_post, g_ff_pre, w_gate, w_up, w_down, g_ff_post):
    depth = w_in.shape[0]
    bp, tp, d = x_prompt.shape
    bs, ts, _ = x_sample.shape
    past_len = cache_mla_ckv.shape[2]
    pool_w = cache_pool.shape[3]
    heads_d, diff_w = cache_diff_k.shape[3], cache_diff_k.shape[3] * cache_diff_k.shape[4]
    q_rank, kv_rank = g_mla_q.shape[1], g_mla_kv.shape[1]
    mla_heads = w_mla_uk.shape[2]
    mem_len, mem_heads, mem_hd = cache_mem_k.shape[2:]
    mem_w = mem_heads * mem_hd

    uq = w_mla_uq.reshape(depth, q_rank, mla_heads, MLA_NOPE + MLA_ROPE)
    wts = {
        "depth": depth, "dims": (pool_w, diff_w, q_rank, kv_rank), "mem_heads": mem_heads,
        "w_in": w_in.astype(BF16),
        "w_pool": w_pool.astype(BF16), "pool_scale": pool_scale, "diff_lambda": diff_lambda,
        "g_diff_sub": g_diff_sub, "g_mla_q": g_mla_q, "g_mla_kv": g_mla_kv,
        "w_uq": jnp.pad(uq, ((0, 0), (0, 0), (0, 0), (0, MLA_KEY_W - MLA_NOPE - MLA_ROPE))).reshape(
            depth, q_rank, mla_heads * MLA_KEY_W).astype(BF16),
        "w_uk_t": jnp.transpose(w_mla_uk, (0, 2, 3, 1)).astype(BF16),
        "w_uv": jnp.transpose(w_mla_uv, (0, 2, 1, 3)).astype(BF16),
        "w_kv_up": _kv_up_weight(w_mla_uk, w_mla_uv),
        "w_out": w_out.astype(BF16), "g_mix_post": g_mix_post,
        "w_mem_q": w_mem_q.astype(BF16), "w_mem_o": w_mem_o.astype(BF16),
        "g_x_pre": g_x_pre, "g_x_post": g_x_post, "g_ff_pre": g_ff_pre,
        "w_gate": w_gate, "w_up": w_up, "w_down": w_down.astype(BF16),
        "g_ff_post": g_ff_post,
    }
    w_mem_kv = jnp.concatenate([w_mem_k, w_mem_v], axis=2).astype(BF16)

    tabs_p = _rope_tables(jnp.arange(tp))
    tabs_s = tuple(jnp.tile(t, (bs, 1)) for t in _rope_tables(past_len + jnp.arange(ts)))
    tiles_p = (1024, 256, 512, 512, 512, 512, 256)
    tiles_s = (bs * ts, ts, ts, 0, ts, 0, ts)

    xp = x_prompt.reshape(bp * tp, d)
    xs = x_sample.reshape(bs * ts, d)
    hp = rmsnorm_bf16(xp, g_mix_pre[0], tiles_p[1])
    hs = rmsnorm_bf16(xs, g_mix_pre[0], tiles_s[1])
    mem_n = mem_prompt.reshape(bp * mem_len, d)
    mem_k_s = cache_mem_k.reshape(depth, bs * mem_len, mem_w).astype(BF16)
    mem_v_s = cache_mem_v.reshape(depth, bs * mem_len, mem_w).astype(BF16)
    past = {"diff_k": jnp.transpose(cache_diff_k, (0, 1, 3, 2, 4)).astype(BF16),
            "diff_v": jnp.transpose(cache_diff_v, (0, 1, 3, 2, 4)).astype(BF16),
            "ckv": cache_mla_ckv, "kpe": cache_mla_kpe}

    state_p = state_s = None
    pools_p, pools_s, mem_ks, mem_vs = [], [], [], []
    for l in range(depth):
        g_next = g_mix_pre[l + 1] if l + 1 < depth else None
        m = rmsnorm_bf16(mem_n, g_mem[l], 256)
        mkv = matmul(m, w_mem_kv, l, F32, 1024, 1024)
        mk, mv = mkv[:, :mem_w], mkv[:, mem_w:]
        mem_ks.append(mk.reshape(bp, mem_len, mem_heads, mem_hd))
        mem_vs.append(mv.reshape(bp, mem_len, mem_heads, mem_hd))
        xp, hp, state_p, pool_p = _mixer_and_memory(wts, l, xp, hp, bp, tp, 0, tabs_p, mk.astype(BF16),
                                                    mv.astype(BF16), None, tiles_p, state_p)
        pools_p.append(pool_p)
        xs, hs, state_s, pool_s = _mixer_and_memory(wts, l, xs, hs, bs, ts, past_len, tabs_s, mem_k_s[l],
                                                    mem_v_s[l], dict(past, pool=cache_pool[l]), tiles_s, state_s)
        pools_s.append(pool_s)
        xp, hp, xs, hs = _ffn_pair(wts, l, xp, hp, xs, hs, g_next, tiles_p, tiles_s)

    hd = 2 * DIFF_D
    shape_p = lambda a, tail: a.reshape((depth, bp, tp) + tail)
    shape_s = lambda a, tail: a.reshape((depth, bs, ts) + tail)
    return (xp.reshape(bp, tp, d), xs.reshape(bs, ts, d),
            shape_p(state_p[0], (heads_d, hd)), shape_p(state_p[1], (heads_d, hd)),
            shape_p(state_p[2], (kv_rank,)), shape_p(state_p[3], (MLA_ROPE,)),
            jnp.stack(pools_p), jnp.stack(mem_ks), jnp.stack(mem_vs),
            shape_s(state_s[0], (heads_d, hd)), shape_s(state_s[1], (heads_d, hd)),
            shape_s(state_s[2], (kv_rank,)), shape_s(state_s[3], (MLA_ROPE,)),
            jnp.stack(pools_s))
```

```python
import functools
import math

import jax
import jax.numpy as jnp
from jax import lax
from jax.experimental import pallas as pl
from jax.experimental.pallas import tpu as pltpu

F32 = jnp.float32
BF16 = jnp.bfloat16

EPS = 1e-6
CHUNK = 64
ROPE_THETA = 10000.0
POOL_WINDOWS = (2, 4, 8, 16)
POOL_HALO = 16
DIFF_D = 64
MLA_NOPE = 128
MLA_ROPE = 64
LANES = 128
MIB = 1024 * 1024
VMEM_LIMIT = 56 * MIB

_NT = (((1,), (1,)), ((), ()))


def _cparams(*sem):
    return pltpu.CompilerParams(dimension_semantics=sem, vmem_limit_bytes=VMEM_LIMIT)


def _rms(x, g):
    return x * lax.rsqrt(jnp.mean(x * x, axis=-1, keepdims=True) + EPS) * g


def _rmsnorm_kernel(x_ref, g_ref, o_ref):
    o_ref[...] = _rms(x_ref[...], g_ref[...]).astype(o_ref.dtype)


def rmsnorm_bf16(x, g, tm):
    m, d = x.shape
    return pl.pallas_call(
        _rmsnorm_kernel,
        grid=(m // tm,),
        in_specs=[pl.BlockSpec((tm, d), lambda i: (i, 0)), pl.BlockSpec((1, d), lambda i: (0, 0))],
        out_specs=pl.BlockSpec((tm, d), lambda i: (i, 0)),
        out_shape=jax.ShapeDtypeStruct((m, d), BF16),
        compiler_params=_cparams("parallel"),
        name="rmsnorm",
    )(x, g.reshape(1, d))


def _post_res_kernel(y_ref, x_ref, gp_ref, gn_ref, xo_ref, ho_ref):
    x = x_ref[...] + _rms(y_ref[...], gp_ref[...])
    xo_ref[...] = x
    ho_ref[...] = _rms(x, gn_ref[...]).astype(ho_ref.dtype)


def _post_res_last_kernel(y_ref, x_ref, gp_ref, xo_ref):
    xo_ref[...] = x_ref[...] + _rms(y_ref[...], gp_ref[...])


def post_res(y, x, g_post, g_next, tm):
    m, d = x.shape
    row = pl.BlockSpec((tm, d), lambda i: (i, 0))
    vec = pl.BlockSpec((1, d), lambda i: (0, 0))
    if g_next is None:
        return pl.pallas_call(
            _post_res_last_kernel, grid=(m // tm,), in_specs=[row, row, vec], out_specs=row,
            out_shape=jax.ShapeDtypeStruct((m, d), F32), compiler_params=_cparams("parallel"),
            name="post_res_last",
        )(y, x, g_post.reshape(1, d)), None
    return pl.pallas_call(
        _post_res_kernel, grid=(m // tm,), in_specs=[row, row, vec, vec], out_specs=[row, row],
        out_shape=[jax.ShapeDtypeStruct((m, d), F32), jax.ShapeDtypeStruct((m, d), BF16)],
        compiler_params=_cparams("parallel"), name="post_res",
    )(y, x, g_post.reshape(1, d), g_next.reshape(1, d))


def _mm_kernel(x_ref, w_ref, o_ref):
    o_ref[...] = jnp.dot(x_ref[...], w_ref[...], preferred_element_type=F32).astype(o_ref.dtype)


def matmul(x, w, l, out_dtype, tm, tn):
    m, k = x.shape
    n = w.shape[2]
    tm, tn = min(tm, m), min(tn, n)
    return pl.pallas_call(
        _mm_kernel, grid=(n // tn, m // tm),
        in_specs=[pl.BlockSpec((tm, k), lambda j, i: (i, 0)), pl.BlockSpec((None, k, tn), lambda j, i: (l, 0, j))],
        out_specs=pl.BlockSpec((tm, tn), lambda j, i: (i, j)),
        out_shape=jax.ShapeDtypeStruct((m, n), out_dtype),
        compiler_params=_cparams("parallel", "parallel"), name="matmul",
    )(x, w)


def _mm_cat_kernel(*refs):
    n_in = (len(refs) - 1) // 2
    acc = jnp.dot(refs[0][...], refs[n_in][...], preferred_element_type=F32)
    for a in range(1, n_in):
        acc += jnp.dot(refs[a][...], refs[n_in + a][...], preferred_element_type=F32)
    refs[-1][...] = acc.astype(refs[-1].dtype)


def matmul_cat(xs, w, l, out_dtype, tm, tn):
    m = xs[0].shape[0]
    n = w.shape[2]
    tm, tn = min(tm, m), min(tn, n)
    x_specs, w_specs, off = [], [], 0
    for x in xs:
        k = x.shape[1]
        assert off % k == 0, "each input's row band of w must start on a multiple of its own width"
        x_specs.append(pl.BlockSpec((tm, k), lambda j, i: (i, 0)))
        w_specs.append(pl.BlockSpec((None, k, tn), functools.partial(lambda j, i, r: (l, r, j), r=off // k)))
        off += k
    return pl.pallas_call(
        _mm_cat_kernel, grid=(n // tn, m // tm),
        in_specs=x_specs + w_specs,
        out_specs=pl.BlockSpec((tm, tn), lambda j, i: (i, j)),
        out_shape=jax.ShapeDtypeStruct((m, n), out_dtype),
        compiler_params=_cparams("parallel", "parallel"), name="matmul_cat",
    )(*xs, *([w] * len(xs)))


def _swiglu_kernel(x_ref, wg_ref, wu_ref, o_ref):
    x = x_ref[...]
    g = jnp.dot(x, wg_ref[...].astype(BF16), preferred_element_type=F32)
    u = jnp.dot(x, wu_ref[...].astype(BF16), preferred_element_type=F32)
    o_ref[...] = (g * jax.nn.sigmoid(g) * u).astype(o_ref.dtype)


def matmul_swiglu(x, wg, wu, l, tm, tn):
    m, k = x.shape
    n = wg.shape[2]
    tm = min(tm, m)
    wspec = pl.BlockSpec((None, k, tn), lambda i, j: (l, 0, j))
    return pl.pallas_call(
        _swiglu_kernel, grid=(m // tm, n // tn),
        in_specs=[pl.BlockSpec((tm, k), lambda i, j: (i, 0), pipeline_mode=pl.Buffered(1)), wspec, wspec],
        out_specs=pl.BlockSpec((tm, tn), lambda i, j: (i, j)),
        out_shape=jax.ShapeDtypeStruct((m, n), BF16),
        compiler_params=_cparams("parallel", "parallel"), name="matmul_swiglu",
    )(x, wg, wu)


def _swiglu_pair_kernel(x_ref, wg_ref, wu_ref, xs_ref, o_ref, os_ref):
    wg, wu = wg_ref[...].astype(BF16), wu_ref[...].astype(BF16)

    def gated(x):
        g = jnp.dot(x, wg, preferred_element_type=F32)
        u = jnp.dot(x, wu, preferred_element_type=F32)
        return (g * jax.nn.sigmoid(g) * u).astype(BF16)

    o_ref[...] = gated(x_ref[...])

    @pl.when(pl.program_id(0) == 0)
    def _():
        os_ref[...] = gated(xs_ref[...])


def matmul_swiglu_pair(x, xs, wg, wu, l, tm, tn):
    m, k = x.shape
    ms = xs.shape[0]
    n = wg.shape[2]
    tm = min(tm, m)
    nj = n // tn
    wspec = pl.BlockSpec((None, k, tn), lambda i, j: (l, 0, j))
    once = pl.Buffered(1)
    return pl.pallas_call(
        _swiglu_pair_kernel, grid=(m // tm, nj),
        in_specs=[pl.BlockSpec((tm, k), lambda i, j: (i, 0), pipeline_mode=once), wspec, wspec,
                  pl.BlockSpec((ms, k), lambda i, j: (0, 0), pipeline_mode=once)],
        out_specs=[pl.BlockSpec((tm, tn), lambda i, j: (i, j)),
                   pl.BlockSpec((ms, tn), lambda i, j: (0, jnp.where(i == 0, j, nj - 1)))],
        out_shape=[jax.ShapeDtypeStruct((m, n), BF16), jax.ShapeDtypeStruct((ms, n), BF16)],
        compiler_params=_cparams("arbitrary", "arbitrary"), name="matmul_swiglu_pair",
    )(x, wg, wu, xs)


def _mm_pair_kernel(x_ref, w_ref, xs_ref, o_ref, os_ref):
    o_ref[...] = jnp.dot(x_ref[...], w_ref[...], preferred_element_type=F32).astype(o_ref.dtype)

    @pl.when(pl.program_id(1) == 0)
    def _():
        os_ref[...] = jnp.dot(xs_ref[...], w_ref[...], preferred_element_type=F32).astype(os_ref.dtype)


def matmul_pair(x, xs, w, l, out_dtype, tm, tn):
    m, k = x.shape
    ms = xs.shape[0]
    n = w.shape[2]
    tm, tn = min(tm, m), min(tn, n)
    return pl.pallas_call(
        _mm_pair_kernel, grid=(n // tn, m // tm),
        in_specs=[pl.BlockSpec((tm, k), lambda j, i: (i, 0)), pl.BlockSpec((None, k, tn), lambda j, i: (l, 0, j)),
                  pl.BlockSpec((ms, k), lambda j, i: (0, 0), pipeline_mode=pl.Buffered(1))],
        out_specs=[pl.BlockSpec((tm, tn), lambda j, i: (i, j)), pl.BlockSpec((ms, tn), lambda j, i: (0, j))],
        out_shape=[jax.ShapeDtypeStruct((m, n), out_dtype), jax.ShapeDtypeStruct((ms, n), out_dtype)],
        compiler_params=_cparams("arbitrary", "arbitrary"), name="matmul_pair",
    )(x, w, xs)


def _q_proj_kernel(x_ref, w_ref, cos_ref, sa_ref, sb_ref, o_ref, *, scale):
    acc = jnp.dot(x_ref[...], w_ref[...], preferred_element_type=F32)
    cos, sa, sb = cos_ref[...], sa_ref[...], sb_ref[...]
    for s in range(acc.shape[1] // MLA_KEY_W):
        lo, mid, hi = s * MLA_KEY_W, s * MLA_KEY_W + MLA_NOPE, (s + 1) * MLA_KEY_W
        o_ref[:, lo:mid] = (acc[:, lo:mid] * scale).astype(o_ref.dtype)
        o_ref[:, mid:hi] = (_rope_chunk(acc[:, mid:hi], cos, sa, sb) * scale).astype(o_ref.dtype)


def mla_query_projection(cqn, w_slots, tabs, l, tm, tn):
    m, k = cqn.shape
    n = w_slots.shape[2]
    t_tab = tabs[0].shape[0]
    tm, tn = min(tm, t_tab), min(tn, n)
    nt = t_tab // tm
    tab = pl.BlockSpec((tm, LANES), lambda j, i: (i % nt, 0))
    return pl.pallas_call(
        functools.partial(_q_proj_kernel, scale=(MLA_NOPE + MLA_ROPE) ** -0.5),
        grid=(n // tn, m // tm),
        in_specs=[pl.BlockSpec((tm, k), lambda j, i: (i, 0)), pl.BlockSpec((None, k, tn), lambda j, i: (l, 0, j)),
                  tab, tab, tab],
        out_specs=pl.BlockSpec((tm, tn), lambda j, i: (i, j)),
        out_shape=jax.ShapeDtypeStruct((m, n), BF16),
        compiler_params=_cparams("parallel", "parallel"), name="mla_query_projection",
    )(cqn, w_slots, *tabs)


def _rope_chunk(x, cos, sa, sb):
    return x * cos + pltpu.roll(x, LANES - 32, 1) * sa + pltpu.roll(x, 32, 1) * sb


def _rope_cols(x, cos, sa, sb):
    return jnp.concatenate([_rope_chunk(x[:, c * LANES:(c + 1) * LANES], cos, sa, sb)
                            for c in range(x.shape[1] // LANES)], axis=1)


_IN_GROUPS = 5
_IN_INPUTS = 6


def _in_proj_kernel(*refs, q_scale):
    x_ref, w_ref, cos_ref, sa_ref, sb_ref, gq_ref = refs[:_IN_INPUTS]
    u_ref, qd_ref, kd32_ref, kd16_ref, vd32_ref, vd16_ref, cq_ref, acc_a, acc_b = refs[-9:]
    accs = (acc_a, acc_b)
    j = pl.program_id(1)
    rope = lambda a: _rope_cols(a, cos_ref[...], sa_ref[...], sb_ref[...])

    def finish(g, acc):
        if g == 0:
            u_ref[...] = acc
        elif g == 1:
            qd_ref[...] = (rope(acc) * q_scale).astype(BF16)
        elif g == 2:
            k = rope(acc)
            kd32_ref[...] = k
            kd16_ref[...] = k.astype(BF16)
        elif g == 3:
            vd32_ref[...] = acc
            vd16_ref[...] = acc.astype(BF16)
        else:
            cq_ref[...] = _rms(acc, gq_ref[...]).astype(BF16)

    for g in range(_IN_GROUPS + 1):
        @pl.when(j == g)
        def _(g=g):
            if g < _IN_GROUPS:
                accs[g % 2][...] = jnp.dot(x_ref[...], w_ref[...], preferred_element_type=F32)
            if g > 0:
                finish(g - 1, accs[(g - 1) % 2][...])


_LAT_INPUTS = 6


def _in_proj_latent_kernel(*refs):
    x_ref, w_ref, cos_ref, sa_ref, sb_ref, gkv_ref = refs[:_LAT_INPUTS]
    ckv32_ref, ckv16_ref, kpe32_ref, kpe16_ref, kc16_ref = refs[-5:]
    rank = ckv32_ref.shape[1]
    acc = jnp.dot(x_ref[...], w_ref[...], preferred_element_type=F32)
    ckv = _rms(acc[:, 0:rank], gkv_ref[...])
    ckv32_ref[...] = ckv
    ckv16_ref[...] = ckv.astype(BF16)
    lane = lax.broadcasted_iota(jnp.int32, (acc.shape[0], LANES), 1)
    kpe_raw = jnp.where(lane < MLA_ROPE, acc[:, rank:rank + LANES], 0.0)
    kpe_chunk = _rope_chunk(kpe_raw, cos_ref[...], sa_ref[...], sb_ref[...])
    kpe = kpe_chunk[:, :MLA_ROPE]
    kpe32_ref[...] = kpe
    kpe16_ref[...] = kpe.astype(BF16)
    kc16_ref[:, 0:rank] = ckv.astype(BF16)
    kc16_ref[:, rank:rank + LANES] = kpe_chunk.astype(BF16)


def input_projection(h, w_in, tabs, g_q, g_kv, dims, tm, tm_lat, l, depth, state):
    m, d = h.shape
    pool_w, diff_w, q_rank, kv_rank = dims
    gw = diff_w
    lw = kv_rank + LANES
    assert pool_w == gw and q_rank == gw and w_in.shape[2] == _IN_GROUPS * gw + kv_rank + MLA_ROPE
    assert (_IN_GROUPS * gw) % lw == 0, "the latent columns must start on a multiple of their block width"
    t_tab = tabs[0].shape[0]
    tm, tm_lat = min(tm, t_tab), min(tm_lat, t_tab)
    state = (None, None) if state is None else ((state[0], state[1]), (state[2], state[3]))
    sds = lambda w, dt: jax.ShapeDtypeStruct((m, w), dt)
    stk = lambda w: jax.ShapeDtypeStruct((depth, m, w), F32)
    alias = pl.BlockSpec(memory_space=pl.ANY)

    nt = t_tab // tm
    row = lambda w: pl.BlockSpec((tm, w), lambda i, j: (i, 0))
    lay = lambda w: pl.BlockSpec((None, tm, w), lambda i, j: (l, i, 0))
    tab = pl.BlockSpec((tm, LANES), lambda i, j: (i % nt, 0))
    prev = () if state[0] is None else state[0]
    u, qd, kd32, kd16, vd32, vd16, cqn = pl.pallas_call(
        functools.partial(_in_proj_kernel, q_scale=DIFF_D ** -0.5),
        grid=(m // tm, _IN_GROUPS + 1),
        in_specs=[pl.BlockSpec((tm, d), lambda i, j: (jnp.minimum(i + j // _IN_GROUPS, m // tm - 1), 0)),
                  pl.BlockSpec((None, d, gw), lambda i, j: (l, 0, j % _IN_GROUPS)),
                  tab, tab, tab, pl.BlockSpec((1, gw), lambda i, j: (0, 0))] + [alias] * len(prev),
        out_specs=[row(gw), row(gw), lay(gw), row(gw), lay(gw), row(gw), row(gw)],
        out_shape=[sds(gw, F32), sds(gw, BF16), stk(gw), sds(gw, BF16), stk(gw), sds(gw, BF16), sds(gw, BF16)],
        scratch_shapes=[pltpu.VMEM((tm, gw), F32), pltpu.VMEM((tm, gw), F32)],
        input_output_aliases={_IN_INPUTS + a: o for a, o in enumerate((2, 4)[:len(prev)])},
        compiler_params=_cparams("parallel", "arbitrary"), name="in_proj",
    )(h, w_in, *tabs, g_q.reshape(1, gw), *prev)

    nt = t_tab // tm_lat
    row = lambda w: pl.BlockSpec((tm_lat, w), lambda i: (i, 0))
    lay = lambda w: pl.BlockSpec((None, tm_lat, w), lambda i: (l, i, 0))
    tab = pl.BlockSpec((tm_lat, LANES), lambda i: (i % nt, 0))
    prev = () if state[1] is None else state[1]
    ckv32, ckv16, kpe32, kpe16, kc16 = pl.pallas_call(
        _in_proj_latent_kernel,
        grid=(m // tm_lat,),
        in_specs=[pl.BlockSpec((tm_lat, d), lambda i: (i, 0)),
                  pl.BlockSpec((None, d, lw), lambda i: (l, 0, _IN_GROUPS * gw // lw)),
                  tab, tab, tab, pl.BlockSpec((1, kv_rank), lambda i: (0, 0))] + [alias] * len(prev),
        out_specs=[lay(kv_rank), row(kv_rank), lay(MLA_ROPE), row(MLA_ROPE), row(lw)],
        out_shape=[stk(kv_rank), sds(kv_rank, BF16), stk(MLA_ROPE), sds(MLA_ROPE, BF16), sds(lw, BF16)],
        input_output_aliases={_LAT_INPUTS + a: o for a, o in enumerate((0, 2)[:len(prev)])},
        compiler_params=_cparams("parallel"), name="in_proj_latent",
    )(h, w_in, *tabs, g_kv.reshape(1, kv_rank), *prev)
    return (u, qd, kd16, vd16, cqn, ckv16, kpe16, kc16), (kd32, vd32, ckv32, kpe32)


def _pool_kernel(z_ref, halo_ref, prev_ref, wp_ref, sc_ref, o_ref, ext_ref, *, tt, pos0):
    i = pl.program_id(1)
    ext_ref[0:POOL_HALO, :] = jnp.where(i == 0, prev_ref[0], halo_ref[...])
    ext_ref[POOL_HALO:POOL_HALO + tt, :] = z_ref[...]
    pos = pos0 + i * tt + lax.broadcasted_iota(jnp.int32, (tt, 1), 0)
    group = wp_ref.shape[1]
    for g, w in enumerate(POOL_WINDOWS):
        cs = slice(g * group, (g + 1) * group)
        s = ext_ref[POOL_HALO:POOL_HALO + tt, cs]
        u = s
        for j in range(1, w):
            s = s + ext_ref[POOL_HALO - j:POOL_HALO - j + tt, cs]
        cnt = jnp.minimum(w, pos + 1).astype(F32)
        d = s / cnt - u
        y = jnp.dot(d.astype(BF16), wp_ref[g], preferred_element_type=F32) * sc_ref[:, cs]
        o_ref[:, cs] = y.astype(o_ref.dtype)


def pool_mix(z, prev, w_pool, l, scale, batch, t_len, pos0, tt):
    m = z.shape[0]
    c = prev.shape[2]
    nt = t_len // tt
    hb = tt // POOL_HALO
    return pl.pallas_call(
        functools.partial(_pool_kernel, tt=tt, pos0=pos0),
        grid=(batch, nt),
        in_specs=[pl.BlockSpec((tt, c), lambda b, i: (b * nt + i, 0)),
                  pl.BlockSpec((POOL_HALO, c), lambda b, i: (jnp.maximum((b * nt + i) * hb - 1, 0), 0)),
                  pl.BlockSpec((1, POOL_HALO, c), lambda b, i: (b, 0, 0)),
                  pl.BlockSpec((None,) + w_pool.shape[1:], lambda b, i: (l, 0, 0, 0)),
                  pl.BlockSpec((1, c), lambda b, i: (0, 0))],
        out_specs=pl.BlockSpec((tt, c), lambda b, i: (b * nt + i, 0)),
        out_shape=jax.ShapeDtypeStruct((m, c), BF16),
        scratch_shapes=[pltpu.VMEM((POOL_HALO + tt, c), F32)],
        compiler_params=_cparams("parallel", "parallel"), name="pool_mix",
    )(z, z, prev, w_pool, scale.reshape(1, c))


def _last_kv_block(i, tq, tkv, q0, s_len):
    last_q = q0 + (i + 1) * tq - 1
    visible = jnp.minimum((last_q // CHUNK + 1) * CHUNK, s_len)
    return (visible - 1) // tkv


def _mask_bias(q_start, tq, k_start, tkv, s_len):
    qpos = q_start + lax.broadcasted_iota(jnp.int32, (tq, 1), 0)
    kpos = k_start + lax.broadcasted_iota(jnp.int32, (1, tkv), 1)
    shift = CHUNK.bit_length() - 1
    ok = (jnp.right_shift(kpos, shift) <= jnp.right_shift(qpos, shift)) & (kpos < s_len)
    return jnp.where(ok, 0.0, -jnp.inf).astype(F32)


def _add_bias(s, bias, groups):
    rows, tkv = s.shape
    return (s.reshape(groups, rows // groups, tkv) + bias[None]).reshape(rows, tkv)


def _lane_tile(x, width):
    return x if width == LANES else jnp.concatenate([x] * (width // LANES), axis=1)


def _flash_update(s, v, m_ref, l_ref, acc_ref):
    tkv = s.shape[1]
    m_prev = m_ref[...]
    m_new = jnp.maximum(m_prev, jnp.max(s, axis=-1, keepdims=True))
    alpha = jnp.exp(m_prev - m_new)
    p = jnp.exp(s - _lane_tile(m_new, tkv))
    psum = p[:, :LANES]
    for c in range(1, tkv // LANES):
        psum = psum + p[:, c * LANES:(c + 1) * LANES]
    l_ref[...] = alpha * l_ref[...] + psum
    acc_ref[...] = (_lane_tile(alpha, acc_ref.shape[1]) * acc_ref[...]
                    + jnp.dot(p.astype(BF16), v, preferred_element_type=F32))
    m_ref[...] = m_new


def _flash_init(m_ref, l_ref, acc_ref):
    m_ref[...] = jnp.full(m_ref.shape, -jnp.inf, F32)
    l_ref[...] = jnp.zeros(l_ref.shape, F32)
    acc_ref[...] = jnp.zeros(acc_ref.shape, F32)


def _flash_result(l_ref, acc_ref):
    return acc_ref[...] / jnp.sum(l_ref[...], axis=-1, keepdims=True)


def _flash_scratch(groups, rows, dv):
    lead = () if groups is None else (groups,)
    return [pltpu.VMEM(lead + (rows, LANES), F32), pltpu.VMEM(lead + (rows, LANES), F32),
            pltpu.VMEM(lead + (rows, dv), F32)]


def _full_kv_blocks(i, tq, tkv, q0, s_len):
    first_q = q0 + i * tq
    return jnp.minimum((first_q // CHUNK + 1) * CHUNK, s_len) // tkv


def _causal_windows(nq, nkv, tq, tkv, t_len):
    last = functools.partial(_last_kv_block, tq=tq, tkv=tkv, q0=0, s_len=t_len)
    q_row = lambda b, i, j: b * nq + jnp.where(j <= last(i), i, jnp.minimum(i + 1, nq - 1))
    kv_row = lambda b, i, j: b * nkv + jnp.where(j <= last(i), j, 0)
    return q_row, kv_row


def _with_ones(v):
    return jnp.concatenate([v, jnp.ones((v.shape[0], LANES), v.dtype)], axis=1)


def _flash_update_wide(s, v_ones, m_ref, acc_ref):
    tkv = s.shape[1]
    m_prev = m_ref[...]
    m_new = jnp.maximum(m_prev, jnp.max(s, axis=-1, keepdims=True))
    alpha = jnp.exp(m_prev - m_new)
    p = jnp.exp((s - _lane_tile(m_new, tkv)).astype(BF16))
    acc_ref[...] = (_lane_tile(alpha, acc_ref.shape[1]) * acc_ref[...]
                    + jnp.dot(p, v_ones, preferred_element_type=F32))
    m_ref[...] = m_new


def _flash_init_wide(m_ref, acc_ref):
    m_ref[...] = jnp.full(m_ref.shape, -jnp.inf, F32)
    acc_ref[...] = jnp.zeros(acc_ref.shape, F32)


def _flash_result_wide(acc_ref):
    dv = acc_ref.shape[-1] - LANES
    assert dv == LANES
    return acc_ref[:, 0:dv] / acc_ref[:, dv:dv + LANES]


def _flash_scratch_wide(groups, rows, dv):
    return [pltpu.VMEM((groups, rows, LANES), F32), pltpu.VMEM((groups, rows, dv + LANES), F32)]


def _past_tile(p_len, cap):
    tile = max(t for t in range(LANES, cap + 1, LANES) if p_len % t == 0)
    return tile


def _diff_split_q(q_ref, qs_ref, heads, tq):
    hd = 2 * DIFF_D
    lane = lax.broadcasted_iota(jnp.int32, (tq, hd), 1)
    for h in range(heads):
        q = q_ref[:, h * hd:(h + 1) * hd]
        qs_ref[h, 0:tq, :] = jnp.where(lane < DIFF_D, q, jnp.zeros_like(q))
        qs_ref[h, tq:2 * tq, :] = jnp.where(lane >= DIFF_D, q, jnp.zeros_like(q))


def _heads_sweep(q_of, k_of, v_of, bias, groups, m_ref, acc_ref, heads):
    for h in range(heads):
        s = lax.dot_general(q_of(h), k_of(h), _NT, preferred_element_type=F32)
        if bias is not None:
            s = s + bias if groups == 1 else _add_bias(s, bias, groups)
        _flash_update_wide(s, _with_ones(v_of(h)), m_ref.at[h], acc_ref.at[h])


def _diff_finish(lam_ref, g_ref, o_ref, acc_ref, heads, tq, out_scale):
    hd = 2 * DIFF_D
    lam = lam_ref[0, 0]
    for h in range(heads):
        a = _flash_result_wide(acc_ref.at[h])
        o = a[0:tq] - lam * a[tq:2 * tq]
        o_ref[:, h * hd:(h + 1) * hd] = (_rms(o, g_ref[...]) * out_scale).astype(o_ref.dtype)


def _diff_kernel(lam_ref, q_ref, k_ref, v_ref, g_ref, o_ref, qs_ref, m_ref, acc_ref, *,
                 heads, tq, tkv, s_len, out_scale):
    i, j = pl.program_id(1), pl.program_id(2)
    hd = 2 * DIFF_D
    k_of = lambda h: k_ref[:, h * hd:(h + 1) * hd]
    v_of = lambda h: v_ref[:, h * hd:(h + 1) * hd]
    full = _full_kv_blocks(i, tq, tkv, 0, s_len)

    @pl.when(j == 0)
    def _():
        _flash_init_wide(m_ref, acc_ref)
        _diff_split_q(q_ref, qs_ref, heads, tq)

    @pl.when(j < full)
    def _():
        _heads_sweep(lambda h: qs_ref[h], k_of, v_of, None, 2, m_ref, acc_ref, heads)

    @pl.when((j >= full) & (j <= _last_kv_block(i, tq, tkv, 0, s_len)))
    def _():
        _heads_sweep(lambda h: qs_ref[h], k_of, v_of, _mask_bias(i * tq, tq, j * tkv, tkv, s_len), 2,
                     m_ref, acc_ref, heads)

    @pl.when(j == pl.num_programs(2) - 1)
    def _():
        _diff_finish(lam_ref, g_ref, o_ref, acc_ref, heads, tq, out_scale)


def diff_attention(q, k, v, lam, g_sub, batch, heads, t_len, out_scale, tq, tkv):
    hd = 2 * DIFF_D
    w = heads * hd
    nq, nkv = t_len // tq, t_len // tkv
    q_row, kv_row = _causal_windows(nq, nkv, tq, tkv, t_len)
    kvmap = lambda b, i, j: (kv_row(b, i, j), 0)
    return pl.pallas_call(
        functools.partial(_diff_kernel, heads=heads, tq=tq, tkv=tkv, s_len=t_len, out_scale=out_scale),
        grid=(batch, nq, nkv),
        in_specs=[pl.BlockSpec(memory_space=pltpu.SMEM),
                  pl.BlockSpec((tq, w), lambda b, i, j: (q_row(b, i, j), 0)),
                  pl.BlockSpec((tkv, w), kvmap),
                  pl.BlockSpec((tkv, w), kvmap),
                  pl.BlockSpec((1, hd), lambda b, i, j: (0, 0))],
        out_specs=pl.BlockSpec((tq, w), lambda b, i, j: (b * nq + i, 0)),
        out_shape=jax.ShapeDtypeStruct((batch * t_len, w), BF16),
        scratch_shapes=[pltpu.VMEM((heads, 2 * tq, hd), BF16)] + _flash_scratch_wide(heads, 2 * tq, hd),
        compiler_params=_cparams("parallel", "parallel", "arbitrary"), name="diff_attention",
    )(lam.reshape(1, 1), q, k, v, g_sub.reshape(1, hd))


def _diff_past_kernel(lam_ref, q_ref, kc_ref, vc_ref, kn_ref, vn_ref, g_ref, o_ref,
                      qs_ref, m_ref, acc_ref, *, heads, tq, tkv, p_len, s_len, out_scale):
    j = pl.program_id(1)
    n_past = pl.num_programs(1) - 1
    hd = 2 * DIFF_D

    @pl.when(j == 0)
    def _():
        _flash_init_wide(m_ref, acc_ref)
        _diff_split_q(q_ref, qs_ref, heads, tq)

    @pl.when(j < n_past)
    def _():
        _heads_sweep(lambda h: qs_ref[h], lambda h: kc_ref[h], lambda h: vc_ref[h],
                     _mask_bias(p_len, tq, j * tkv, tkv, s_len), 2, m_ref, acc_ref, heads)

    @pl.when(j == n_past)
    def _():
        _heads_sweep(lambda h: qs_ref[h], lambda h: kn_ref[:, h * hd:(h + 1) * hd],
                     lambda h: vn_ref[:, h * hd:(h + 1) * hd],
                     _mask_bias(p_len, tq, p_len, kn_ref.shape[0], s_len), 2, m_ref, acc_ref, heads)
        _diff_finish(lam_ref, g_ref, o_ref, acc_ref, heads, tq, out_scale)


def diff_attention_past(q, kc, vc, kn, vn, l, lam, g_sub, batch, heads, t_len, out_scale):
    hd = 2 * DIFF_D
    w = heads * hd
    p_len = kc.shape[3]
    tn = kn.shape[0] // batch
    tkv = _past_tile(p_len, 1024)
    n_past = p_len // tkv
    nxt = lambda b, j: jnp.where(j == n_past, jnp.minimum(b + 1, batch - 1), b)
    cache = pl.BlockSpec((None, None, heads, tkv, hd), lambda b, j: (l, nxt(b, j), 0, j % n_past, 0))
    new = pl.BlockSpec((tn, w), lambda b, j: (b, 0))
    qmap = lambda b, j: (b, 0)
    return pl.pallas_call(
        functools.partial(_diff_past_kernel, heads=heads, tq=t_len, tkv=tkv, p_len=p_len, s_len=p_len + t_len,
                          out_scale=out_scale),
        grid=(batch, n_past + 1),
        in_specs=[pl.BlockSpec(memory_space=pltpu.SMEM), pl.BlockSpec((t_len, w), qmap), cache, cache, new, new,
                  pl.BlockSpec((1, hd), lambda b, j: (0, 0))],
        out_specs=pl.BlockSpec((t_len, w), qmap),
        out_shape=jax.ShapeDtypeStruct((batch * t_len, w), BF16),
        scratch_shapes=[pltpu.VMEM((heads, 2 * t_len, hd), BF16)] + _flash_scratch_wide(heads, 2 * t_len, hd),
        compiler_params=_cparams("parallel", "arbitrary"), name="diff_attention_past",
    )(lam.reshape(1, 1), q, kc, vc, kn, vn, g_sub.reshape(1, hd))


MLA_KEY_W = 2 * MLA_NOPE


def _mla_kernel(q_ref, k_ref, v_ref, o_ref, m_ref, acc_ref, *, heads, tq, tkv, s_len):
    i, j = pl.program_id(1), pl.program_id(2)
    q_of = lambda h: q_ref[:, h * MLA_KEY_W:(h + 1) * MLA_KEY_W]
    k_of = lambda h: k_ref[:, h * MLA_KEY_W:(h + 1) * MLA_KEY_W]
    v_of = lambda h: v_ref[:, h * MLA_NOPE:(h + 1) * MLA_NOPE]
    full = _full_kv_blocks(i, tq, tkv, 0, s_len)

    @pl.when(j == 0)
    def _():
        _flash_init_wide(m_ref, acc_ref)

    @pl.when(j < full)
    def _():
        _heads_sweep(q_of, k_of, v_of, None, 1, m_ref, acc_ref, heads)

    @pl.when((j >= full) & (j <= _last_kv_block(i, tq, tkv, 0, s_len)))
    def _():
        _heads_sweep(q_of, k_of, v_of, _mask_bias(i * tq, tq, j * tkv, tkv, s_len), 1, m_ref, acc_ref, heads)

    @pl.when(j == pl.num_programs(2) - 1)
    def _():
        for h in range(heads):
            o_ref[:, h * MLA_NOPE:(h + 1) * MLA_NOPE] = _flash_result_wide(acc_ref.at[h]).astype(o_ref.dtype)


def mla_attention(q, kv_up, batch, heads, t_len, tq, tkv):
    kw, vw = heads * MLA_KEY_W, heads * MLA_NOPE
    nq, nkv = t_len // tq, t_len // tkv
    q_row, kv_row = _causal_windows(nq, nkv, tq, tkv, t_len)
    return pl.pallas_call(
        functools.partial(_mla_kernel, heads=heads, tq=tq, tkv=tkv, s_len=t_len),
        grid=(batch, nq, nkv),
        in_specs=[pl.BlockSpec((tq, kw), lambda b, i, j: (q_row(b, i, j), 0)),
                  pl.BlockSpec((tkv, kw), lambda b, i, j: (kv_row(b, i, j), 0)),
                  pl.BlockSpec((tkv, vw), lambda b, i, j: (kv_row(b, i, j), kw // vw))],
        out_specs=pl.BlockSpec((tq, vw), lambda b, i, j: (b * nq + i, 0)),
        out_shape=jax.ShapeDtypeStruct((batch * t_len, vw), BF16),
        scratch_shapes=_flash_scratch_wide(heads, tq, MLA_NOPE),
        compiler_params=_cparams("parallel", "parallel", "arbitrary"), name="mla_attention",
    )(q, kv_up, kv_up)


def _mla_past_kernel(q_ref, wuk_ref, ckvc_ref, kpec_ref, ckvn_ref, kpen_ref, wuv_ref,
                     o_ref, qlat_ref, qpe_ref, m_ref, l_ref, acc_ref, *, heads, tq, tkv, p_len, s_len):
    j = pl.program_id(1)
    n_past = pl.num_programs(1) - 1

    @pl.when(j == 0)
    def _():
        _flash_init(m_ref, l_ref, acc_ref)
        for h in range(heads):
            lo = h * MLA_KEY_W
            qlat = jnp.dot(q_ref[:, lo:lo + MLA_NOPE], wuk_ref[h], preferred_element_type=F32)
            qlat_ref[h * tq:(h + 1) * tq, :] = qlat.astype(BF16)
            qpe_ref[h * tq:(h + 1) * tq, :] = q_ref[:, lo + MLA_NOPE:lo + MLA_NOPE + MLA_ROPE]

    def sweep(ckv, kpe, k_start):
        s = (lax.dot_general(qlat_ref[...], ckv, _NT, preferred_element_type=F32)
             + lax.dot_general(qpe_ref[...], kpe, _NT, preferred_element_type=F32))
        s = _add_bias(s, _mask_bias(p_len, tq, k_start, ckv.shape[0], s_len), heads)
        _flash_update(s, ckv, m_ref, l_ref, acc_ref)

    @pl.when(j < n_past)
    def _():
        sweep(ckvc_ref[...].astype(BF16), kpec_ref[...].astype(BF16), j * tkv)

    @pl.when(j == n_past)
    def _():
        sweep(ckvn_ref[...], kpen_ref[...], p_len)
        v_w = wuv_ref.shape[2]
        o_lat = _flash_result(l_ref, acc_ref).astype(BF16)
        for h in range(heads):
            o_h = jnp.dot(o_lat[h * tq:(h + 1) * tq], wuv_ref[h], preferred_element_type=F32)
            o_ref[:, h * v_w:(h + 1) * v_w] = o_h.astype(o_ref.dtype)


def mla_attention_past(q, wuk_t, wuv, l, ckv_c, kpe_c, ckv_n, kpe_n, batch, t_len):
    _, heads, _, rank = wuk_t.shape
    v_w = wuv.shape[3]
    p_len = ckv_c.shape[2]
    tn = ckv_n.shape[0] // batch
    tkv = _past_tile(p_len, 1024)
    n_past = p_len // tkv
    rows = heads * t_len
    layer4 = lambda a: pl.BlockSpec((None,) + a.shape[1:], lambda b, j: (l, 0, 0, 0))
    nxt = lambda b, j: jnp.where(j == n_past, jnp.minimum(b + 1, batch - 1), b)
    cache = lambda w: pl.BlockSpec((None, None, tkv, w), lambda b, j: (l, nxt(b, j), j % n_past, 0))
    new = lambda w: pl.BlockSpec((tn, w), lambda b, j: (b, 0))
    return pl.pallas_call(
        functools.partial(_mla_past_kernel, heads=heads, tq=t_len, tkv=tkv, p_len=p_len, s_len=p_len + t_len),
        grid=(batch, n_past + 1),
        in_specs=[pl.BlockSpec((t_len, q.shape[1]), lambda b, j: (b, 0)), layer4(wuk_t),
                  cache(rank), cache(MLA_ROPE), new(rank), new(MLA_ROPE), layer4(wuv)],
        out_specs=pl.BlockSpec((t_len, heads * v_w), lambda b, j: (b, 0)),
        out_shape=jax.ShapeDtypeStruct((batch * t_len, heads * v_w), BF16),
        scratch_shapes=[pltpu.VMEM((rows, rank), BF16), pltpu.VMEM((rows, MLA_ROPE), BF16)]
                       + _flash_scratch(None, rows, rank),
        compiler_params=_cparams("parallel", "arbitrary"), name="mla_attention_past",
    )(q, wuk_t, ckv_c, kpe_c, ckv_n, kpe_n, wuv)


def _cross_block_kernel(y_ref, x_ref, g_mix_ref, g_pre_ref, wq_ref, k_ref, v_ref, wo_ref, g_post_ref, g_next_ref,
                        xo_ref, ho_ref, *, heads, hd):
    x1 = x_ref[...] + _rms(y_ref[...], g_mix_ref[...])
    q = jnp.dot(_rms(x1, g_pre_ref[...]).astype(BF16), wq_ref[...], preferred_element_type=F32).astype(BF16)
    outs = []
    for h in range(heads):
        sl = slice(h * hd, (h + 1) * hd)
        s = lax.dot_general(q[:, sl], k_ref[:, sl], _NT, preferred_element_type=F32) * (hd ** -0.5)
        p = jnp.exp(s - jnp.max(s, axis=-1, keepdims=True))
        o = jnp.dot(p.astype(BF16), v_ref[:, sl], preferred_element_type=F32)
        outs.append((o / jnp.sum(p, axis=-1, keepdims=True)).astype(BF16))
    y2 = jnp.dot(jnp.concatenate(outs, axis=1), wo_ref[...], preferred_element_type=F32)
    x2 = x1 + _rms(y2, g_post_ref[...])
    xo_ref[...] = x2
    ho_ref[...] = _rms(x2, g_next_ref[...]).astype(ho_ref.dtype)


def cross_block(y_mix, x, g_mix_post, g_pre, wq, mk, mv, wo, l, g_post, g_next, batch, t_len, mem_len, heads, tm):
    m, d = x.shape
    w = wq.shape[2]
    nq = t_len // tm
    row = pl.BlockSpec((tm, d), lambda b, i: (b * nq + i, 0))
    vec = pl.BlockSpec((1, d), lambda b, i: (0, 0))
    kv = pl.BlockSpec((mem_len, w), lambda b, i: (b, 0))
    once = pl.Buffered(1)
    return pl.pallas_call(
        functools.partial(_cross_block_kernel, heads=heads, hd=w // heads),
        grid=(batch, nq),
        in_specs=[row, row, vec, vec,
                  pl.BlockSpec((None, d, w), lambda b, i: (l, 0, 0), pipeline_mode=once), kv, kv,
                  pl.BlockSpec((None, w, d), lambda b, i: (l, 0, 0), pipeline_mode=once), vec, vec],
        out_specs=[row, row],
        out_shape=[jax.ShapeDtypeStruct((m, d), F32), jax.ShapeDtypeStruct((m, d), BF16)],
        compiler_params=_cparams("parallel", "parallel"), name="cross_block",
    )(y_mix, x, g_mix_post.reshape(1, d), g_pre.reshape(1, d), wq, mk, mv, wo,
      g_post.reshape(1, d), g_next.reshape(1, d))


def _rope_tables(pos):
    half = DIFF_D // 2
    inv = ROPE_THETA ** (-jnp.arange(half, dtype=F32) / half)
    ang = pos.astype(F32)[:, None] * inv[None, :]
    cos, sin, zero = jnp.cos(ang), jnp.sin(ang), jnp.zeros_like(ang)
    reps = LANES // DIFF_D
    return (jnp.tile(cos, (1, 2 * reps)), jnp.tile(jnp.concatenate([-sin, zero], 1), (1, reps)),
            jnp.tile(jnp.concatenate([zero, sin], 1), (1, reps)))


def _round_up(a, b):
    return -(-a // b) * b


def _kv_up_weight(w_uk, w_uv):
    depth, rank, heads, nope = w_uk.shape
    k_rows = jnp.concatenate([w_uk, jnp.zeros_like(w_uk)], axis=3).reshape(depth, rank, heads * MLA_KEY_W)
    slot = jnp.concatenate([jnp.zeros((MLA_ROPE, nope), F32), jnp.eye(MLA_ROPE, dtype=F32),
                            jnp.zeros((MLA_ROPE, MLA_KEY_W - nope - MLA_ROPE), F32)], axis=1)
    rope_rows = jnp.broadcast_to(jnp.tile(slot, (1, heads)), (depth, MLA_ROPE, heads * MLA_KEY_W))
    pad = LANES - MLA_ROPE
    w_k = jnp.concatenate([k_rows, rope_rows, jnp.zeros((depth, pad, heads * MLA_KEY_W), F32)], axis=1)
    w_v = jnp.concatenate([w_uv.reshape(depth, rank, -1), jnp.zeros((depth, LANES, heads * w_uv.shape[3]), F32)],
                          axis=1)
    return jnp.concatenate([w_k, w_v], axis=2).astype(BF16)


def _pad_rows(a, batch, rows):
    t_len = a.shape[0] // batch
    a = jnp.pad(a.reshape(batch, t_len, a.shape[1]), ((0, 0), (0, rows - t_len), (0, 0)))
    return a.reshape(batch * rows, a.shape[2])


def _mixer_and_memory(wts, l, x, h, batch, t_len, pos0, tabs, mem_k, mem_v, past, tiles, state):
    tm, tt, tq_diff, tkv_diff, tq_mla, tkv_mla, tq_x = tiles
    lam_init = 0.8 - 0.6 * math.exp(-0.3 * l)
    pool_w, diff_w, q_rank, kv_rank = wts["dims"]
    heads_d = diff_w // (2 * DIFF_D)
    heads_m = wts["w_uk_t"].shape[1]

    (u, qd, kd16, vd16, cqn, ckv16, kpe16, kc16), state = input_projection(
        h, wts["w_in"], tabs, wts["g_mla_q"][l], wts["g_mla_kv"][l], wts["dims"],
        min(tm, 512), tm, l, wts["depth"], state)

    keep = POOL_HALO - 1
    if past is None:
        prev = jnp.zeros((batch, POOL_HALO, pool_w), F32)
    else:
        prev = jnp.concatenate([jnp.zeros((batch, 1, pool_w), F32), past["pool"]], axis=1)
    u3 = u.reshape(batch, t_len, pool_w)
    if t_len >= keep:
        new_pool = u3[:, t_len - keep:]
    else:
        new_pool = jnp.concatenate([prev[:, 1 + t_len:], u3], axis=1)
    y_pool = pool_mix(u, prev, wts["w_pool"], l, wts["pool_scale"][l], batch, t_len, pos0, tt)

    lq = wts["diff_lambda"][l]
    lam = jnp.exp(jnp.sum(lq[0] * lq[1])) - jnp.exp(jnp.sum(lq[2] * lq[3])) + lam_init
    q = mla_query_projection(cqn, wts["w_uq"], tabs, l, tm, 2048)
    if past is None:
        o_diff = diff_attention(qd, kd16, vd16, lam, wts["g_diff_sub"][l], batch, heads_d, t_len,
                                1.0 - lam_init, tq_diff, tkv_diff)
        kv_up = matmul(kc16, wts["w_kv_up"], l, BF16, 2 * tm, heads_m * MLA_NOPE)
        o_mla = mla_attention(q, kv_up, batch, heads_m, t_len, tq_mla, tkv_mla)
    else:
        tn = _round_up(t_len, LANES)
        o_diff = diff_attention_past(qd, past["diff_k"], past["diff_v"], _pad_rows(kd16, batch, tn),
                                     _pad_rows(vd16, batch, tn), l, lam, wts["g_diff_sub"][l], batch, heads_d,
                                     t_len, 1.0 - lam_init)
        o_mla = mla_attention_past(q, wts["w_uk_t"], wts["w_uv"], l, past["ckv"], past["kpe"],
                                   _pad_rows(ckv16, batch, tn), _pad_rows(kpe16, batch, tn), batch, t_len)

    y = matmul_cat([y_pool, o_diff, o_mla], wts["w_out"], l, F32, tm, 1024)
    x, h = cross_block(y, x, wts["g_mix_post"][l], wts["g_x_pre"][l], wts["w_mem_q"], mem_k, mem_v,
                       wts["w_mem_o"], l, wts["g_x_post"][l], wts["g_ff_pre"][l], batch, t_len,
                       mem_k.shape[0] // batch, wts["mem_heads"], tq_x)
    return x, h, state, new_pool


def _ffn_pair(wts, l, x, h, xs, hs, g_next, tiles, tiles_s):
    a, a_s = matmul_swiglu_pair(h, hs, wts["w_gate"], wts["w_up"], l, 2 * tiles[0], 256)
    y, y_s = matmul_pair(a, a_s, wts["w_down"], l, F32, 512, 512)
    x, h = post_res(y, x, wts["g_ff_post"][l], g_next, tiles[1])
    xs, hs = post_res(y_s, xs, wts["g_ff_post"][l], g_next, tiles_s[1])
    return x, h, xs, hs


def kernel(x_prompt, x_sample, cache_diff_k, cache_diff_v, cache_mla_ckv, cache_mla_kpe, cache_pool, cache_mem_k, cache_mem_v, mem_prompt, g_mix_pre, w_in, w_pool, pool_scale, diff_lambda, g_diff_sub, g_mla_q, w_mla_uq, w_mla_uk, w_mla_uv, g_mla_kv, w_out, g_mix_post, g_mem, w_mem_k, w_mem_v, w_mem_q, w_mem_o, g_x_pre, g_x_post, g_ff_pre, w_gate, w_up, w_down, g_ff_post):
    depth = w_in.shape[0]
    bp, tp, d = x_prompt.shape
    bs, ts, _ = x_sample.shape
    past_len = cache_mla_ckv.shape[2]
    pool_w = cache_pool.shape[3]
    heads_d, diff_w = cache_diff_k.shape[3], cache_diff_k.shape[3] * cache_diff_k.shape[4]
    q_rank, kv_rank = g_mla_q.shape[1], g_mla_kv.shape[1]
    mla_heads = w_mla_uk.shape[2]
    mem_len, mem_heads, mem_hd = cache_mem_k.shape[2:]
    mem_w = mem_heads * mem_hd

    uq = w_mla_uq.reshape(depth, q_rank, mla_heads, MLA_NOPE + MLA_ROPE)
    wts = {
        "depth": depth, "dims": (pool_w, diff_w, q_rank, kv_rank), "mem_heads": mem_heads,
        "w_in": w_in.astype(BF16),
        "w_pool": w_pool.astype(BF16), "pool_scale": pool_scale, "diff_lambda": diff_lambda,
        "g_diff_sub": g_diff_sub, "g_mla_q": g_mla_q, "g_mla_kv": g_mla_kv,
        "w_uq": jnp.pad(uq, ((0, 0), (0, 0), (0, 0), (0, MLA_KEY_W - MLA_NOPE - MLA_ROPE))).reshape(
            depth, q_rank, mla_heads * MLA_KEY_W).astype(BF16),
        "w_uk_t": jnp.transpose(w_mla_uk, (0, 2, 3, 1)).astype(BF16),
        "w_uv": jnp.transpose(w_mla_uv, (0, 2, 1, 3)).astype(BF16),
        "w_kv_up": _kv_up_weight(w_mla_uk, w_mla_uv),
        "w_out": w_out.astype(BF16), "g_mix_post": g_mix_post,
        "w_mem_q": w_mem_q.astype(BF16), "w_mem_o": w_mem_o.astype(BF16),
        "g_x_pre": g_x_pre, "g_x_post": g_x_post, "g_ff_pre": g_ff_pre,
        "w_gate": w_gate, "w_up": w_up, "w_down": w_down.astype(BF16),
        "g_ff_post": g_ff_post,
    }
    w_mem_kv = jnp.concatenate([w_mem_k, w_mem_v], axis=2).astype(BF16)

    tabs_p = _rope_tables(jnp.arange(tp))
    tabs_s = tuple(jnp.tile(t, (bs, 1)) for t in _rope_tables(past_len + jnp.arange(ts)))
    tiles_p = (1024, 256, 512, 512, 512, 512, 256)
    tiles_s = (bs * ts, ts, ts, 0, ts, 0, ts)

    xp = x_prompt.reshape(bp * tp, d)
    xs = x_sample.reshape(bs * ts, d)
    hp = rmsnorm_bf16(xp, g_mix_pre[0], 2 * tiles_p[1])
    hs = rmsnorm_bf16(xs, g_mix_pre[0], tiles_s[1])
    mem_n = mem_prompt.reshape(bp * mem_len, d)
    mem_k_s = cache_mem_k.reshape(depth, bs * mem_len, mem_w).astype(BF16)
    mem_v_s = cache_mem_v.reshape(depth, bs * mem_len, mem_w).astype(BF16)
    past = {"diff_k": jnp.transpose(cache_diff_k, (0, 1, 3, 2, 4)).astype(BF16),
            "diff_v": jnp.transpose(cache_diff_v, (0, 1, 3, 2, 4)).astype(BF16),
            "ckv": cache_mla_ckv, "kpe": cache_mla_kpe}

    state_p = state_s = None
    pools_p, pools_s, mem_ks, mem_vs = [], [], [], []
    for l in range(depth):
        g_next = g_mix_pre[l + 1] if l + 1 < depth else None
        m = rmsnorm_bf16(mem_n, g_mem[l], 256)
        mkv = matmul(m, w_mem_kv, l, F32, 1024, 1024)
        mk, mv = mkv[:, :mem_w], mkv[:, mem_w:]
        mem_ks.append(mk.reshape(bp, mem_len, mem_heads, mem_hd))
        mem_vs.append(mv.reshape(bp, mem_len, mem_heads, mem_hd))
        xp, hp, state_p, pool_p = _mixer_and_memory(wts, l, xp, hp, bp, tp, 0, tabs_p, mk.astype(BF16),
                                                    mv.astype(BF16), None, tiles_p, state_p)
        pools_p.append(pool_p)
        xs, hs, state_s, pool_s = _mixer_and_memory(wts, l, xs, hs, bs, ts, past_len, tabs_s, mem_k_s[l],
                                                    mem_v_s[l], dict(past, pool=cache_pool[l]), tiles_s, state_s)
        pools_s.append(pool_s)
        xp, hp, xs, hs = _ffn_pair(wts, l, xp, hp, xs, hs, g_next, tiles_p, tiles_s)

    hd = 2 * DIFF_D
    shape_p = lambda a, tail: a.reshape((depth, bp, tp) + tail)
    shape_s = lambda a, tail: a.reshape((depth, bs, ts) + tail)
    return (xp.reshape(bp, tp, d), xs.reshape(bs, ts, d),
            shape_p(state_p[0], (heads_d, hd)), shape_p(state_p[1], (heads_d, hd)),
            shape_p(state_p[2], (kv_rank,)), shape_p(state_p[3], (MLA_ROPE,)),
            jnp.stack(pools_p), jnp.stack(mem_ks), jnp.stack(mem_vs),
            shape_s(state_s[0], (heads_d, hd)), shape_s(state_s[1], (heads_d, hd)),
            shape_s(state_s[2], (kv_rank,)), shape_s(state_s[3], (MLA_ROPE,)),
            jnp.stack(pools_s))
```
